```python
import jax, jax.numpy as jnp
from jax import lax
import numpy as np

D_MODEL = 1024
BATCH = 8
SEQ = 4096
DEPTH = 1

N_HEADS = 8
HEAD_DIM = 64
N_KV = 2
GROUP = N_HEADS // N_KV
ROT_DIM = HEAD_DIM // 4
ROPE_THETA = 500000.0
CMP_BLOCK = 32
CMP_STRIDE = 16
CMP_HIDDEN = 4 * HEAD_DIM
SEL_BLOCK = 64
SEL_TOPK = 16
WINDOW = 512
Q_BLOCK = 128
N_NSA_BRANCHES = 3
ATTN_WIDTH = N_HEADS * HEAD_DIM
KV_WIDTH = N_KV * HEAD_DIM
POOL_WIDTH = 512
POOL_WINDOWS = (2, 4, 8, 16)
POOL_GROUP = POOL_WIDTH // len(POOL_WINDOWS)
D_FF = 4 * D_MODEL
IN_SIZES = (ATTN_WIDTH, 6 * KV_WIDTH, N_NSA_BRANCHES * N_HEADS, POOL_WIDTH, 2 * D_MODEL)
IN_WIDTH = sum(IN_SIZES)
EPS = 1e-6
NEG_INF = -1e30
FORCE_SCORE = 1e4

kernel_name = "nsa_pool_gated_hybrid_block"


def rms_norm(x, g):
    xf = x.astype(jnp.float32)
    y = xf * lax.rsqrt(jnp.mean(xf * xf, axis=-1, keepdims=True) + EPS)
    return (y * g.astype(jnp.float32)).astype(x.dtype)


def rope(x, pos):
    inv = ROPE_THETA ** (-jnp.arange(0, ROT_DIM, 2, dtype=jnp.float32) / ROT_DIM)
    ang = pos.astype(jnp.float32)[:, None] * inv
    cos, sin = jnp.cos(ang), jnp.sin(ang)
    xr = x[..., :ROT_DIM].astype(jnp.float32)
    x1, x2 = xr[..., :ROT_DIM // 2], xr[..., ROT_DIM // 2:]
    rot = jnp.concatenate([x1 * cos - x2 * sin, x2 * cos + x1 * sin], axis=-1)
    return jnp.concatenate([rot.astype(x.dtype), x[..., ROT_DIM:]], axis=-1)


def masked_softmax(s, mask):
    s = jnp.where(mask, s.astype(jnp.float32), NEG_INF)
    e = jnp.exp(s - jnp.max(s, axis=-1, keepdims=True)) * mask
    return e / jnp.maximum(jnp.sum(e, axis=-1, keepdims=True), 1e-30)


def cmp_to_sel_matrix(S):
    n_cmp = (S - CMP_BLOCK) // CMP_STRIDE + 1
    n_sel = S // SEL_BLOCK
    cs = np.arange(n_cmp)[:, None] * CMP_STRIDE
    js = np.arange(n_sel)[None, :] * SEL_BLOCK
    ov = np.clip(np.minimum(cs + CMP_BLOCK, js + SEL_BLOCK) - np.maximum(cs, js), 0, None)
    return jnp.asarray((ov / CMP_STRIDE).astype(np.float32))


def compress_blocks(k, pe, w1, w2):
    B, G, S, dh = k.shape
    ratio = CMP_BLOCK // CMP_STRIDE
    chunks = k.reshape(B, G, S // CMP_STRIDE, CMP_STRIDE, dh)
    n_cmp = S // CMP_STRIDE - ratio + 1
    blocks = jnp.concatenate([chunks[:, :, j:j + n_cmp] for j in range(ratio)], axis=3)
    flat = (blocks + pe).reshape(B, G, n_cmp, CMP_BLOCK * dh)
    return jax.nn.gelu(flat @ w1) @ w2


def nsa_attention(q_flat, kv_flat, gate_logits, pe_k, pe_v, ck_w1, ck_w2, cv_w1, cv_w2):
    B, S, _ = q_flat.shape
    dt = q_flat.dtype
    scale = HEAD_DIM ** -0.5
    t = jnp.arange(S)
    q = rope(q_flat.reshape(B, S, N_KV, GROUP, HEAD_DIM).transpose(0, 2, 3, 1, 4), t)

    def kv_heads(a):
        return a.reshape(B, S, N_KV, HEAD_DIM).transpose(0, 2, 1, 3)

    kc, vc, ks, vs, kw, vw = [kv_heads(a) for a in jnp.split(kv_flat, 6, axis=-1)]
    ks = rope(ks, t)
    kw = rope(kw, t)

    n_cmp = (S - CMP_BLOCK) // CMP_STRIDE + 1
    pos_c = jnp.arange(n_cmp) * CMP_STRIDE + CMP_BLOCK - 1
    k_cmp = rope(compress_blocks(kc, pe_k, ck_w1, ck_w2), pos_c)
    v_cmp = compress_blocks(vc, pe_v, cv_w1, cv_w2)
    s = jnp.einsum('bgrsd,bgcd->bgrsc', q, k_cmp) * scale
    p_cmp = masked_softmax(s, pos_c[None, :] <= t[:, None])
    o_cmp = jnp.einsum('bgrsc,bgcd->bgrsd', p_cmp.astype(dt), v_cmp)

    n_sel = S // SEL_BLOCK
    top = min(SEL_TOPK, n_sel)
    imp = jnp.einsum('bgrsc,cj->bgsj', p_cmp, cmp_to_sel_matrix(S))
    blk = jnp.arange(n_sel)[None, :]
    cur = (t // SEL_BLOCK)[:, None]
    imp = jnp.where(blk > cur, -FORCE_SCORE, imp)
    imp = jnp.where((blk == 0) | (blk == cur) | (blk == cur - 1), FORCE_SCORE, imp)
    _, sel_idx = lax.top_k(imp, top)

    nq = S // Q_BLOCK
    q_blocks = q.reshape(B, N_KV, GROUP, nq, Q_BLOCK, HEAD_DIM).transpose(3, 0, 1, 2, 4, 5)
    idx_blocks = sel_idx.reshape(B, N_KV, nq, Q_BLOCK, top).transpose(2, 0, 1, 3, 4)
    ks_blk = ks.reshape(B, N_KV, n_sel, SEL_BLOCK, HEAD_DIM)
    vs_blk = vs.reshape(B, N_KV, n_sel, SEL_BLOCK, HEAD_DIM)
    pad = ((0, 0), (0, 0), (WINDOW, 0), (0, 0))
    kw_pad = jnp.pad(kw, pad)
    vw_pad = jnp.pad(vw, pad)
    bi = jnp.arange(B)[:, None, None, None]
    gi = jnp.arange(N_KV)[None, :, None, None]
    offs = jnp.arange(SEL_BLOCK)
    band = WINDOW + Q_BLOCK

    def step(args):
        i, qb, ib = args
        qpos = i * Q_BLOCK + jnp.arange(Q_BLOCK)
        k_sel = ks_blk[bi, gi, ib].reshape(B, N_KV, Q_BLOCK, top * SEL_BLOCK, HEAD_DIM)
        v_sel = vs_blk[bi, gi, ib].reshape(B, N_KV, Q_BLOCK, top * SEL_BLOCK, HEAD_DIM)
        kpos = (ib[..., None] * SEL_BLOCK + offs).reshape(B, N_KV, Q_BLOCK, top * SEL_BLOCK)
        s_sel = jnp.einsum('bgrqd,bgqkd->bgrqk', qb, k_sel) * scale
        p_sel = masked_softmax(s_sel, (kpos <= qpos[:, None])[:, :, None])
        o_sel = jnp.einsum('bgrqk,bgqkd->bgrqd', p_sel.astype(dt), v_sel)
        k_win = lax.dynamic_slice_in_dim(kw_pad, i * Q_BLOCK, band, axis=2)
        v_win = lax.dynamic_slice_in_dim(vw_pad, i * Q_BLOCK, band, axis=2)
        wpos = i * Q_BLOCK - WINDOW + jnp.arange(band)
        dist = qpos[:, None] - wpos[None, :]
        wmask = (dist >= 0) & (dist < WINDOW) & (wpos >= 0)[None, :]
        s_win = jnp.einsum('bgrqd,bgkd->bgrqk', qb, k_win) * scale
        p_win = masked_softmax(s_win, wmask)
        o_win = jnp.einsum('bgrqk,bgkd->bgrqd', p_win.astype(dt), v_win)
        return o_sel, o_win

    o_sel, o_win = lax.map(step, (jnp.arange(nq), q_blocks, idx_blocks))
    o_sel = o_sel.transpose(1, 2, 3, 0, 4, 5).reshape(B, N_KV, GROUP, S, HEAD_DIM)
    o_win = o_win.transpose(1, 2, 3, 0, 4, 5).reshape(B, N_KV, GROUP, S, HEAD_DIM)

    g = jax.nn.sigmoid(gate_logits).reshape(B, S, N_NSA_BRANCHES, N_KV, GROUP).transpose(2, 0, 3, 4, 1)[..., None]
    o = g[0] * o_cmp + g[1] * o_sel + g[2] * o_win
    return o.transpose(0, 3, 1, 2, 4).reshape(B, S, ATTN_WIDTH)


def pool_mixer(u, w_pool, pool_scale):
    B, S, _ = u.shape
    uf = u.astype(jnp.float32)
    csp = jnp.concatenate([jnp.zeros((B, 1, POOL_WIDTH), jnp.float32), jnp.cumsum(uf, axis=1)], axis=1)
    t = jnp.arange(S)
    outs = []
    for gidx, w in enumerate(POOL_WINDOWS):
        lo, hi = gidx * POOL_GROUP, (gidx + 1) * POOL_GROUP
        c = csp[..., lo:hi]
        lower = jnp.concatenate([jnp.zeros((B, w - 1, POOL_GROUP), jnp.float32), c[:, :S - w + 1]], axis=1)
        cnt = jnp.minimum(t + 1, w).astype(jnp.float32)[:, None]
        outs.append((c[:, 1:] - lower) / cnt - uf[..., lo:hi])
    pooled = jnp.stack(outs, axis=2).astype(u.dtype)
    mixed = jnp.einsum('bsgc,gcd->bsgd', pooled, w_pool).reshape(B, S, POOL_WIDTH)
    return mixed * pool_scale


def hybrid_layer(x, norm_mix, w_in, cmp_pe_k, cmp_pe_v, cmp_k_w1, cmp_k_w2, cmp_v_w1, cmp_v_w2,
                 w_branch_attn, pool_w, pool_scale, w_branch_pool, w_out, norm_mlp, w_ff1, w_ff2):
    h = rms_norm(x, norm_mix)
    proj = h @ w_in
    q_flat, kv_flat, g_attn, u_pool, g_merge = jnp.split(proj, np.cumsum(IN_SIZES)[:-1].tolist(), axis=-1)
    a = nsa_attention(q_flat, kv_flat, g_attn, cmp_pe_k, cmp_pe_v, cmp_k_w1, cmp_k_w2, cmp_v_w1, cmp_v_w2) @ w_branch_attn
    b = pool_mixer(u_pool, pool_w, pool_scale) @ w_branch_pool
    ga, gb = jnp.split(jax.nn.sigmoid(g_merge), 2, axis=-1)
    x = x + (ga * a + gb * b) @ w_out
    h = rms_norm(x, norm_mlp)
    return x + jnp.square(jax.nn.relu(h @ w_ff1)) @ w_ff2


def setup_inputs(seed: int = 0) -> dict:
    key = jax.random.key(seed)
    ks = jax.random.split(key, 20)
    f32 = jnp.float32

    def nrm(k, shape, fan_in):
        return jax.random.normal(k, shape, f32) * (fan_in ** -0.5)

    def gain(k, shape):
        return 1.0 + 0.05 * jax.random.normal(k, shape, f32)

    L = DEPTH
    return {
        "x": jax.random.normal(ks[0], (BATCH, SEQ, D_MODEL), f32),
        "norm_mix": gain(ks[1], (L, D_MODEL)),
        "w_in": nrm(ks[2], (L, D_MODEL, IN_WIDTH), D_MODEL),
        "cmp_pe_k": 0.02 * jax.random.normal(ks[3], (L, CMP_BLOCK, HEAD_DIM), f32),
        "cmp_pe_v": 0.02 * jax.random.normal(ks[4], (L, CMP_BLOCK, HEAD_DIM), f32),
        "cmp_k_w1": nrm(ks[5], (L, CMP_BLOCK * HEAD_DIM, CMP_HIDDEN), CMP_BLOCK * HEAD_DIM),
        "cmp_k_w2": nrm(ks[6], (L, CMP_HIDDEN, HEAD_DIM), CMP_HIDDEN),
        "cmp_v_w1": nrm(ks[7], (L, CMP_BLOCK * HEAD_DIM, CMP_HIDDEN), CMP_BLOCK * HEAD_DIM),
        "cmp_v_w2": nrm(ks[8], (L, CMP_HIDDEN, HEAD_DIM), CMP_HIDDEN),
        "w_branch_attn": nrm(ks[9], (L, ATTN_WIDTH, D_MODEL), ATTN_WIDTH),
        "pool_w": nrm(ks[10], (L, len(POOL_WINDOWS), POOL_GROUP, POOL_GROUP), POOL_GROUP),
        "pool_scale": 1.0 + 0.1 * jax.random.normal(ks[11], (L, POOL_WIDTH), f32),
        "w_branch_pool": nrm(ks[12], (L, POOL_WIDTH, D_MODEL), POOL_WIDTH),
        "w_out": nrm(ks[13], (L, D_MODEL, D_MODEL), D_MODEL),
        "norm_mlp": gain(ks[14], (L, D_MODEL)),
        "w_ff1": nrm(ks[15], (L, D_MODEL, D_FF), D_MODEL),
        "w_ff2": nrm(ks[16], (L, D_FF, D_MODEL), D_FF),
        "norm_final": gain(ks[17], (D_MODEL,)),
    }


def reference(x, norm_mix, w_in, cmp_pe_k, cmp_pe_v, cmp_k_w1, cmp_k_w2, cmp_v_w1, cmp_v_w2,
              w_branch_attn, pool_w, pool_scale, w_branch_pool, w_out, norm_mlp, w_ff1, w_ff2,
              norm_final):
    for l in range(DEPTH):
        x = hybrid_layer(x, norm_mix[l], w_in[l], cmp_pe_k[l], cmp_pe_v[l], cmp_k_w1[l], cmp_k_w2[l],
                         cmp_v_w1[l], cmp_v_w2[l], w_branch_attn[l], pool_w[l], pool_scale[l],
                         w_branch_pool[l], w_out[l], norm_mlp[l], w_ff1[l], w_ff2[l])
    return rms_norm(x, norm_final)
```

```python
import functools

import jax
import jax.numpy as jnp
import numpy as np
from jax import lax
from jax.experimental import pallas as pl
from jax.experimental.pallas import tpu as pltpu

f32 = jnp.float32
bf16 = jnp.bfloat16

D_MODEL = 1024
N_HEADS = 8
HEAD_DIM = 64
N_KV = 2
GROUP = N_HEADS // N_KV
ROT_DIM = HEAD_DIM // 4
ROT_HALF = ROT_DIM // 2
ROPE_THETA = 500000.0
CMP_BLOCK = 32
CMP_STRIDE = 16
CMP_HIDDEN = 4 * HEAD_DIM
SEL_BLOCK = 64
SEL_SHIFT = 6
SEL_TOPK = 16
WINDOW = 512
N_BRANCH = 3
ATTN_WIDTH = N_HEADS * HEAD_DIM
KV_WIDTH = N_KV * HEAD_DIM
POOL_WIDTH = 512
POOL_WINDOWS = (2, 4, 8, 16)
POOL_GROUP = POOL_WIDTH // len(POOL_WINDOWS)
POOL_HALO = 16
D_FF = 4 * D_MODEL
EPS = 1e-6
NEG_INF = -1e30
FORCE_SCORE = 1e4
BIG = 2.0 ** 100
M_INIT = -3.0e38

LANE = 128
VMEM_LIMIT = 56 * 1024 * 1024

SEG_Q = (0, N_HEADS * LANE)
SEG_KS = (SEG_Q[1], SEG_Q[1] + N_KV * LANE)
SEG_KW = (SEG_KS[1], SEG_KS[1] + N_KV * LANE)
SEG_VS = (SEG_KW[1], SEG_KW[1] + KV_WIDTH)
SEG_VW = (SEG_VS[1], SEG_VS[1] + KV_WIDTH)
SEG_KC = (SEG_VW[1], SEG_VW[1] + KV_WIDTH)
SEG_VC = (SEG_KC[1], SEG_KC[1] + KV_WIDTH)
SEG_U = (SEG_VC[1], SEG_VC[1] + POOL_WIDTH)
SEG_GA = (SEG_U[1], SEG_U[1] + N_KV * LANE)
SEG_GM = (SEG_GA[1], SEG_GA[1] + 2 * D_MODEL)
W_COLS = SEG_GM[1]

TS_IN = 512
TQ = 256
TK = 256
TS_OUT = 256


def _dot(a, b):
    return jnp.dot(a, b, preferred_element_type=f32)


def _dot_nt(a, b):
    return lax.dot_general(a, b, (((1,), (1,)), ((), ())), preferred_element_type=f32)


def _rope_lane_tile(x, c, sa, sb):
    return (x * c + pltpu.roll(x, LANE - ROT_HALF, axis=1) * sa
            + pltpu.roll(x, ROT_HALF, axis=1) * sb)


def _rms(x, g):
    ms = jnp.mean(x * x, axis=-1, keepdims=True)
    return x * lax.rsqrt(ms + EPS) * g


def _in_proj_kernel(x_ref, g_ref, w_ref, rc_ref, rsa_ref, rsb_ref, pw_ref, ps_ref,
                    q_ref, ks_ref, kw_ref, vs_ref, vw_ref, kc_ref, vc_ref, ga_ref, pm_ref, gm_ref,
                    ubuf):
    si = pl.program_id(1)
    t0 = si * TS_IN
    h = _rms(x_ref[0], g_ref[...]).astype(bf16)

    def seg(s):
        return _dot(h, w_ref[:, s[0]:s[1]])

    rc, rsa, rsb = rc_ref[...], rsa_ref[...], rsb_ref[...]

    q = seg(SEG_Q)
    for hh in range(N_HEADS):
        sl = slice(hh * LANE, (hh + 1) * LANE)
        q_ref[0, :, sl] = (_rope_lane_tile(q[:, sl], rc, rsa, rsb) * (HEAD_DIM ** -0.5)).astype(bf16)

    row_t = t0 + lax.broadcasted_iota(jnp.int32, (TS_IN, LANE), 0)
    lane = lax.broadcasted_iota(jnp.int32, (TS_IN, LANE), 1)
    onehot = jnp.where(lane - HEAD_DIM == (row_t >> SEL_SHIFT), BIG, 0.0).astype(f32)
    ks = seg(SEG_KS)
    kw = seg(SEG_KW)
    for gg in range(N_KV):
        sl = slice(gg * LANE, (gg + 1) * LANE)
        ks_ref[0, :, sl] = (_rope_lane_tile(ks[:, sl], rc, rsa, rsb) + onehot).astype(bf16)
        kw_ref[0, :, sl] = _rope_lane_tile(kw[:, sl], rc, rsa, rsb).astype(bf16)

    vs_ref[0] = seg(SEG_VS).astype(bf16)
    vw_ref[0] = seg(SEG_VW).astype(bf16)
    kc_ref[0] = seg(SEG_KC)
    vc_ref[0] = seg(SEG_VC)
    ga_ref[0] = jax.nn.sigmoid(seg(SEG_GA))
    gm_ref[0] = jax.nn.sigmoid(seg(SEG_GM)).astype(bf16)

    u = seg(SEG_U)

    @pl.when(si == 0)
    def _():
        ubuf[0:POOL_HALO, :] = jnp.zeros((POOL_HALO, POOL_WIDTH), f32)

    ubuf[POOL_HALO:, :] = u
    tpos1 = (t0 + 1 + lax.broadcasted_iota(jnp.int32, (TS_IN, 1), 0)).astype(f32)
    for gi, w in enumerate(POOL_WINDOWS):
        sl = slice(gi * POOL_GROUP, (gi + 1) * POOL_GROUP)
        acc = u[:, sl]
        for k in range(1, w):
            acc = acc + ubuf[POOL_HALO - k:POOL_HALO - k + TS_IN, sl]
        cnt = jnp.minimum(tpos1, float(w))
        pooled = (acc / cnt - u[:, sl]).astype(bf16)
        mixed = _dot(pooled, pw_ref[gi]) * ps_ref[:, sl]
        pm_ref[0, :, sl] = mixed.astype(bf16)
    ubuf[0:POOL_HALO, :] = ubuf[TS_IN:TS_IN + POOL_HALO, :]


def _in_proj(x, g, w, rc, rsa, rsb, pw, ps):
    B, S, D = x.shape
    grid = (B, S // TS_IN)
    tok = lambda width: pl.BlockSpec((1, TS_IN, width), lambda b, s: (b, s, 0))
    const = lambda shape: pl.BlockSpec(shape, lambda b, s: (0,) * len(shape),
                                       pipeline_mode=pl.Buffered(1))
    tab = pl.BlockSpec((TS_IN, LANE), lambda b, s: (s, 0))
    out_shapes = [
        jax.ShapeDtypeStruct((B, S, N_HEADS * LANE), bf16),
        jax.ShapeDtypeStruct((B, S, N_KV * LANE), bf16),
        jax.ShapeDtypeStruct((B, S, N_KV * LANE), bf16),
        jax.ShapeDtypeStruct((B, S, KV_WIDTH), bf16),
        jax.ShapeDtypeStruct((B, S, KV_WIDTH), bf16),
        jax.ShapeDtypeStruct((B, S, KV_WIDTH), f32),
        jax.ShapeDtypeStruct((B, S, KV_WIDTH), f32),
        jax.ShapeDtypeStruct((B, S, N_KV * LANE), f32),
        jax.ShapeDtypeStruct((B, S, POOL_WIDTH), bf16),
        jax.ShapeDtypeStruct((B, S, 2 * D_MODEL), bf16),
    ]
    return pl.pallas_call(
        _in_proj_kernel,
        grid=grid,
        in_specs=[tok(D), const((1, D)), const((D, W_COLS)), tab, tab, tab,
                  const((len(POOL_WINDOWS), POOL_GROUP, POOL_GROUP)), const((1, POOL_WIDTH))],
        out_specs=[tok(s.shape[-1]) for s in out_shapes],
        out_shape=out_shapes,
        scratch_shapes=[pltpu.VMEM((POOL_HALO + TS_IN, POOL_WIDTH), f32)],
        compiler_params=pltpu.CompilerParams(
            dimension_semantics=("arbitrary", "arbitrary"), vmem_limit_bytes=VMEM_LIMIT),
        name="in_proj",
    )(x, g, w, rc, rsa, rsb, pw, ps)


def _compress_kernel(kc_ref, vc_ref, pek_ref, pev_ref, kw1_ref, kw2_ref, vw1_ref, vw2_ref,
                     rc_ref, rsa_ref, rsb_ref, kout_ref, vout_ref):
    half = CMP_STRIDE * HEAD_DIM
    n_chunk = kc_ref.shape[2]

    def hidden(src, pe_ref, w1_ref):
        lo = _dot((src + pe_ref[0:1, :]).astype(bf16), w1_ref[0:half, :])
        hi = _dot((src + pe_ref[1:2, :]).astype(bf16), w1_ref[half:, :])
        pre = lo + pltpu.roll(hi, n_chunk - 1, axis=0)
        return jax.nn.gelu(pre, approximate=True).astype(bf16)

    vacc = None
    for gg in range(N_KV):
        hk = hidden(kc_ref[0, gg], pek_ref, kw1_ref)
        kcmp = _dot(hk, kw2_ref[...])
        kout_ref[0, gg] = _rope_lane_tile(kcmp, rc_ref[...], rsa_ref[...], rsb_ref[...]).astype(bf16)
        hv = hidden(vc_ref[0, gg], pev_ref, vw1_ref)
        part = _dot(hv, vw2_ref[gg])
        vacc = part if vacc is None else vacc + part
    vout_ref[0] = vacc.astype(bf16)


def _compress(kcr, vcr, pek, pev, kw1, kw2, vw1, vw2, rc, rsa, rsb):
    B, G, NC, W = kcr.shape
    const = lambda shape: pl.BlockSpec(shape, lambda b: (0,) * len(shape))
    src = pl.BlockSpec((1, G, NC, W), lambda b: (b, 0, 0, 0))
    return pl.pallas_call(
        _compress_kernel,
        grid=(B,),
        in_specs=[src, src, const(pek.shape), const(pev.shape), const(kw1.shape), const(kw2.shape),
                  const(vw1.shape), const(vw2.shape), const(rc.shape), const(rsa.shape), const(rsb.shape)],
        out_specs=[pl.BlockSpec((1, G, NC, LANE), lambda b: (b, 0, 0, 0)),
                   pl.BlockSpec((1, NC, LANE), lambda b: (b, 0, 0))],
        out_shape=[jax.ShapeDtypeStruct((B, G, NC, LANE), bf16),
                   jax.ShapeDtypeStruct((B, NC, LANE), bf16)],
        compiler_params=pltpu.CompilerParams(
            dimension_semantics=("arbitrary",), vmem_limit_bytes=VMEM_LIMIT),
        name="compress",
    )(kcr, vcr, pek, pev, kw1, kw2, vw1, vw2, rc, rsa, rsb)


def _attn_kernel(q_ref, kc_ref, vc_ref, ks_ref, vs_ref, kw_ref, vw_ref, ga_ref, mt_ref, o_ref,
                 m_sc, l_sc, acc_sc, *, n_cmp, n_sel):
    g = pl.program_id(1)
    i = pl.program_id(2)
    t0 = i * TQ
    rows = GROUP * TQ
    ncp = kc_ref.shape[2]

    row_t = t0 + lax.broadcasted_iota(jnp.int32, (TQ, 1), 0)
    lane_t = t0 + lax.broadcasted_iota(jnp.int32, (1, TQ), 1)
    row4_t = t0 + (lax.broadcasted_iota(jnp.int32, (rows, 1), 0) & (TQ - 1))

    kc = kc_ref[0, 0]
    vc = vc_ref[0]
    cidx = lax.broadcasted_iota(jnp.int32, (1, ncp), 1)
    cvalid = ((cidx * CMP_STRIDE + (CMP_BLOCK - 1)) <= row_t) & (cidx < n_cmp)
    psum = jnp.zeros((TQ, ncp), f32)
    q_heads = []
    o_cmp = []
    for r in range(GROUP):
        qr = q_ref[0, :, r * LANE:(r + 1) * LANE]
        q_heads.append(qr)
        s = jnp.where(cvalid, _dot_nt(qr, kc), NEG_INF)
        mx = jnp.max(s, axis=-1, keepdims=True)
        e = jnp.where(cvalid, jnp.exp(s - mx), 0.0)
        den = jnp.maximum(jnp.sum(e, axis=-1, keepdims=True), 1e-30)
        p = e * (1.0 / den)
        psum = psum + p
        o_cmp.append(_dot(p.astype(bf16), vc))

    mt = mt_ref[...]
    p_hi = psum.astype(bf16)
    rem = psum - p_hi.astype(f32)
    p_mid = rem.astype(bf16)
    p_lo = (rem - p_mid.astype(f32)).astype(bf16)
    imp = _dot_nt(mt, p_hi) + _dot_nt(mt, p_mid) + _dot_nt(mt, p_lo)
    blk = lax.broadcasted_iota(jnp.int32, (n_sel, TQ), 0)
    cur = lane_t >> SEL_SHIFT
    imp = jnp.where(blk > cur, -FORCE_SCORE, imp)
    imp = jnp.where((blk == 0) | (blk == cur) | (blk == cur - 1), FORCE_SCORE, imp)
    ahead = jnp.zeros((n_sel, TQ), jnp.int32)
    for j in range(n_sel):
        row = imp[j:j + 1, :]
        wins = (row > imp) | ((row == imp) & (blk > j))
        ahead = ahead + wins.astype(jnp.int32)
    sel_m1 = jnp.where(ahead < min(SEL_TOPK, n_sel), 0.0, -1.0).astype(f32)
    sel_pad = jnp.concatenate([jnp.zeros((LANE - n_sel, TQ), f32), sel_m1], axis=0).T
    sel_pad = sel_pad.astype(bf16)
    q_aug = jnp.concatenate([qr + sel_pad for qr in q_heads], axis=0)

    def reset():
        m_sc[...] = jnp.full((rows, 1), M_INIT, f32)
        l_sc[...] = jnp.zeros((rows, 1), f32)
        acc_sc[...] = jnp.zeros((rows, LANE), f32)

    def update(s, v):
        m_prev = m_sc[...]
        m_new = jnp.maximum(m_prev, jnp.max(s, axis=-1, keepdims=True))
        alpha = jnp.exp(m_prev - m_new)
        p = jnp.exp(s - m_new)
        l_sc[...] = alpha * l_sc[...] + jnp.sum(p, axis=-1, keepdims=True)
        acc_sc[...] = alpha * acc_sc[...] + _dot(p.astype(bf16), v)
        m_sc[...] = m_new

    def finish():
        return acc_sc[...] * (1.0 / l_sc[...])

    def tile(k_ref, v_ref, kt, mode):
        k0 = pl.multiple_of(kt * TK, TK)
        s = _dot_nt(q_aug, k_ref[0, pl.ds(k0, TK), :])
        if mode is not None:
            kpos = k0 + lax.broadcasted_iota(jnp.int32, (1, TK), 1)
            keep = (kpos <= row4_t) if mode == "causal" else (row4_t - kpos < WINDOW)
            s = jnp.where(keep, s, -BIG)
        update(s, v_ref[0, pl.ds(k0, TK), :])

    reset()

    def sel_body(kt, carry):
        tile(ks_ref, vs_ref, kt, None)
        return carry

    lax.fori_loop(0, i, sel_body, 0)
    tile(ks_ref, vs_ref, i, "causal")
    o_sel = finish()

    reset()
    n_back = WINDOW // TK
    for d in range(n_back, 0, -1):
        @pl.when(i >= d)
        def _(d=d):
            tile(kw_ref, vw_ref, i - d, "far" if d == n_back else None)
    tile(kw_ref, vw_ref, i, "causal")
    o_win = finish()

    lane = lax.broadcasted_iota(jnp.int32, (TQ, LANE), 1)
    mixed = []
    for r in range(GROUP):
        rs = slice(r * TQ, (r + 1) * TQ)
        mixed.append(ga_ref[0, :, r:r + 1] * o_cmp[r]
                     + ga_ref[0, :, GROUP + r:GROUP + r + 1] * o_sel[rs]
                     + ga_ref[0, :, 2 * GROUP + r:2 * GROUP + r + 1] * o_win[rs])
    for pair in range(GROUP // 2):
        a, b = mixed[2 * pair], mixed[2 * pair + 1]
        ar, br = pltpu.roll(a, HEAD_DIM, axis=1), pltpu.roll(b, HEAD_DIM, axis=1)
        left = jnp.where(g == 0, a, ar)
        right = jnp.where(g == 0, br, b)
        o_ref[0, :, pair * LANE:(pair + 1) * LANE] = jnp.where(lane < HEAD_DIM, left, right).astype(bf16)


def _attention(q, kcmp, vcmp, ks, vs, kw, vw, ga, mt, n_cmp):
    B, S, _ = q.shape
    G, NC = kcmp.shape[1], kcmp.shape[2]
    n_sel = S // SEL_BLOCK
    rows = GROUP * TQ
    grid = (B, G, S // TQ)
    per_g = lambda width: pl.BlockSpec((1, S, width), lambda b, g, i: (b, 0, g))
    both = lambda width: pl.BlockSpec((1, S, width), lambda b, g, i: (b, 0, 0))
    return pl.pallas_call(
        functools.partial(_attn_kernel, n_cmp=n_cmp, n_sel=n_sel),
        grid=grid,
        in_specs=[
            pl.BlockSpec((1, TQ, GROUP * LANE), lambda b, g, i: (b, i, g)),
            pl.BlockSpec((1, 1, NC, LANE), lambda b, g, i: (b, g, 0, 0)),
            pl.BlockSpec((1, NC, LANE), lambda b, g, i: (b, 0, 0)),
            per_g(LANE), both(KV_WIDTH), per_g(LANE), both(KV_WIDTH),
            pl.BlockSpec((1, TQ, LANE), lambda b, g, i: (b, i, g)),
            pl.BlockSpec(mt.shape, lambda b, g, i: (0, 0)),
        ],
        out_specs=pl.BlockSpec((1, TQ, GROUP * HEAD_DIM), lambda b, g, i: (b, i, g)),
        out_shape=jax.ShapeDtypeStruct((B, S, ATTN_WIDTH), bf16),
        scratch_shapes=[pltpu.VMEM((rows, 1), f32), pltpu.VMEM((rows, 1), f32),
                        pltpu.VMEM((rows, LANE), f32)],
        compiler_params=pltpu.CompilerParams(
            dimension_semantics=("arbitrary", "arbitrary", "arbitrary"),
            vmem_limit_bytes=VMEM_LIMIT),
        name="attention",
    )(q, kcmp, vcmp, ks, vs, kw, vw, ga, mt)


def _out_mlp_kernel(x_ref, ao_ref, pm_ref, gm_ref, wba_ref, wbp_ref, wo_ref, gn_ref,
                    w1_ref, w2_ref, gf_ref, o_ref):
    a = _dot(ao_ref[...], wba_ref[...])
    b = _dot(pm_ref[...], wbp_ref[...])
    gm = gm_ref[...].astype(f32)
    merged = (gm[:, :D_MODEL] * a + gm[:, D_MODEL:] * b).astype(bf16)
    x1 = x_ref[...] + _dot(merged, wo_ref[...])
    h = _rms(x1, gn_ref[...]).astype(bf16)
    f = jnp.square(jnp.maximum(_dot(h, w1_ref[...]), 0.0)).astype(bf16)
    x2 = x1 + _dot(f, w2_ref[...])
    o_ref[...] = _rms(x2, gf_ref[...])


def _out_mlp(x2d, ao, pm, gm, wba, wbp, wo, gn, w1, w2, gf):
    T, D = x2d.shape
    tok = lambda width: pl.BlockSpec((TS_OUT, width), lambda t: (t, 0))
    const = lambda arr: pl.BlockSpec(arr.shape, lambda t: (0, 0), pipeline_mode=pl.Buffered(1))
    return pl.pallas_call(
        _out_mlp_kernel,
        grid=(T // TS_OUT,),
        in_specs=[tok(D), tok(ATTN_WIDTH), tok(POOL_WIDTH), tok(2 * D_MODEL),
                  const(wba), const(wbp), const(wo), const(gn), const(w1), const(w2), const(gf)],
        out_specs=tok(D),
        out_shape=jax.ShapeDtypeStruct((T, D), f32),
        compiler_params=pltpu.CompilerParams(
            dimension_semantics=("arbitrary",), vmem_limit_bytes=VMEM_LIMIT),
        name="out_mlp",
    )(x2d, ao, pm, gm, wba, wbp, wo, gn, w1, w2, gf)


def _rope_tables(pos):
    inv = ROPE_THETA ** (-jnp.arange(0, ROT_DIM, 2, dtype=f32) / ROT_DIM)
    ang = pos.astype(f32)[:, None] * inv
    cos, sin = jnp.cos(ang), jnp.sin(ang)
    P = pos.shape[0]
    ones = jnp.ones((P, LANE - ROT_DIM), f32)
    zeros = jnp.zeros((P, LANE - ROT_HALF), f32)
    c = jnp.concatenate([cos, cos, ones], axis=1)
    sa = jnp.concatenate([-sin, zeros], axis=1)
    sb = jnp.concatenate([jnp.zeros((P, ROT_HALF), f32), sin,
                          jnp.zeros((P, LANE - ROT_DIM), f32)], axis=1)
    return c, sa, sb


def _cmp_to_sel_t(S):
    n_cmp = (S - CMP_BLOCK) // CMP_STRIDE + 1
    n_sel = S // SEL_BLOCK
    cs = np.arange(n_cmp)[:, None] * CMP_STRIDE
    js = np.arange(n_sel)[None, :] * SEL_BLOCK
    ov = np.clip(np.minimum(cs + CMP_BLOCK, js + SEL_BLOCK) - np.maximum(cs, js), 0, None)
    m = np.zeros((S // CMP_STRIDE, n_sel), np.float32)
    m[:n_cmp] = ov / CMP_STRIDE
    return jnp.asarray(m.T, dtype=bf16), n_cmp


def _layout_w_in(w):
    D = w.shape[0]
    o = 0
    wq = w[:, o:o + ATTN_WIDTH]; o += ATTN_WIDTH
    kcw, vcw, ksw, vsw, kww, vww = [w[:, o + j * KV_WIDTH:o + (j + 1) * KV_WIDTH] for j in range(6)]
    o += 6 * KV_WIDTH
    gaw = w[:, o:o + N_BRANCH * N_HEADS]; o += N_BRANCH * N_HEADS
    uw = w[:, o:o + POOL_WIDTH]; o += POOL_WIDTH
    gmw = w[:, o:]

    def pad_heads(m, n):
        m = m.reshape(D, n, HEAD_DIM)
        return jnp.concatenate([m, jnp.zeros_like(m)], axis=2).reshape(D, n * LANE)

    gaw = gaw.reshape(D, N_BRANCH, N_KV, GROUP).transpose(0, 2, 1, 3).reshape(D, N_KV, N_BRANCH * GROUP)
    gaw = jnp.concatenate([gaw, jnp.zeros((D, N_KV, LANE - N_BRANCH * GROUP), w.dtype)], axis=2)
    cols = [pad_heads(wq, N_HEADS), pad_heads(ksw, N_KV), pad_heads(kww, N_KV), vsw, vww, kcw, vcw,
            uw, gaw.reshape(D, N_KV * LANE), gmw]
    out = jnp.concatenate(cols, axis=1)
    assert out.shape[1] == W_COLS
    return out.astype(bf16)


def kernel(x, norm_mix, w_in, cmp_pe_k, cmp_pe_v, cmp_k_w1, cmp_k_w2, cmp_v_w1, cmp_v_w2,
           w_branch_attn, pool_w, pool_scale, w_branch_pool, w_out, norm_mlp, w_ff1, w_ff2,
           norm_final):
    B, S, D = x.shape
    assert norm_mix.shape[0] == 1, "single-layer block: the final norm is fused into out_mlp"
    assert S % TS_IN == 0 and S % TQ == 0 and (B * S) % TS_OUT == 0 and TQ == TK and WINDOW % TK == 0
    n_chunk = S // CMP_STRIDE
    rc, rsa, rsb = _rope_tables(jnp.arange(S))
    crc, crsa, crsb = _rope_tables(jnp.arange(n_chunk) * CMP_STRIDE + CMP_BLOCK - 1)
    mt, n_cmp = _cmp_to_sel_t(S)
    half = CMP_STRIDE * HEAD_DIM

    (q, ks, kw, vs, vw, kc, vc, ga, pm, gm) = _in_proj(
        x, norm_mix[0][None, :], _layout_w_in(w_in[0]), rc, rsa, rsb,
        pool_w[0].astype(bf16), pool_scale[0][None, :])

    def chunked(a):
        return a.reshape(B, n_chunk, CMP_STRIDE, N_KV, HEAD_DIM).transpose(0, 3, 1, 2, 4).reshape(
            B, N_KV, n_chunk, half)

    zeros_w2 = jnp.zeros((CMP_HIDDEN, HEAD_DIM), f32)
    kw2 = jnp.concatenate([cmp_k_w2[0], zeros_w2], axis=1).astype(bf16)
    vw2 = jnp.stack([jnp.concatenate([cmp_v_w2[0], zeros_w2], axis=1),
                     jnp.concatenate([zeros_w2, cmp_v_w2[0]], axis=1)]).astype(bf16)
    kcmp, vcmp = _compress(
        chunked(kc), chunked(vc), cmp_pe_k[0].reshape(2, half), cmp_pe_v[0].reshape(2, half),
        cmp_k_w1[0].astype(bf16), kw2, cmp_v_w1[0].astype(bf16), vw2, crc, crsa, crsb)

    ao = _attention(q, kcmp, vcmp, ks, vs, kw, vw, ga, mt, n_cmp)

    y = _out_mlp(x.reshape(B * S, D), ao.reshape(B * S, ATTN_WIDTH), pm.reshape(B * S, POOL_WIDTH),
                 gm.reshape(B * S, 2 * D_MODEL), w_branch_attn[0].astype(bf16),
                 w_branch_pool[0].astype(bf16), w_out[0].astype(bf16), norm_mlp[0][None, :],
                 w_ff1[0].astype(bf16), w_ff2[0].astype(bf16), norm_final[None, :])
    return y.reshape(B, S, D)
```

```python
import functools

import jax
import jax.numpy as jnp
import numpy as np
from jax import lax
from jax.experimental import pallas as pl
from jax.experimental.pallas import tpu as pltpu

f32 = jnp.float32
bf16 = jnp.bfloat16

D_MODEL = 1024
N_HEADS = 8
HEAD_DIM = 64
N_KV = 2
GROUP = N_HEADS // N_KV
ROT_DIM = HEAD_DIM // 4
ROT_HALF = ROT_DIM // 2
ROPE_THETA = 500000.0
CMP_BLOCK = 32
CMP_STRIDE = 16
CMP_HIDDEN = 4 * HEAD_DIM
SEL_BLOCK = 64
SEL_SHIFT = 6
SEL_TOPK = 16
WINDOW = 512
N_BRANCH = 3
ATTN_WIDTH = N_HEADS * HEAD_DIM
KV_WIDTH = N_KV * HEAD_DIM
POOL_WIDTH = 512
POOL_WINDOWS = (2, 4, 8, 16)
POOL_GROUP = POOL_WIDTH // len(POOL_WINDOWS)
POOL_HALO = 16
D_FF = 4 * D_MODEL
EPS = 1e-6
NEG_INF = -1e30
FORCE_SCORE = 1e4
BIG = 2.0 ** 100
M_INIT = -3.0e38

LANE = 128
VMEM_LIMIT = 56 * 1024 * 1024

SEG_Q = (0, N_HEADS * LANE)
SEG_KS = (SEG_Q[1], SEG_Q[1] + N_KV * LANE)
SEG_KW = (SEG_KS[1], SEG_KS[1] + N_KV * LANE)
SEG_VS = (SEG_KW[1], SEG_KW[1] + KV_WIDTH)
SEG_VW = (SEG_VS[1], SEG_VS[1] + KV_WIDTH)
SEG_KC = (SEG_VW[1], SEG_VW[1] + KV_WIDTH)
SEG_VC = (SEG_KC[1], SEG_KC[1] + KV_WIDTH)
SEG_U = (SEG_VC[1], SEG_VC[1] + POOL_WIDTH)
SEG_GA = (SEG_U[1], SEG_U[1] + N_KV * LANE)
SEG_GM = (SEG_GA[1], SEG_GA[1] + 2 * D_MODEL)
W_COLS = SEG_GM[1]

TS_IN = 512
TQ = 256
TK = 512
TS_OUT = 256


def _dot(a, b):
    return jnp.dot(a, b, preferred_element_type=f32)


def _dot_nt(a, b):
    return lax.dot_general(a, b, (((1,), (1,)), ((), ())), preferred_element_type=f32)


def _rope_lane_tile(x, c, sa, sb):
    return (x * c + pltpu.roll(x, LANE - ROT_HALF, axis=1) * sa
            + pltpu.roll(x, ROT_HALF, axis=1) * sb)


def _rms(x, g):
    ms = jnp.mean(x * x, axis=-1, keepdims=True)
    return x * lax.rsqrt(ms + EPS) * g


def _in_proj_kernel(x_ref, g_ref, w_ref, rc_ref, rsa_ref, rsb_ref, pw_ref, ps_ref,
                    q_ref, ks_ref, kw_ref, vs_ref, vw_ref, kc_ref, vc_ref, ga_ref, pm_ref, gm_ref,
                    ubuf):
    si = pl.program_id(1)
    t0 = si * TS_IN
    h = _rms(x_ref[0], g_ref[...]).astype(bf16)

    def seg(s):
        return _dot(h, w_ref[:, s[0]:s[1]])

    rc, rsa, rsb = rc_ref[...], rsa_ref[...], rsb_ref[...]

    q = seg(SEG_Q)
    for hh in range(N_HEADS):
        sl = slice(hh * LANE, (hh + 1) * LANE)
        q_ref[0, :, sl] = (_rope_lane_tile(q[:, sl], rc, rsa, rsb) * (HEAD_DIM ** -0.5)).astype(bf16)

    row_t = t0 + lax.broadcasted_iota(jnp.int32, (TS_IN, LANE), 0)
    lane = lax.broadcasted_iota(jnp.int32, (TS_IN, LANE), 1)
    onehot = jnp.where(lane - HEAD_DIM == (row_t >> SEL_SHIFT), BIG, 0.0).astype(f32)
    ks = seg(SEG_KS)
    kw = seg(SEG_KW)
    for gg in range(N_KV):
        sl = slice(gg * LANE, (gg + 1) * LANE)
        ks_ref[0, :, sl] = (_rope_lane_tile(ks[:, sl], rc, rsa, rsb) + onehot).astype(bf16)
        kw_ref[0, :, sl] = _rope_lane_tile(kw[:, sl], rc, rsa, rsb).astype(bf16)

    vs_ref[0] = seg(SEG_VS).astype(bf16)
    vw_ref[0] = seg(SEG_VW).astype(bf16)
    kc_ref[0] = seg(SEG_KC)
    vc_ref[0] = seg(SEG_VC)
    ga_ref[0] = jax.nn.sigmoid(seg(SEG_GA))
    gm_ref[0] = jax.nn.sigmoid(seg(SEG_GM)).astype(bf16)

    u = seg(SEG_U)

    @pl.when(si == 0)
    def _():
        ubuf[0:POOL_HALO, :] = jnp.zeros((POOL_HALO, POOL_WIDTH), f32)

    ubuf[POOL_HALO:, :] = u
    tpos1 = (t0 + 1 + lax.broadcasted_iota(jnp.int32, (TS_IN, 1), 0)).astype(f32)
    for gi, w in enumerate(POOL_WINDOWS):
        sl = slice(gi * POOL_GROUP, (gi + 1) * POOL_GROUP)
        acc = u[:, sl]
        for k in range(1, w):
            acc = acc + ubuf[POOL_HALO - k:POOL_HALO - k + TS_IN, sl]
        cnt = jnp.minimum(tpos1, float(w))
        pooled = (acc / cnt - u[:, sl]).astype(bf16)
        mixed = _dot(pooled, pw_ref[gi]) * ps_ref[:, sl]
        pm_ref[0, :, sl] = mixed.astype(bf16)
    ubuf[0:POOL_HALO, :] = ubuf[TS_IN:TS_IN + POOL_HALO, :]


def _in_proj(x, g, w, rc, rsa, rsb, pw, ps):
    B, S, D = x.shape
    grid = (B, S // TS_IN)
    tok = lambda width: pl.BlockSpec((1, TS_IN, width), lambda b, s: (b, s, 0))
    const = lambda shape: pl.BlockSpec(shape, lambda b, s: (0,) * len(shape),
                                       pipeline_mode=pl.Buffered(1))
    tab = pl.BlockSpec((TS_IN, LANE), lambda b, s: (s, 0))
    out_shapes = [
        jax.ShapeDtypeStruct((B, S, N_HEADS * LANE), bf16),
        jax.ShapeDtypeStruct((B, S, N_KV * LANE), bf16),
        jax.ShapeDtypeStruct((B, S, N_KV * LANE), bf16),
        jax.ShapeDtypeStruct((B, S, KV_WIDTH), bf16),
        jax.ShapeDtypeStruct((B, S, KV_WIDTH), bf16),
        jax.ShapeDtypeStruct((B, S, KV_WIDTH), f32),
        jax.ShapeDtypeStruct((B, S, KV_WIDTH), f32),
        jax.ShapeDtypeStruct((B, S, N_KV * LANE), f32),
        jax.ShapeDtypeStruct((B, S, POOL_WIDTH), bf16),
        jax.ShapeDtypeStruct((B, S, 2 * D_MODEL), bf16),
    ]
    return pl.pallas_call(
        _in_proj_kernel,
        grid=grid,
        in_specs=[tok(D), const((1, D)), const((D, W_COLS)), tab, tab, tab,
                  const((len(POOL_WINDOWS), POOL_GROUP, POOL_GROUP)), const((1, POOL_WIDTH))],
        out_specs=[tok(s.shape[-1]) for s in out_shapes],
        out_shape=out_shapes,
        scratch_shapes=[pltpu.VMEM((POOL_HALO + TS_IN, POOL_WIDTH), f32)],
        compiler_params=pltpu.CompilerParams(
            dimension_semantics=("arbitrary", "arbitrary"), vmem_limit_bytes=VMEM_LIMIT),
        name="in_proj",
    )(x, g, w, rc, rsa, rsb, pw, ps)


def _compress_kernel(kc_ref, vc_ref, pek_ref, pev_ref, kw1_ref, kw2_ref, vw1_ref, vw2_ref,
                     rc_ref, rsa_ref, rsb_ref, kout_ref, vout_ref):
    half = CMP_STRIDE * HEAD_DIM
    n_chunk = kc_ref.shape[2]

    def hidden(src, pe_ref, w1_ref):
        lo = _dot((src + pe_ref[0:1, :]).astype(bf16), w1_ref[0:half, :])
        hi = _dot((src + pe_ref[1:2, :]).astype(bf16), w1_ref[half:, :])
        pre = lo + pltpu.roll(hi, n_chunk - 1, axis=0)
        return jax.nn.gelu(pre, approximate=True).astype(bf16)

    vacc = None
    for gg in range(N_KV):
        hk = hidden(kc_ref[0, gg], pek_ref, kw1_ref)
        kcmp = _dot(hk, kw2_ref[...])
        kout_ref[0, gg] = _rope_lane_tile(kcmp, rc_ref[...], rsa_ref[...], rsb_ref[...]).astype(bf16)
        hv = hidden(vc_ref[0, gg], pev_ref, vw1_ref)
        part = _dot(hv, vw2_ref[gg])
        vacc = part if vacc is None else vacc + part
    vout_ref[0] = vacc.astype(bf16)


def _compress(kcr, vcr, pek, pev, kw1, kw2, vw1, vw2, rc, rsa, rsb):
    B, G, NC, W = kcr.shape
    const = lambda shape: pl.BlockSpec(shape, lambda b: (0,) * len(shape))
    src = pl.BlockSpec((1, G, NC, W), lambda b: (b, 0, 0, 0))
    return pl.pallas_call(
        _compress_kernel,
        grid=(B,),
        in_specs=[src, src, const(pek.shape), const(pev.shape), const(kw1.shape), const(kw2.shape),
                  const(vw1.shape), const(vw2.shape), const(rc.shape), const(rsa.shape), const(rsb.shape)],
        out_specs=[pl.BlockSpec((1, G, NC, LANE), lambda b: (b, 0, 0, 0)),
                   pl.BlockSpec((1, NC, LANE), lambda b: (b, 0, 0))],
        out_shape=[jax.ShapeDtypeStruct((B, G, NC, LANE), bf16),
                   jax.ShapeDtypeStruct((B, NC, LANE), bf16)],
        compiler_params=pltpu.CompilerParams(
            dimension_semantics=("arbitrary",), vmem_limit_bytes=VMEM_LIMIT),
        name="compress",
    )(kcr, vcr, pek, pev, kw1, kw2, vw1, vw2, rc, rsa, rsb)


def _attn_kernel(q_ref, kc_ref, vc_ref, ks_ref, vs_ref, kw_ref, vw_ref, ga_ref, mt_ref, o_ref,
                 qa_sc, s_sc, mx_sc, l_sc, acc_sc, *, n_cmp, n_sel):
    g = pl.program_id(1)
    i = pl.program_id(2)
    t0 = i * TQ
    rows = GROUP * TQ
    ncp = kc_ref.shape[2]

    row_t = t0 + lax.broadcasted_iota(jnp.int32, (TQ, 1), 0)
    lane_t = t0 + lax.broadcasted_iota(jnp.int32, (1, TQ), 1)
    row4_t = t0 + (lax.broadcasted_iota(jnp.int32, (rows, 1), 0) & (TQ - 1))

    kc = kc_ref[0, 0]
    vc = vc_ref[0]
    cidx = lax.broadcasted_iota(jnp.int32, (1, ncp), 1)
    cvalid = ((cidx * CMP_STRIDE + (CMP_BLOCK - 1)) <= row_t) & (cidx < n_cmp)
    psum = jnp.zeros((TQ, ncp), f32)
    q_heads = []
    o_cmp = []
    for r in range(GROUP):
        qr = q_ref[0, :, r * LANE:(r + 1) * LANE]
        q_heads.append(qr)
        s = jnp.where(cvalid, _dot_nt(qr, kc), NEG_INF)
        mx = jnp.max(s, axis=-1, keepdims=True)
        e = jnp.where(cvalid, jnp.exp(s - mx), 0.0)
        den = jnp.maximum(jnp.sum(e, axis=-1, keepdims=True), 1e-30)
        p = e * (1.0 / den)
        psum = psum + p
        o_cmp.append(_dot(p.astype(bf16), vc))

    mt = mt_ref[...]
    p_hi = psum.astype(bf16)
    rem = psum - p_hi.astype(f32)
    p_mid = rem.astype(bf16)
    p_lo = (rem - p_mid.astype(f32)).astype(bf16)
    imp = _dot_nt(mt, p_hi) + _dot_nt(mt, p_mid) + _dot_nt(mt, p_lo)
    blk = lax.broadcasted_iota(jnp.int32, (n_sel, TQ), 0)
    cur = lane_t >> SEL_SHIFT
    imp = jnp.where(blk > cur, -FORCE_SCORE, imp)
    imp = jnp.where((blk == 0) | (blk == cur) | (blk == cur - 1), FORCE_SCORE, imp)
    ahead = jnp.zeros((n_sel, TQ), jnp.int32)
    for j in range(n_sel):
        row = imp[j:j + 1, :]
        wins = (row > imp) | ((row == imp) & (blk > j))
        ahead = ahead + wins.astype(jnp.int32)
    sel_m1 = jnp.where(ahead < min(SEL_TOPK, n_sel), 0.0, -1.0).astype(f32)
    sel_pad = jnp.concatenate([jnp.zeros((LANE - n_sel, TQ), f32), sel_m1], axis=0).T
    sel_pad = sel_pad.astype(bf16)
    q_aug = jnp.concatenate([qr + sel_pad for qr in q_heads], axis=0)

    qa_sc[...] = q_aug

    def lane_chunks(s):
        return [s[:, c * LANE:(c + 1) * LANE] for c in range(s.shape[1] // LANE)]

    def chunk_max(s):
        return functools.reduce(jnp.maximum, lane_chunks(s))

    n_tiles = (t0 + TQ + TK - 1) // TK
    mx_sc[...] = jnp.full((rows, LANE), M_INIT, f32)

    def score_tile(kt, causal):
        k0 = pl.multiple_of(kt * TK, TK)
        s = _dot_nt(qa_sc[...], ks_ref[0, pl.ds(k0, TK), :])
        if causal:
            kpos = k0 + lax.broadcasted_iota(jnp.int32, (1, TK), 1)
            s = jnp.where(kpos <= row4_t, s, -BIG)
        s_sc[kt] = s
        mx_sc[...] = jnp.maximum(mx_sc[...], chunk_max(s))

    def score_body(kt, carry):
        score_tile(kt, False)
        return carry

    lax.fori_loop(0, n_tiles - 1, score_body, 0)
    score_tile(n_tiles - 1, True)

    mx_sc[...] = jnp.broadcast_to(jnp.max(mx_sc[...], axis=-1, keepdims=True), (rows, LANE))
    l_sc[...] = jnp.zeros((rows, LANE), f32)
    acc_sc[...] = jnp.zeros((rows, LANE), f32)

    def weight_body(kt, carry):
        k0 = pl.multiple_of(kt * TK, TK)
        mb = mx_sc[...]
        ps = [jnp.exp(c - mb) for c in lane_chunks(s_sc[kt])]
        l_sc[...] = l_sc[...] + functools.reduce(jnp.add, ps)
        p = jnp.concatenate(ps, axis=1).astype(bf16)
        acc_sc[...] = acc_sc[...] + _dot(p, vs_ref[0, pl.ds(k0, TK), :])
        return carry

    lax.fori_loop(0, n_tiles, weight_body, 0)
    o_sel = acc_sc[...] * (1.0 / jnp.sum(l_sc[...], axis=-1, keepdims=True))

    band = WINDOW + TQ
    w0 = pl.multiple_of(jnp.maximum(t0 - WINDOW, 0), TQ)
    wpos = w0 + lax.broadcasted_iota(jnp.int32, (1, band), 1)
    sw = _dot_nt(qa_sc[...], kw_ref[0, pl.ds(w0, band), :])
    sw = jnp.where((wpos <= row4_t) & (row4_t - wpos < WINDOW), sw, -BIG)
    pw = jnp.exp(sw - jnp.max(sw, axis=-1, keepdims=True))
    o_win = (_dot(pw.astype(bf16), vw_ref[0, pl.ds(w0, band), :])
             * (1.0 / jnp.sum(pw, axis=-1, keepdims=True)))

    lane = lax.broadcasted_iota(jnp.int32, (TQ, LANE), 1)
    mixed = []
    for r in range(GROUP):
        rs = slice(r * TQ, (r + 1) * TQ)
        mixed.append(ga_ref[0, :, r:r + 1] * o_cmp[r]
                     + ga_ref[0, :, GROUP + r:GROUP + r + 1] * o_sel[rs]
                     + ga_ref[0, :, 2 * GROUP + r:2 * GROUP + r + 1] * o_win[rs])
    for pair in range(GROUP // 2):
        a, b = mixed[2 * pair], mixed[2 * pair + 1]
        ar, br = pltpu.roll(a, HEAD_DIM, axis=1), pltpu.roll(b, HEAD_DIM, axis=1)
        left = jnp.where(g == 0, a, ar)
        right = jnp.where(g == 0, br, b)
        o_ref[0, :, pair * LANE:(pair + 1) * LANE] = jnp.where(lane < HEAD_DIM, left, right).astype(bf16)


def _attention(q, kcmp, vcmp, ks, vs, kw, vw, ga, mt, n_cmp):
    B, S, _ = q.shape
    G, NC = kcmp.shape[1], kcmp.shape[2]
    n_sel = S // SEL_BLOCK
    rows = GROUP * TQ
    grid = (B, G, S // TQ)
    per_g = lambda width: pl.BlockSpec((1, S, width), lambda b, g, i: (b, 0, g))
    both = lambda width: pl.BlockSpec((1, S, width), lambda b, g, i: (b, 0, 0))
    return pl.pallas_call(
        functools.partial(_attn_kernel, n_cmp=n_cmp, n_sel=n_sel),
        grid=grid,
        in_specs=[
            pl.BlockSpec((1, TQ, GROUP * LANE), lambda b, g, i: (b, i, g)),
            pl.BlockSpec((1, 1, NC, LANE), lambda b, g, i: (b, g, 0, 0)),
            pl.BlockSpec((1, NC, LANE), lambda b, g, i: (b, 0, 0)),
            per_g(LANE), both(KV_WIDTH), per_g(LANE), both(KV_WIDTH),
            pl.BlockSpec((1, TQ, LANE), lambda b, g, i: (b, i, g)),
            pl.BlockSpec(mt.shape, lambda b, g, i: (0, 0)),
        ],
        out_specs=pl.BlockSpec((1, TQ, GROUP * HEAD_DIM), lambda b, g, i: (b, i, g)),
        out_shape=jax.ShapeDtypeStruct((B, S, ATTN_WIDTH), bf16),
        scratch_shapes=[pltpu.VMEM((rows, LANE), bf16),
                        pltpu.VMEM((S // TK, rows, TK), f32),
                        pltpu.VMEM((rows, LANE), f32),
                        pltpu.VMEM((rows, LANE), f32),
                        pltpu.VMEM((rows, LANE), f32)],
        compiler_params=pltpu.CompilerParams(
            dimension_semantics=("arbitrary", "arbitrary", "arbitrary"),
            vmem_limit_bytes=VMEM_LIMIT),
        name="attention",
    )(q, kcmp, vcmp, ks, vs, kw, vw, ga, mt)


def _out_mlp_kernel(x_ref, ao_ref, pm_ref, gm_ref, wba_ref, wbp_ref, wo_ref, gn_ref,
                    w1_ref, w2_ref, gf_ref, o_ref):
    a = _dot(ao_ref[...], wba_ref[...])
    b = _dot(pm_ref[...], wbp_ref[...])
    gm = gm_ref[...].astype(f32)
    merged = (gm[:, :D_MODEL] * a + gm[:, D_MODEL:] * b).astype(bf16)
    x1 = x_ref[...] + _dot(merged, wo_ref[...])
    h = _rms(x1, gn_ref[...]).astype(bf16)
    f = jnp.square(jnp.maximum(_dot(h, w1_ref[...]), 0.0)).astype(bf16)
    x2 = x1 + _dot(f, w2_ref[...])
    o_ref[...] = _rms(x2, gf_ref[...])


def _out_mlp(x2d, ao, pm, gm, wba, wbp, wo, gn, w1, w2, gf):
    T, D = x2d.shape
    tok = lambda width: pl.BlockSpec((TS_OUT, width), lambda t: (t, 0))
    const = lambda arr: pl.BlockSpec(arr.shape, lambda t: (0, 0), pipeline_mode=pl.Buffered(1))
    return pl.pallas_call(
        _out_mlp_kernel,
        grid=(T // TS_OUT,),
        in_specs=[tok(D), tok(ATTN_WIDTH), tok(POOL_WIDTH), tok(2 * D_MODEL),
                  const(wba), const(wbp), const(wo), const(gn), const(w1), const(w2), const(gf)],
        out_specs=tok(D),
        out_shape=jax.ShapeDtypeStruct((T, D), f32),
        compiler_params=pltpu.CompilerParams(
            dimension_semantics=("arbitrary",), vmem_limit_bytes=VMEM_LIMIT),
        name="out_mlp",
    )(x2d, ao, pm, gm, wba, wbp, wo, gn, w1, w2, gf)


def _rope_tables(pos):
    inv = ROPE_THETA ** (-jnp.arange(0, ROT_DIM, 2, dtype=f32) / ROT_DIM)
    ang = pos.astype(f32)[:, None] * inv
    cos, sin = jnp.cos(ang), jnp.sin(ang)
    P = pos.shape[0]
    ones = jnp.ones((P, LANE - ROT_DIM), f32)
    zeros = jnp.zeros((P, LANE - ROT_HALF), f32)
    c = jnp.concatenate([cos, cos, ones], axis=1)
    sa = jnp.concatenate([-sin, zeros], axis=1)
    sb = jnp.concatenate([jnp.zeros((P, ROT_HALF), f32), sin,
                          jnp.zeros((P, LANE - ROT_DIM), f32)], axis=1)
    return c, sa, sb


def _cmp_to_sel_t(S):
    n_cmp = (S - CMP_BLOCK) // CMP_STRIDE + 1
    n_sel = S // SEL_BLOCK
    cs = np.arange(n_cmp)[:, None] * CMP_STRIDE
    js = np.arange(n_sel)[None, :] * SEL_BLOCK
    ov = np.clip(np.minimum(cs + CMP_BLOCK, js + SEL_BLOCK) - np.maximum(cs, js), 0, None)
    m = np.zeros((S // CMP_STRIDE, n_sel), np.float32)
    m[:n_cmp] = ov / CMP_STRIDE
    return jnp.asarray(m.T, dtype=bf16), n_cmp


def _layout_w_in(w):
    D = w.shape[0]
    o = 0
    wq = w[:, o:o + ATTN_WIDTH]; o += ATTN_WIDTH
    kcw, vcw, ksw, vsw, kww, vww = [w[:, o + j * KV_WIDTH:o + (j + 1) * KV_WIDTH] for j in range(6)]
    o += 6 * KV_WIDTH
    gaw = w[:, o:o + N_BRANCH * N_HEADS]; o += N_BRANCH * N_HEADS
    uw = w[:, o:o + POOL_WIDTH]; o += POOL_WIDTH
    gmw = w[:, o:]

    def pad_heads(m, n):
        m = m.reshape(D, n, HEAD_DIM)
        return jnp.concatenate([m, jnp.zeros_like(m)], axis=2).reshape(D, n * LANE)

    gaw = gaw.reshape(D, N_BRANCH, N_KV, GROUP).transpose(0, 2, 1, 3).reshape(D, N_KV, N_BRANCH * GROUP)
    gaw = jnp.concatenate([gaw, jnp.zeros((D, N_KV, LANE - N_BRANCH * GROUP), w.dtype)], axis=2)
    cols = [pad_heads(wq, N_HEADS), pad_heads(ksw, N_KV), pad_heads(kww, N_KV), vsw, vww, kcw, vcw,
            uw, gaw.reshape(D, N_KV * LANE), gmw]
    out = jnp.concatenate(cols, axis=1)
    assert out.shape[1] == W_COLS
    return out.astype(bf16)


def kernel(x, norm_mix, w_in, cmp_pe_k, cmp_pe_v, cmp_k_w1, cmp_k_w2, cmp_v_w1, cmp_v_w2,
           w_branch_attn, pool_w, pool_scale, w_branch_pool, w_out, norm_mlp, w_ff1, w_ff2,
           norm_final):
    B, S, D = x.shape
    assert norm_mix.shape[0] == 1, "single-layer block: the final norm is fused into out_mlp"
    assert S % TS_IN == 0 and S % TQ == 0 and (B * S) % TS_OUT == 0 and TK % TQ == 0 and S % TK == 0 and WINDOW % TQ == 0
    n_chunk = S // CMP_STRIDE
    rc, rsa, rsb = _rope_tables(jnp.arange(S))
    crc, crsa, crsb = _rope_tables(jnp.arange(n_chunk) * CMP_STRIDE + CMP_BLOCK - 1)
    mt, n_cmp = _cmp_to_sel_t(S)
    half = CMP_STRIDE * HEAD_DIM

    (q, ks, kw, vs, vw, kc, vc, ga, pm, gm) = _in_proj(
        x, norm_mix[0][None, :], _layout_w_in(w_in[0]), rc, rsa, rsb,
        pool_w[0].astype(bf16), pool_scale[0][None, :])

    def chunked(a):
        return a.reshape(B, n_chunk, CMP_STRIDE, N_KV, HEAD_DIM).transpose(0, 3, 1, 2, 4).reshape(
            B, N_KV, n_chunk, half)

    zeros_w2 = jnp.zeros((CMP_HIDDEN, HEAD_DIM), f32)
    kw2 = jnp.concatenate([cmp_k_w2[0], zeros_w2], axis=1).astype(bf16)
    vw2 = jnp.stack([jnp.concatenate([cmp_v_w2[0], zeros_w2], axis=1),
                     jnp.concatenate([zeros_w2, cmp_v_w2[0]], axis=1)]).astype(bf16)
    kcmp, vcmp = _compress(
        chunked(kc), chunked(vc), cmp_pe_k[0].reshape(2, half), cmp_pe_v[0].reshape(2, half),
        cmp_k_w1[0].astype(bf16), kw2, cmp_v_w1[0].astype(bf16), vw2, crc, crsa, crsb)

    ao = _attention(q, kcmp, vcmp, ks, vs, kw, vw, ga, mt, n_cmp)

    y = _out_mlp(x.reshape(B * S, D), ao.reshape(B * S, ATTN_WIDTH), pm.reshape(B * S, POOL_WIDTH),
                 gm.reshape(B * S, 2 * D_MODEL), w_branch_attn[0].astype(bf16),
                 w_branch_pool[0].astype(bf16), w_out[0].astype(bf16), norm_mlp[0][None, :],
                 w_ff1[0].astype(bf16), w_ff2[0].astype(bf16), norm_final[None, :])
    return y.reshape(B, S, D)
```

```python
import functools
import math

import jax
import jax.numpy as jnp
import numpy as np
from jax import lax
from jax.experimental import pallas as pl
from jax.experimental.pallas import tpu as pltpu

f32 = jnp.float32
bf16 = jnp.bfloat16

D_MODEL = 1024
N_HEADS = 8
HEAD_DIM = 64
N_KV = 2
GROUP = N_HEADS // N_KV
ROT_DIM = HEAD_DIM // 4
ROT_HALF = ROT_DIM // 2
ROPE_THETA = 500000.0
CMP_BLOCK = 32
CMP_STRIDE = 16
CMP_RATIO = CMP_BLOCK // CMP_STRIDE
CMP_HIDDEN = 4 * HEAD_DIM
SEL_BLOCK = 64
SEL_SHIFT = 6
SEL_TOPK = 16
WINDOW = 512
N_BRANCH = 3
ATTN_WIDTH = N_HEADS * HEAD_DIM
KV_WIDTH = N_KV * HEAD_DIM
POOL_WIDTH = 512
POOL_WINDOWS = (2, 4, 8, 16)
POOL_GROUP = POOL_WIDTH // len(POOL_WINDOWS)
POOL_HALO = 16
D_FF = 4 * D_MODEL
EPS = 1e-6
NEG_INF = -1e30
FORCE_SCORE = 1e4
BIG = 2.0 ** 100
M_INIT = -3.0e38
Q_SCALE = HEAD_DIM ** -0.5 * math.log2(math.e)

LANE = 128
SUBLANE = 8
VMEM_LIMIT = 56 * 1024 * 1024

SEG_Q = (0, ATTN_WIDTH)
SEG_KS = (SEG_Q[1], SEG_Q[1] + KV_WIDTH)
SEG_KW = (SEG_KS[1], SEG_KS[1] + KV_WIDTH)
SEG_VS = (SEG_KW[1], SEG_KW[1] + KV_WIDTH)
SEG_VW = (SEG_VS[1], SEG_VS[1] + KV_WIDTH)
SEG_KC = (SEG_VW[1], SEG_VW[1] + KV_WIDTH)
SEG_VC = (SEG_KC[1], SEG_KC[1] + KV_WIDTH)
SEG_U = (SEG_VC[1], SEG_VC[1] + POOL_WIDTH)
SEG_GA = (SEG_U[1], SEG_U[1] + LANE)
SEG_GM = (SEG_GA[1], SEG_GA[1] + 2 * D_MODEL)
W_COLS = SEG_GM[1]

TS_IN = 512
TQ = 256
TK = 512
TS_OUT = 256


def _dot(a, b):
    return jnp.dot(a, b, preferred_element_type=f32)


def _dot_nt(a, b):
    return lax.dot_general(a, b, (((1,), (1,)), ((), ())), preferred_element_type=f32)


def _swap_halves(x):
    return pltpu.roll(x, HEAD_DIM, axis=1)


def _rope_lane_tile(x, c, sa, sb):
    return (x * c + pltpu.roll(x, LANE - ROT_HALF, axis=1) * sa
            + pltpu.roll(x, ROT_HALF, axis=1) * sb)


def _rms(x, g):
    ms = jnp.mean(x * x, axis=-1, keepdims=True)
    return x * lax.rsqrt(ms + EPS) * g


def _lane_chunks(s):
    return [s[:, c * LANE:(c + 1) * LANE] for c in range(s.shape[1] // LANE)]


def _normalize(pv, lo):
    inv = 1.0 / jnp.where(lo, 1.0, pv)
    return pv * _swap_halves(inv)


def _in_proj_kernel(x_ref, g_ref, w_ref, rc_ref, rsa_ref, rsb_ref, pw_ref, ps_ref,
                    q_ref, ks_ref, kw_ref, vs_ref, vw_ref, kc_ref, vc_ref, ga_ref, pm_ref, gm_ref,
                    ubuf):
    si = pl.program_id(1)
    t0 = si * TS_IN
    h = _rms(x_ref[0], g_ref[...]).astype(bf16)

    def seg(s):
        return _dot(h, w_ref[:, s[0]:s[1]])

    rc, rsa, rsb = rc_ref[...], rsa_ref[...], rsb_ref[...]
    row_t = t0 + lax.broadcasted_iota(jnp.int32, (TS_IN, LANE), 0)
    lane = lax.broadcasted_iota(jnp.int32, (TS_IN, LANE), 1)
    lo = lane < HEAD_DIM

    def head_tiles(x):
        return (x, _swap_halves(x))

    q = seg(SEG_Q)
    for m, chunk in enumerate(_lane_chunks(q)):
        for half, xh in enumerate(head_tiles(chunk)):
            hh = 2 * m + half
            rot = _rope_lane_tile(xh, rc, rsa, rsb) * Q_SCALE
            q_ref[0, :, hh * LANE:(hh + 1) * LANE] = jnp.where(lo, rot, 0.0).astype(bf16)

    onehot = jnp.where(lane - HEAD_DIM == (row_t >> SEL_SHIFT), BIG, 0.0).astype(f32)
    for gg, (ksh, kwh, vsh, vwh) in enumerate(zip(head_tiles(seg(SEG_KS)), head_tiles(seg(SEG_KW)),
                                                   head_tiles(seg(SEG_VS)), head_tiles(seg(SEG_VW)))):
        sl = slice(gg * LANE, (gg + 1) * LANE)
        ks_ref[0, :, sl] = jnp.where(lo, _rope_lane_tile(ksh, rc, rsa, rsb), onehot).astype(bf16)
        kw_ref[0, :, sl] = jnp.where(lo, _rope_lane_tile(kwh, rc, rsa, rsb), 0.0).astype(bf16)
        vs_ref[0, :, sl] = jnp.where(lo, vsh, 1.0).astype(bf16)
        vw_ref[0, :, sl] = jnp.where(lo, vwh, 1.0).astype(bf16)

    kc_ref[0] = seg(SEG_KC)
    vc_ref[0] = seg(SEG_VC)
    ga_ref[0] = jax.nn.sigmoid(seg(SEG_GA))
    gm_ref[0] = jax.nn.sigmoid(seg(SEG_GM)).astype(bf16)

    u = seg(SEG_U)

    @pl.when(si == 0)
    def _():
        ubuf[0:POOL_HALO, :] = jnp.zeros((POOL_HALO, POOL_WIDTH), f32)

    ubuf[POOL_HALO:, :] = u
    tpos1 = (t0 + 1 + lax.broadcasted_iota(jnp.int32, (TS_IN, 1), 0)).astype(f32)
    for gi, w in enumerate(POOL_WINDOWS):
        sl = slice(gi * POOL_GROUP, (gi + 1) * POOL_GROUP)
        acc = u[:, sl]
        for k in range(1, w):
            acc = acc + ubuf[POOL_HALO - k:POOL_HALO - k + TS_IN, sl]
        cnt = jnp.minimum(tpos1, float(w))
        pooled = (acc / cnt - u[:, sl]).astype(bf16)
        mixed = _dot(pooled, pw_ref[gi]) * ps_ref[:, sl]
        pm_ref[0, :, sl] = mixed.astype(bf16)
    ubuf[0:POOL_HALO, :] = ubuf[TS_IN:TS_IN + POOL_HALO, :]


def _in_proj(x, g, w, rc, rsa, rsb, pw, ps):
    B, S, D = x.shape
    grid = (B, S // TS_IN)
    tok = lambda width: pl.BlockSpec((1, TS_IN, width), lambda b, s: (b, s, 0))
    const = lambda shape: pl.BlockSpec(shape, lambda b, s: (0,) * len(shape),
                                       pipeline_mode=pl.Buffered(1))
    tab = pl.BlockSpec((TS_IN, LANE), lambda b, s: (s, 0))
    out_shapes = [
        jax.ShapeDtypeStruct((B, S, N_HEADS * LANE), bf16),
        jax.ShapeDtypeStruct((B, S, N_KV * LANE), bf16),
        jax.ShapeDtypeStruct((B, S, N_KV * LANE), bf16),
        jax.ShapeDtypeStruct((B, S, N_KV * LANE), bf16),
        jax.ShapeDtypeStruct((B, S, N_KV * LANE), bf16),
        jax.ShapeDtypeStruct((B, S, KV_WIDTH), f32),
        jax.ShapeDtypeStruct((B, S, KV_WIDTH), f32),
        jax.ShapeDtypeStruct((B, S, LANE), f32),
        jax.ShapeDtypeStruct((B, S, POOL_WIDTH), bf16),
        jax.ShapeDtypeStruct((B, S, 2 * D_MODEL), bf16),
    ]
    return pl.pallas_call(
        _in_proj_kernel,
        grid=grid,
        in_specs=[tok(D), const((1, D)), const((D, W_COLS)), tab, tab, tab,
                  const((len(POOL_WINDOWS), POOL_GROUP, POOL_GROUP)), const((1, POOL_WIDTH))],
        out_specs=[tok(s.shape[-1]) for s in out_shapes],
        out_shape=out_shapes,
        scratch_shapes=[pltpu.VMEM((POOL_HALO + TS_IN, POOL_WIDTH), f32)],
        compiler_params=pltpu.CompilerParams(
            dimension_semantics=("arbitrary", "arbitrary"), vmem_limit_bytes=VMEM_LIMIT),
        name="in_proj",
    )(x, g, w, rc, rsa, rsb, pw, ps)


def _compress_kernel(kc_ref, vc_ref, pek_ref, pev_ref, kw1_ref, kw2_ref, vw1_ref, vw2_ref,
                     rc_ref, rsa_ref, rsb_ref, kout_ref, vout_ref):
    n_chunk = kc_ref.shape[1] // CMP_STRIDE

    def hidden(src_ref, pe_ref, w1_ref):
        parts = []
        for part in range(CMP_RATIO):
            acc = None
            for tok in range(CMP_STRIDE):
                t = part * CMP_STRIDE + tok
                a = src_ref[0, pl.ds(tok, n_chunk, stride=CMP_STRIDE), :] + pe_ref[t:t + 1, :]
                d = _dot(a.astype(bf16), w1_ref[t])
                acc = d if acc is None else acc + d
            parts.append(acc)
        pre = parts[0] + pltpu.roll(parts[1], n_chunk - 1, axis=0)
        return jax.nn.gelu(pre, approximate=True).astype(bf16)

    lane = lax.broadcasted_iota(jnp.int32, (n_chunk, LANE), 1)
    lo = lane < HEAD_DIM
    kcmp = _dot(hidden(kc_ref, pek_ref, kw1_ref), kw2_ref[...])
    vcmp = _dot(hidden(vc_ref, pev_ref, vw1_ref), vw2_ref[...])
    for gg in range(N_KV):
        sl = slice(gg * LANE, (gg + 1) * LANE)
        kout_ref[0, gg] = _rope_lane_tile(kcmp[:, sl], rc_ref[...], rsa_ref[...], rsb_ref[...]).astype(bf16)
        vout_ref[0, gg] = jnp.where(lo, vcmp[:, sl], 1.0).astype(bf16)


def _compress(kc, vc, pek, pev, kw1, kw2, vw1, vw2, rc, rsa, rsb):
    B, S, W = kc.shape
    NC = S // CMP_STRIDE
    const = lambda shape: pl.BlockSpec(shape, lambda b: (0,) * len(shape))
    src = pl.BlockSpec((1, S, W), lambda b: (b, 0, 0))
    out = pl.BlockSpec((1, N_KV, NC, LANE), lambda b: (b, 0, 0, 0))
    return pl.pallas_call(
        _compress_kernel,
        grid=(B,),
        in_specs=[src, src, const(pek.shape), const(pev.shape), const(kw1.shape), const(kw2.shape),
                  const(vw1.shape), const(vw2.shape), const(rc.shape), const(rsa.shape), const(rsb.shape)],
        out_specs=[out, out],
        out_shape=[jax.ShapeDtypeStruct((B, N_KV, NC, LANE), bf16),
                   jax.ShapeDtypeStruct((B, N_KV, NC, LANE), bf16)],
        compiler_params=pltpu.CompilerParams(
            dimension_semantics=("arbitrary",), vmem_limit_bytes=VMEM_LIMIT),
        name="compress",
    )(kc, vc, pek, pev, kw1, kw2, vw1, vw2, rc, rsa, rsb)


def _attn_kernel(q_ref, kc_ref, vc_ref, ks_ref, vs_ref, kw_ref, vw_ref, ga_ref, mt_ref, wb_ref, o_ref,
                 qa_sc, s_sc, mx_sc, acc_sc, imp_sc, ah_sc, *, n_cmp, n_sel):
    g = pl.program_id(1)
    i = pl.program_id(2)
    t0 = i * TQ
    rows = GROUP * TQ
    ncp = kc_ref.shape[2]

    row_t = t0 + lax.broadcasted_iota(jnp.int32, (TQ, 1), 0)
    lane_t = t0 + lax.broadcasted_iota(jnp.int32, (1, TQ), 1)
    row4_t = t0 + (lax.broadcasted_iota(jnp.int32, (rows, 1), 0) & (TQ - 1))
    lo = lax.broadcasted_iota(jnp.int32, (TQ, LANE), 1) < HEAD_DIM
    lo4 = lax.broadcasted_iota(jnp.int32, (rows, LANE), 1) < HEAD_DIM

    kc = kc_ref[0, 0]
    vc = vc_ref[0, 0]
    cidx = lax.broadcasted_iota(jnp.int32, (1, ncp), 1)
    cvalid = ((cidx * CMP_STRIDE + (CMP_BLOCK - 1)) <= row_t) & (cidx < n_cmp)
    cbias = jnp.where(cvalid, 0.0, NEG_INF).astype(f32)
    any_valid = (row_t >= CMP_BLOCK - 1).astype(f32)
    psum = jnp.zeros((TQ, ncp), f32)
    q_heads = []
    o_cmp = []
    for r in range(GROUP):
        qr = q_ref[0, :, r * LANE:(r + 1) * LANE]
        q_heads.append(qr)
        s = _dot_nt(qr, kc) + cbias
        e = jnp.exp2(s - jnp.max(s, axis=-1, keepdims=True))
        inv = 1.0 / jnp.sum(e, axis=-1, keepdims=True)
        psum = psum + e * inv
        o_cmp.append(_dot(e.astype(bf16), vc) * (inv * any_valid))

    mt = mt_ref[...]
    p_hi = psum.astype(bf16)
    rem = psum - p_hi.astype(f32)
    p_mid = rem.astype(bf16)
    p_lo = (rem - p_mid.astype(f32)).astype(bf16)
    imp = _dot_nt(mt, p_hi) + _dot_nt(mt, p_mid) + _dot_nt(mt, p_lo)
    blk = lax.broadcasted_iota(jnp.int32, (n_sel, TQ), 0)
    cur = lane_t >> SEL_SHIFT
    imp = jnp.where(blk > cur, -FORCE_SCORE, imp)
    imp = jnp.where((blk == 0) | (blk == cur) | (blk == cur - 1), FORCE_SCORE, imp)
    imp_sc[...] = imp
    ah_sc[...] = jnp.zeros((n_sel, TQ), f32)
    sub = lax.broadcasted_iota(jnp.int32, (SUBLANE, TQ), 0)
    n_vrow = n_sel // SUBLANE
    for vj in range(n_vrow):
        @pl.when(vj * SUBLANE * SEL_BLOCK < t0 + TQ)
        def _(vj=vj):
            mine = imp_sc[vj * SUBLANE:(vj + 1) * SUBLANE, :]
            for v in range(n_vrow):
                other = imp_sc[v * SUBLANE:(v + 1) * SUBLANE, :]
                cnt = jnp.zeros((SUBLANE, TQ), f32)
                for sj in range(SUBLANE):
                    rival = jnp.broadcast_to(mine[sj:sj + 1, :], (SUBLANE, TQ))
                    if v > vj:
                        cnt = cnt + jnp.where(rival >= other, 1.0, 0.0)
                    elif v < vj:
                        cnt = cnt + jnp.where(rival > other, 1.0, 0.0)
                    else:
                        tie = jnp.where(rival == other, (sub > sj).astype(f32), 0.0)
                        cnt = cnt + jnp.where(rival > other, 1.0, tie)
                ah_sc[v * SUBLANE:(v + 1) * SUBLANE, :] += cnt
    sel_m1 = jnp.where(ah_sc[...] < float(min(SEL_TOPK, n_sel)), 0.0, -1.0).astype(f32)
    sel_pad = jnp.concatenate([jnp.zeros((LANE - n_sel, TQ), f32), sel_m1], axis=0).T
    sel_pad = sel_pad.astype(bf16)
    qa_sc[...] = jnp.concatenate([qr + sel_pad for qr in q_heads], axis=0)

    n_tiles = (t0 + TQ + TK - 1) // TK
    mx_sc[...] = jnp.full((rows, LANE), M_INIT, f32)

    def score_tiles(kts, causal):
        chunks = []
        for kt in kts:
            k0 = pl.multiple_of(kt * TK, TK)
            s = _dot_nt(qa_sc[...], ks_ref[0, pl.ds(k0, TK), :])
            if causal:
                kpos = k0 + lax.broadcasted_iota(jnp.int32, (1, TK), 1)
                s = jnp.where(kpos <= row4_t, s, -BIG)
            s_sc[kt] = s
            chunks += _lane_chunks(s)
        mx_sc[...] = jnp.maximum(mx_sc[...], functools.reduce(jnp.maximum, chunks))

    def weight_tiles(kts):
        mb = mx_sc[...]
        total = None
        for kt in kts:
            k0 = pl.multiple_of(kt * TK, TK)
            p = jnp.concatenate([jnp.exp2(c - mb).astype(bf16) for c in _lane_chunks(s_sc[kt])], axis=1)
            d = _dot(p, vs_ref[0, pl.ds(k0, TK), :])
            total = d if total is None else total + d
        acc_sc[...] += total

    def sweep(n, tiles_fn):
        def pair(j, carry):
            tiles_fn([2 * j, 2 * j + 1])
            return carry

        lax.fori_loop(0, n // 2, pair, 0)

        @pl.when(n % 2 == 1)
        def _():
            tiles_fn([n - 1])

    sweep(n_tiles - 1, lambda kts: score_tiles(kts, False))
    score_tiles([n_tiles - 1], True)

    mx_sc[...] = jnp.broadcast_to(jnp.max(mx_sc[...], axis=-1, keepdims=True), (rows, LANE))
    acc_sc[...] = jnp.zeros((rows, LANE), f32)
    sweep(n_tiles, weight_tiles)
    o_sel = _normalize(acc_sc[...], lo4)

    band = WINDOW + TQ
    w0 = pl.multiple_of(jnp.maximum(t0 - WINDOW, 0), TQ)
    wbias = wb_ref[jnp.minimum(i, WINDOW // TQ)]
    sw = _dot_nt(qa_sc[...], kw_ref[0, pl.ds(w0, band), :])
    vw = vw_ref[0, pl.ds(w0, band), :]
    o_win = []
    for r in range(GROUP):
        s = sw[r * TQ:(r + 1) * TQ] + wbias
        p = jnp.exp2(s - jnp.max(s, axis=-1, keepdims=True)).astype(bf16)
        o_win.append(_normalize(_dot(p, vw), lo))

    gates = ga_ref[0]
    gates = jnp.where(g == 0, gates, pltpu.roll(gates, LANE - GROUP, axis=1))
    mixed = []
    for r in range(GROUP):
        mixed.append(gates[:, r:r + 1] * o_cmp[r]
                     + gates[:, N_HEADS + r:N_HEADS + r + 1] * o_sel[r * TQ:(r + 1) * TQ]
                     + gates[:, 2 * N_HEADS + r:2 * N_HEADS + r + 1] * o_win[r])
    for pair in range(GROUP // 2):
        packed = jnp.where(lo, mixed[2 * pair], _swap_halves(mixed[2 * pair + 1]))
        o_ref[0, :, pair * LANE:(pair + 1) * LANE] = packed.astype(bf16)


def _attention(q, kcmp, vcmp, ks, vs, kw, vw, ga, mt, wb, n_cmp):
    B, S, _ = q.shape
    NC = kcmp.shape[2]
    n_sel = S // SEL_BLOCK
    rows = GROUP * TQ
    grid = (B, N_KV, S // TQ)
    per_g = pl.BlockSpec((1, S, LANE), lambda b, g, i: (b, 0, g))
    cmp_g = pl.BlockSpec((1, 1, NC, LANE), lambda b, g, i: (b, g, 0, 0))
    return pl.pallas_call(
        functools.partial(_attn_kernel, n_cmp=n_cmp, n_sel=n_sel),
        grid=grid,
        in_specs=[
            pl.BlockSpec((1, TQ, GROUP * LANE), lambda b, g, i: (b, i, g)),
            cmp_g, cmp_g,
            per_g, per_g, per_g, per_g,
            pl.BlockSpec((1, TQ, LANE), lambda b, g, i: (b, i, 0)),
            pl.BlockSpec(mt.shape, lambda b, g, i: (0, 0)),
            pl.BlockSpec(wb.shape, lambda b, g, i: (0, 0, 0)),
        ],
        out_specs=pl.BlockSpec((1, TQ, GROUP * HEAD_DIM), lambda b, g, i: (b, i, g)),
        out_shape=jax.ShapeDtypeStruct((B, S, ATTN_WIDTH), bf16),
        scratch_shapes=[pltpu.VMEM((rows, LANE), bf16),
                        pltpu.VMEM((S // TK, rows, TK), f32),
                        pltpu.VMEM((rows, LANE), f32),
                        pltpu.VMEM((rows, LANE), f32),
                        pltpu.VMEM((n_sel, TQ), f32),
                        pltpu.VMEM((n_sel, TQ), f32)],
        compiler_params=pltpu.CompilerParams(
            dimension_semantics=("arbitrary", "arbitrary", "arbitrary"),
            vmem_limit_bytes=VMEM_LIMIT),
        name="attention",
    )(q, kcmp, vcmp, ks, vs, kw, vw, ga, mt, wb)


def _out_mlp_kernel(x_ref, ao_ref, pm_ref, gm_ref, wba_ref, wbp_ref, wo_ref, gn_ref,
                    w1_ref, w2_ref, gf_ref, o_ref):
    a = _dot(ao_ref[...], wba_ref[...])
    b = _dot(pm_ref[...], wbp_ref[...])
    gm = gm_ref[...].astype(f32)
    merged = (gm[:, :D_MODEL] * a + gm[:, D_MODEL:] * b).astype(bf16)
    x1 = x_ref[...] + _dot(merged, wo_ref[...])
    h = _rms(x1, gn_ref[...]).astype(bf16)
    f = jnp.square(jnp.maximum(_dot(h, w1_ref[...]), 0.0)).astype(bf16)
    x2 = x1 + _dot(f, w2_ref[...])
    o_ref[...] = _rms(x2, gf_ref[...])


def _out_mlp(x2d, ao, pm, gm, wba, wbp, wo, gn, w1, w2, gf):
    T, D = x2d.shape
    tok = lambda width: pl.BlockSpec((TS_OUT, width), lambda t: (t, 0))
    const = lambda arr: pl.BlockSpec(arr.shape, lambda t: (0, 0), pipeline_mode=pl.Buffered(1))
    return pl.pallas_call(
        _out_mlp_kernel,
        grid=(T // TS_OUT,),
        in_specs=[tok(D), tok(ATTN_WIDTH), tok(POOL_WIDTH), tok(2 * D_MODEL),
                  const(wba), const(wbp), const(wo), const(gn), const(w1), const(w2), const(gf)],
        out_specs=tok(D),
        out_shape=jax.ShapeDtypeStruct((T, D), f32),
        compiler_params=pltpu.CompilerParams(
            dimension_semantics=("arbitrary",), vmem_limit_bytes=VMEM_LIMIT),
        name="out_mlp",
    )(x2d, ao, pm, gm, wba, wbp, wo, gn, w1, w2, gf)


def _rope_tables(pos):
    inv = ROPE_THETA ** (-jnp.arange(0, ROT_DIM, 2, dtype=f32) / ROT_DIM)
    ang = pos.astype(f32)[:, None] * inv
    cos, sin = jnp.cos(ang), jnp.sin(ang)
    P = pos.shape[0]
    ones = jnp.ones((P, LANE - ROT_DIM), f32)
    zeros = jnp.zeros((P, LANE - ROT_HALF), f32)
    c = jnp.concatenate([cos, cos, ones], axis=1)
    sa = jnp.concatenate([-sin, zeros], axis=1)
    sb = jnp.concatenate([jnp.zeros((P, ROT_HALF), f32), sin,
                          jnp.zeros((P, LANE - ROT_DIM), f32)], axis=1)
    return c, sa, sb


def _cmp_to_sel_t(S):
    n_cmp = (S - CMP_BLOCK) // CMP_STRIDE + 1
    n_sel = S // SEL_BLOCK
    cs = np.arange(n_cmp)[:, None] * CMP_STRIDE
    js = np.arange(n_sel)[None, :] * SEL_BLOCK
    ov = np.clip(np.minimum(cs + CMP_BLOCK, js + SEL_BLOCK) - np.maximum(cs, js), 0, None)
    m = np.zeros((S // CMP_STRIDE, n_sel), np.float32)
    m[:n_cmp] = ov / CMP_STRIDE
    return jnp.asarray(m.T, dtype=bf16), n_cmp


def _window_bias():
    band = WINDOW + TQ
    r = np.arange(TQ)[:, None]
    c = np.arange(band)[None, :]
    cases = [c <= r + case * TQ for case in range(WINDOW // TQ)]
    cases.append((c > r) & (c <= r + WINDOW))
    return jnp.asarray(np.where(np.stack(cases), 0.0, -BIG), dtype=f32)


def _layout_w_in(w):
    D = w.shape[0]
    o = 0
    wq = w[:, o:o + ATTN_WIDTH]; o += ATTN_WIDTH
    kcw, vcw, ksw, vsw, kww, vww = [w[:, o + j * KV_WIDTH:o + (j + 1) * KV_WIDTH] for j in range(6)]
    o += 6 * KV_WIDTH
    gaw = w[:, o:o + N_BRANCH * N_HEADS]; o += N_BRANCH * N_HEADS
    uw = w[:, o:o + POOL_WIDTH]; o += POOL_WIDTH
    gmw = w[:, o:]
    gaw = jnp.concatenate([gaw, jnp.zeros((D, LANE - N_BRANCH * N_HEADS), w.dtype)], axis=1)
    out = jnp.concatenate([wq, ksw, kww, vsw, vww, kcw, vcw, uw, gaw, gmw], axis=1)
    assert out.shape[1] == W_COLS
    return out.astype(bf16)


def _block_diag_kv(m):
    z = jnp.zeros_like(m)
    return jnp.concatenate([jnp.concatenate([m, z], axis=-1), jnp.concatenate([z, m], axis=-1)], axis=-2)


def kernel(x, norm_mix, w_in, cmp_pe_k, cmp_pe_v, cmp_k_w1, cmp_k_w2, cmp_v_w1, cmp_v_w2,
           w_branch_attn, pool_w, pool_scale, w_branch_pool, w_out, norm_mlp, w_ff1, w_ff2,
           norm_final):
    B, S, D = x.shape
    assert norm_mix.shape[0] == 1, "single-layer block: the final norm is fused into out_mlp"
    assert S % TS_IN == 0 and S % TQ == 0 and (B * S) % TS_OUT == 0
    assert TK % TQ == 0 and S % TK == 0 and WINDOW % TQ == 0 and N_KV == 2 and GROUP % 2 == 0
    n_chunk = S // CMP_STRIDE
    rc, rsa, rsb = _rope_tables(jnp.arange(S))
    crc, crsa, crsb = _rope_tables(jnp.arange(n_chunk) * CMP_STRIDE + CMP_BLOCK - 1)
    mt, n_cmp = _cmp_to_sel_t(S)

    (q, ks, kw, vs, vw, kc, vc, ga, pm, gm) = _in_proj(
        x, norm_mix[0][None, :], _layout_w_in(w_in[0]), rc, rsa, rsb,
        pool_w[0].astype(bf16), pool_scale[0][None, :])

    def first_layer(w1):
        return _block_diag_kv(w1.reshape(CMP_BLOCK, HEAD_DIM, CMP_HIDDEN)).astype(bf16)

    def second_layer(w2):
        return _block_diag_kv(jnp.concatenate([w2, jnp.zeros_like(w2)], axis=1)).astype(bf16)

    kcmp, vcmp = _compress(
        kc, vc, jnp.tile(cmp_pe_k[0], (1, N_KV)), jnp.tile(cmp_pe_v[0], (1, N_KV)),
        first_layer(cmp_k_w1[0]), second_layer(cmp_k_w2[0]),
        first_layer(cmp_v_w1[0]), second_layer(cmp_v_w2[0]), crc, crsa, crsb)

    ao = _attention(q, kcmp, vcmp, ks, vs, kw, vw, ga, mt, _window_bias(), n_cmp)

    y = _out_mlp(x.reshape(B * S, D), ao.reshape(B * S, ATTN_WIDTH), pm.reshape(B * S, POOL_WIDTH),
                 gm.reshape(B * S, 2 * D_MODEL), w_branch_attn[0].astype(bf16),
                 w_branch_pool[0].astype(bf16), w_out[0].astype(bf16), norm_mlp[0][None, :],
                 w_ff1[0].astype(bf16), w_ff2[0].astype(bf16), norm_final[None, :])
    return y.reshape(B, S, D)
```

```python
import functools
import math

import jax
import jax.numpy as jnp
import numpy as np
from jax import lax
from jax.experimental import pallas as pl
from jax.experimental.pallas import tpu as pltpu

f32 = jnp.float32
bf16 = jnp.bfloat16

D_MODEL = 1024
N_HEADS = 8
HEAD_DIM = 64
N_KV = 2
GROUP = N_HEADS // N_KV
ROT_DIM = HEAD_DIM // 4
ROT_HALF = ROT_DIM // 2
ROPE_THETA = 500000.0
CMP_BLOCK = 32
CMP_STRIDE = 16
CMP_RATIO = CMP_BLOCK // CMP_STRIDE
CMP_HIDDEN = 4 * HEAD_DIM
SEL_BLOCK = 64
SEL_SHIFT = 6
SEL_TOPK = 16
WINDOW = 512
N_BRANCH = 3
ATTN_WIDTH = N_HEADS * HEAD_DIM
KV_WIDTH = N_KV * HEAD_DIM
POOL_WIDTH = 512
POOL_WINDOWS = (2, 4, 8, 16)
POOL_GROUP = POOL_WIDTH // len(POOL_WINDOWS)
POOL_HALO = 16
D_FF = 4 * D_MODEL
EPS = 1e-6
NEG_INF = -1e30
FORCE_SCORE = 1e4
BIG = 2.0 ** 100
M_INIT = -3.0e38
Q_SCALE = HEAD_DIM ** -0.5 * math.log2(math.e)

LANE = 128
SUBLANE = 8
VMEM_LIMIT = 56 * 1024 * 1024

SEG_Q = (0, ATTN_WIDTH)
SEG_KS = (SEG_Q[1], SEG_Q[1] + KV_WIDTH)
SEG_KW = (SEG_KS[1], SEG_KS[1] + KV_WIDTH)
SEG_VS = (SEG_KW[1], SEG_KW[1] + KV_WIDTH)
SEG_VW = (SEG_VS[1], SEG_VS[1] + KV_WIDTH)
SEG_KC = (SEG_VW[1], SEG_VW[1] + KV_WIDTH)
SEG_VC = (SEG_KC[1], SEG_KC[1] + KV_WIDTH)
SEG_U = (SEG_VC[1], SEG_VC[1] + POOL_WIDTH)
SEG_GM = (SEG_U[1], SEG_U[1] + 2 * D_MODEL)
SEG_GA = (SEG_GM[1], SEG_GM[1] + LANE)
W_COLS = SEG_GA[1]

TS_IN = 512
TQ = 256
TK = 512
SWEEP_UNROLL = 4
TS_OUT = 256


def _dot(a, b):
    return jnp.dot(a, b, preferred_element_type=f32)


def _dot_nt(a, b):
    return lax.dot_general(a, b, (((1,), (1,)), ((), ())), preferred_element_type=f32)


def _swap_halves(x):
    return pltpu.roll(x, HEAD_DIM, axis=1)


def _rope_lane_tile(x, c, sa, sb):
    return (x * c + pltpu.roll(x, LANE - ROT_HALF, axis=1) * sa
            + pltpu.roll(x, ROT_HALF, axis=1) * sb)


def _rms(x, g):
    ms = jnp.mean(x * x, axis=-1, keepdims=True)
    return x * lax.rsqrt(ms + EPS) * g


def _lane_chunks(s):
    return [s[:, c * LANE:(c + 1) * LANE] for c in range(s.shape[1] // LANE)]


def _normalize(pv, lo):
    inv = 1.0 / jnp.where(lo, 1.0, pv)
    return pv * _swap_halves(inv)


def _in_proj_kernel(x_ref, g_ref, w_ref, rc_ref, rsa_ref, rsb_ref, pw_ref, ps_ref,
                    q_ref, ks_ref, kw_ref, vs_ref, vw_ref, kc_ref, vc_ref, ga_ref, pm_ref, gm_ref,
                    ubuf):
    si = pl.program_id(1)
    t0 = si * TS_IN
    h = _rms(x_ref[0], g_ref[...]).astype(bf16)

    def seg(s):
        return _dot(h, w_ref[:, s[0]:s[1]])

    rc, rsa, rsb = rc_ref[...], rsa_ref[...], rsb_ref[...]
    row_t = t0 + lax.broadcasted_iota(jnp.int32, (TS_IN, LANE), 0)
    lane = lax.broadcasted_iota(jnp.int32, (TS_IN, LANE), 1)
    lo = lane < HEAD_DIM

    def head_tiles(x):
        return (x, _swap_halves(x))

    q = seg(SEG_Q)
    for m, chunk in enumerate(_lane_chunks(q)):
        for half, xh in enumerate(head_tiles(chunk)):
            hh = 2 * m + half
            rot = _rope_lane_tile(xh, rc, rsa, rsb) * Q_SCALE
            q_ref[0, :, hh * LANE:(hh + 1) * LANE] = jnp.where(lo, rot, 0.0).astype(bf16)

    onehot = jnp.where(lane - HEAD_DIM == (row_t >> SEL_SHIFT), BIG, 0.0).astype(f32)
    k_both = _lane_chunks(seg((SEG_KS[0], SEG_KW[1])))
    v_both = _lane_chunks(seg((SEG_VS[0], SEG_VW[1])))
    c_both = _lane_chunks(seg((SEG_KC[0], SEG_VC[1])))
    for gg, (ksh, kwh, vsh, vwh) in enumerate(zip(head_tiles(k_both[0]), head_tiles(k_both[1]),
                                                   head_tiles(v_both[0]), head_tiles(v_both[1]))):
        sl = slice(gg * LANE, (gg + 1) * LANE)
        ks_ref[0, :, sl] = jnp.where(lo, _rope_lane_tile(ksh, rc, rsa, rsb), onehot).astype(bf16)
        kw_ref[0, :, sl] = jnp.where(lo, _rope_lane_tile(kwh, rc, rsa, rsb), 0.0).astype(bf16)
        vs_ref[0, :, sl] = jnp.where(lo, vsh, 1.0).astype(bf16)
        vw_ref[0, :, sl] = jnp.where(lo, vwh, 1.0).astype(bf16)

    kc_ref[0] = c_both[0]
    vc_ref[0] = c_both[1]
    ga_ref[0] = jax.nn.sigmoid(seg(SEG_GA))
    gm_ref[0] = jax.nn.sigmoid(seg(SEG_GM)).astype(bf16)

    u = seg(SEG_U)

    @pl.when(si == 0)
    def _():
        ubuf[0:POOL_HALO, :] = jnp.zeros((POOL_HALO, POOL_WIDTH), f32)

    ubuf[POOL_HALO:, :] = u
    tpos1 = (t0 + 1 + lax.broadcasted_iota(jnp.int32, (TS_IN, 1), 0)).astype(f32)
    for gi, w in enumerate(POOL_WINDOWS):
        sl = slice(gi * POOL_GROUP, (gi + 1) * POOL_GROUP)
        acc = u[:, sl]
        for k in range(1, w):
            acc = acc + ubuf[POOL_HALO - k:POOL_HALO - k + TS_IN, sl]
        cnt = jnp.minimum(tpos1, float(w))
        pooled = (acc / cnt - u[:, sl]).astype(bf16)
        mixed = _dot(pooled, pw_ref[gi]) * ps_ref[:, sl]
        pm_ref[0, :, sl] = mixed.astype(bf16)
    ubuf[0:POOL_HALO, :] = ubuf[TS_IN:TS_IN + POOL_HALO, :]


def _in_proj(x, g, w, rc, rsa, rsb, pw, ps):
    B, S, D = x.shape
    grid = (B, S // TS_IN)
    tok = lambda width: pl.BlockSpec((1, TS_IN, width), lambda b, s: (b, s, 0))
    const = lambda shape: pl.BlockSpec(shape, lambda b, s: (0,) * len(shape),
                                       pipeline_mode=pl.Buffered(1))
    tab = pl.BlockSpec((TS_IN, LANE), lambda b, s: (s, 0))
    out_shapes = [
        jax.ShapeDtypeStruct((B, S, N_HEADS * LANE), bf16),
        jax.ShapeDtypeStruct((B, S, N_KV * LANE), bf16),
        jax.ShapeDtypeStruct((B, S, N_KV * LANE), bf16),
        jax.ShapeDtypeStruct((B, S, N_KV * LANE), bf16),
        jax.ShapeDtypeStruct((B, S, N_KV * LANE), bf16),
        jax.ShapeDtypeStruct((B, S, KV_WIDTH), f32),
        jax.ShapeDtypeStruct((B, S, KV_WIDTH), f32),
        jax.ShapeDtypeStruct((B, S, LANE), f32),
        jax.ShapeDtypeStruct((B, S, POOL_WIDTH), bf16),
        jax.ShapeDtypeStruct((B, S, 2 * D_MODEL), bf16),
    ]
    return pl.pallas_call(
        _in_proj_kernel,
        grid=grid,
        in_specs=[tok(D), const((1, D)), const((D, W_COLS)), tab, tab, tab,
                  const((len(POOL_WINDOWS), POOL_GROUP, POOL_GROUP)), const((1, POOL_WIDTH))],
        out_specs=[tok(s.shape[-1]) for s in out_shapes],
        out_shape=out_shapes,
        scratch_shapes=[pltpu.VMEM((POOL_HALO + TS_IN, POOL_WIDTH), f32)],
        compiler_params=pltpu.CompilerParams(
            dimension_semantics=("arbitrary", "arbitrary"), vmem_limit_bytes=VMEM_LIMIT),
        name="in_proj",
    )(x, g, w, rc, rsa, rsb, pw, ps)


def _compress_kernel(kc_ref, vc_ref, pek_ref, pev_ref, kw1_ref, kw2_ref, vw1_ref, vw2_ref,
                     rc_ref, rsa_ref, rsb_ref, kout_ref, vout_ref):
    n_chunk = kc_ref.shape[1] // CMP_STRIDE

    def hidden(src_ref, pe_ref, w1_ref):
        parts = []
        for part in range(CMP_RATIO):
            acc = None
            for tok in range(CMP_STRIDE):
                t = part * CMP_STRIDE + tok
                a = src_ref[0, pl.ds(tok, n_chunk, stride=CMP_STRIDE), :] + pe_ref[t:t + 1, :]
                d = _dot(a.astype(bf16), w1_ref[t])
                acc = d if acc is None else acc + d
            parts.append(acc)
        pre = parts[0] + pltpu.roll(parts[1], n_chunk - 1, axis=0)
        return jax.nn.gelu(pre, approximate=True).astype(bf16)

    lane = lax.broadcasted_iota(jnp.int32, (n_chunk, LANE), 1)
    lo = lane < HEAD_DIM
    kcmp = _dot(hidden(kc_ref, pek_ref, kw1_ref), kw2_ref[...])
    vcmp = _dot(hidden(vc_ref, pev_ref, vw1_ref), vw2_ref[...])
    for gg in range(N_KV):
        sl = slice(gg * LANE, (gg + 1) * LANE)
        kout_ref[0, gg] = _rope_lane_tile(kcmp[:, sl], rc_ref[...], rsa_ref[...], rsb_ref[...]).astype(bf16)
        vout_ref[0, gg] = jnp.where(lo, vcmp[:, sl], 1.0).astype(bf16)


def _compress(kc, vc, pek, pev, kw1, kw2, vw1, vw2, rc, rsa, rsb):
    B, S, W = kc.shape
    NC = S // CMP_STRIDE
    const = lambda shape: pl.BlockSpec(shape, lambda b: (0,) * len(shape))
    src = pl.BlockSpec((1, S, W), lambda b: (b, 0, 0))
    out = pl.BlockSpec((1, N_KV, NC, LANE), lambda b: (b, 0, 0, 0))
    return pl.pallas_call(
        _compress_kernel,
        grid=(B,),
        in_specs=[src, src, const(pek.shape), const(pev.shape), const(kw1.shape), const(kw2.shape),
                  const(vw1.shape), const(vw2.shape), const(rc.shape), const(rsa.shape), const(rsb.shape)],
        out_specs=[out, out],
        out_shape=[jax.ShapeDtypeStruct((B, N_KV, NC, LANE), bf16),
                   jax.ShapeDtypeStruct((B, N_KV, NC, LANE), bf16)],
        compiler_params=pltpu.CompilerParams(
            dimension_semantics=("arbitrary",), vmem_limit_bytes=VMEM_LIMIT),
        name="compress",
    )(kc, vc, pek, pev, kw1, kw2, vw1, vw2, rc, rsa, rsb)


def _attn_kernel(q_ref, kc_ref, vc_ref, ks_ref, vs_ref, kw_ref, vw_ref, ga_ref, mt_ref, wb_ref, o_ref,
                 qa_sc, s_sc, mx_sc, acc_sc, imp_sc, ah_sc, *, n_cmp, n_sel):
    g = pl.program_id(1)
    i = pl.program_id(2)
    t0 = i * TQ
    rows = GROUP * TQ
    ncp = kc_ref.shape[2]

    row_t = t0 + lax.broadcasted_iota(jnp.int32, (TQ, 1), 0)
    lane_t = t0 + lax.broadcasted_iota(jnp.int32, (1, TQ), 1)
    row4_t = t0 + (lax.broadcasted_iota(jnp.int32, (rows, 1), 0) & (TQ - 1))
    lo = lax.broadcasted_iota(jnp.int32, (TQ, LANE), 1) < HEAD_DIM
    lo4 = lax.broadcasted_iota(jnp.int32, (rows, LANE), 1) < HEAD_DIM

    kc = kc_ref[0, 0]
    vc = vc_ref[0, 0]
    cidx = lax.broadcasted_iota(jnp.int32, (1, ncp), 1)
    cvalid = ((cidx * CMP_STRIDE + (CMP_BLOCK - 1)) <= row_t) & (cidx < n_cmp)
    cbias = jnp.where(cvalid, 0.0, NEG_INF).astype(f32)
    any_valid = (row_t >= CMP_BLOCK - 1).astype(f32)
    psum = jnp.zeros((TQ, ncp), f32)
    q_heads = []
    o_cmp = []
    for r in range(GROUP):
        qr = q_ref[0, :, r * LANE:(r + 1) * LANE]
        q_heads.append(qr)
        s = _dot_nt(qr, kc) + cbias
        e = jnp.exp2(s - jnp.max(s, axis=-1, keepdims=True))
        inv = 1.0 / jnp.sum(e, axis=-1, keepdims=True)
        psum = psum + e * inv
        o_cmp.append(_dot(e.astype(bf16), vc) * (inv * any_valid))

    mt = mt_ref[...]
    p_hi = psum.astype(bf16)
    rem = psum - p_hi.astype(f32)
    p_mid = rem.astype(bf16)
    p_lo = (rem - p_mid.astype(f32)).astype(bf16)
    imp = _dot_nt(mt, p_hi) + _dot_nt(mt, p_mid) + _dot_nt(mt, p_lo)
    blk = lax.broadcasted_iota(jnp.int32, (n_sel, TQ), 0)
    cur = lane_t >> SEL_SHIFT
    imp = jnp.where(blk > cur, -FORCE_SCORE, imp)
    imp = jnp.where((blk == 0) | (blk == cur) | (blk == cur - 1), FORCE_SCORE, imp)
    imp_sc[...] = imp
    ah_sc[...] = jnp.zeros((n_sel, TQ), f32)
    sub = lax.broadcasted_iota(jnp.int32, (SUBLANE, TQ), 0)
    n_vrow = n_sel // SUBLANE
    for vj in range(n_vrow):
        @pl.when(vj * SUBLANE * SEL_BLOCK < t0 + TQ)
        def _(vj=vj):
            mine = imp_sc[vj * SUBLANE:(vj + 1) * SUBLANE, :]
            for v in range(n_vrow):
                other = imp_sc[v * SUBLANE:(v + 1) * SUBLANE, :]
                cnt = jnp.zeros((SUBLANE, TQ), f32)
                for sj in range(SUBLANE):
                    rival = jnp.broadcast_to(mine[sj:sj + 1, :], (SUBLANE, TQ))
                    if v > vj:
                        cnt = cnt + jnp.where(rival >= other, 1.0, 0.0)
                    elif v < vj:
                        cnt = cnt + jnp.where(rival > other, 1.0, 0.0)
                    else:
                        tie = jnp.where(rival == other, (sub > sj).astype(f32), 0.0)
                        cnt = cnt + jnp.where(rival > other, 1.0, tie)
                ah_sc[v * SUBLANE:(v + 1) * SUBLANE, :] += cnt
    sel_m1 = jnp.where(ah_sc[...] < float(min(SEL_TOPK, n_sel)), 0.0, -1.0).astype(f32)
    sel_pad = jnp.concatenate([jnp.zeros((LANE - n_sel, TQ), f32), sel_m1], axis=0).T
    sel_pad = sel_pad.astype(bf16)
    qa_sc[...] = jnp.concatenate([qr + sel_pad for qr in q_heads], axis=0)

    n_tiles = (t0 + TQ + TK - 1) // TK
    mx_sc[...] = jnp.full((rows, LANE), M_INIT, f32)

    def score_tiles(kts, causal):
        chunks = []
        for kt in kts:
            k0 = pl.multiple_of(kt * TK, TK)
            s = _dot_nt(qa_sc[...], ks_ref[0, pl.ds(k0, TK), :])
            if causal:
                kpos = k0 + lax.broadcasted_iota(jnp.int32, (1, TK), 1)
                s = jnp.where(kpos <= row4_t, s, -BIG)
            s_sc[kt] = s
            chunks += _lane_chunks(s)
        mx_sc[...] = jnp.maximum(mx_sc[...], functools.reduce(jnp.maximum, chunks))

    def weight_tiles(kts):
        mb = mx_sc[...]
        total = None
        for kt in kts:
            k0 = pl.multiple_of(kt * TK, TK)
            p = jnp.concatenate([jnp.exp2(c - mb).astype(bf16) for c in _lane_chunks(s_sc[kt])], axis=1)
            d = _dot(p, vs_ref[0, pl.ds(k0, TK), :])
            total = d if total is None else total + d
        acc_sc[...] += total

    def sweep(n, tiles_fn):
        def trip(j, carry):
            tiles_fn([SWEEP_UNROLL * j + u for u in range(SWEEP_UNROLL)])
            return carry

        lax.fori_loop(0, n // SWEEP_UNROLL, trip, 0)
        group = SWEEP_UNROLL // 2
        while group >= 1:
            start = (n // (2 * group)) * (2 * group)

            @pl.when((n // group) % 2 == 1)
            def _(start=start, group=group):
                tiles_fn([start + u for u in range(group)])
            group //= 2

    sweep(n_tiles - 1, lambda kts: score_tiles(kts, False))
    score_tiles([n_tiles - 1], True)

    mx_sc[...] = jnp.broadcast_to(jnp.max(mx_sc[...], axis=-1, keepdims=True), (rows, LANE))
    acc_sc[...] = jnp.zeros((rows, LANE), f32)
    sweep(n_tiles, weight_tiles)
    o_sel = _normalize(acc_sc[...], lo4)

    band = WINDOW + TQ
    w0 = pl.multiple_of(jnp.maximum(t0 - WINDOW, 0), TQ)
    wbias = wb_ref[jnp.minimum(i, WINDOW // TQ)]
    kwin = kw_ref[0, pl.ds(w0, band), :]
    vw = vw_ref[0, pl.ds(w0, band), :]
    o_win = []
    for r in range(GROUP):
        s = _dot_nt(qa_sc[r * TQ:(r + 1) * TQ, :], kwin) + wbias
        p = jnp.exp2(s - jnp.max(s, axis=-1, keepdims=True)).astype(bf16)
        o_win.append(_normalize(_dot(p, vw), lo))

    gates = ga_ref[0]
    gates = jnp.where(g == 0, gates, pltpu.roll(gates, LANE - GROUP, axis=1))
    mixed = []
    for r in range(GROUP):
        mixed.append(gates[:, r:r + 1] * o_cmp[r]
                     + gates[:, N_HEADS + r:N_HEADS + r + 1] * o_sel[r * TQ:(r + 1) * TQ]
                     + gates[:, 2 * N_HEADS + r:2 * N_HEADS + r + 1] * o_win[r])
    for pair in range(GROUP // 2):
        packed = jnp.where(lo, mixed[2 * pair], _swap_halves(mixed[2 * pair + 1]))
        o_ref[0, :, pair * LANE:(pair + 1) * LANE] = packed.astype(bf16)


def _attention(q, kcmp, vcmp, ks, vs, kw, vw, ga, mt, wb, n_cmp):
    B, S, _ = q.shape
    NC = kcmp.shape[2]
    n_sel = S // SEL_BLOCK
    rows = GROUP * TQ
    grid = (B, N_KV, S // TQ)
    per_g = pl.BlockSpec((1, S, LANE), lambda b, g, i: (b, 0, g))
    cmp_g = pl.BlockSpec((1, 1, NC, LANE), lambda b, g, i: (b, g, 0, 0))
    return pl.pallas_call(
        functools.partial(_attn_kernel, n_cmp=n_cmp, n_sel=n_sel),
        grid=grid,
        in_specs=[
            pl.BlockSpec((1, TQ, GROUP * LANE), lambda b, g, i: (b, i, g)),
            cmp_g, cmp_g,
            per_g, per_g, per_g, per_g,
            pl.BlockSpec((1, TQ, LANE), lambda b, g, i: (b, i, 0)),
            pl.BlockSpec(mt.shape, lambda b, g, i: (0, 0)),
            pl.BlockSpec(wb.shape, lambda b, g, i: (0, 0, 0)),
        ],
        out_specs=pl.BlockSpec((1, TQ, GROUP * HEAD_DIM), lambda b, g, i: (b, i, g)),
        out_shape=jax.ShapeDtypeStruct((B, S, ATTN_WIDTH), bf16),
        scratch_shapes=[pltpu.VMEM((rows, LANE), bf16),
                        pltpu.VMEM((S // TK, rows, TK), f32),
                        pltpu.VMEM((rows, LANE), f32),
                        pltpu.VMEM((rows, LANE), f32),
                        pltpu.VMEM((n_sel, TQ), f32),
                        pltpu.VMEM((n_sel, TQ), f32)],
        compiler_params=pltpu.CompilerParams(
            dimension_semantics=("arbitrary", "arbitrary", "arbitrary"),
            vmem_limit_bytes=VMEM_LIMIT),
        name="attention",
    )(q, kcmp, vcmp, ks, vs, kw, vw, ga, mt, wb)


def _out_mlp_kernel(x_ref, ao_ref, pm_ref, gm_ref, wba_ref, wbp_ref, wo_ref, gn_ref,
                    w1_ref, w2_ref, gf_ref, o_ref):
    a = _dot(ao_ref[...], wba_ref[...])
    b = _dot(pm_ref[...], wbp_ref[...])
    gm = gm_ref[...].astype(f32)
    merged = (gm[:, :D_MODEL] * a + gm[:, D_MODEL:] * b).astype(bf16)
    x1 = x_ref[...] + _dot(merged, wo_ref[...])
    h = _rms(x1, gn_ref[...]).astype(bf16)
    f = jnp.square(jnp.maximum(_dot(h, w1_ref[...]), 0.0)).astype(bf16)
    x2 = x1 + _dot(f, w2_ref[...])
    o_ref[...] = _rms(x2, gf_ref[...])


def _out_mlp(x2d, ao, pm, gm, wba, wbp, wo, gn, w1, w2, gf):
    T, D = x2d.shape
    tok = lambda width: pl.BlockSpec((TS_OUT, width), lambda t: (t, 0))
    const = lambda arr: pl.BlockSpec(arr.shape, lambda t: (0, 0), pipeline_mode=pl.Buffered(1))
    return pl.pallas_call(
        _out_mlp_kernel,
        grid=(T // TS_OUT,),
        in_specs=[tok(D), tok(ATTN_WIDTH), tok(POOL_WIDTH), tok(2 * D_MODEL),
                  const(wba), const(wbp), const(wo), const(gn), const(w1), const(w2), const(gf)],
        out_specs=tok(D),
        out_shape=jax.ShapeDtypeStruct((T, D), f32),
        compiler_params=pltpu.CompilerParams(
            dimension_semantics=("arbitrary",), vmem_limit_bytes=VMEM_LIMIT),
        name="out_mlp",
    )(x2d, ao, pm, gm, wba, wbp, wo, gn, w1, w2, gf)


def _rope_tables(pos):
    inv = ROPE_THETA ** (-jnp.arange(0, ROT_DIM, 2, dtype=f32) / ROT_DIM)
    ang = pos.astype(f32)[:, None] * inv
    cos, sin = jnp.cos(ang), jnp.sin(ang)
    P = pos.shape[0]
    ones = jnp.ones((P, LANE - ROT_DIM), f32)
    zeros = jnp.zeros((P, LANE - ROT_HALF), f32)
    c = jnp.concatenate([cos, cos, ones], axis=1)
    sa = jnp.concatenate([-sin, zeros], axis=1)
    sb = jnp.concatenate([jnp.zeros((P, ROT_HALF), f32), sin,
                          jnp.zeros((P, LANE - ROT_DIM), f32)], axis=1)
    return c, sa, sb


def _cmp_to_sel_t(S):
    n_cmp = (S - CMP_BLOCK) // CMP_STRIDE + 1
    n_sel = S // SEL_BLOCK
    cs = np.arange(n_cmp)[:, None] * CMP_STRIDE
    js = np.arange(n_sel)[None, :] * SEL_BLOCK
    ov = np.clip(np.minimum(cs + CMP_BLOCK, js + SEL_BLOCK) - np.maximum(cs, js), 0, None)
    m = np.zeros((S // CMP_STRIDE, n_sel), np.float32)
    m[:n_cmp] = ov / CMP_STRIDE
    return jnp.asarray(m.T, dtype=bf16), n_cmp


def _window_bias():
    band = WINDOW + TQ
    r = np.arange(TQ)[:, None]
    c = np.arange(band)[None, :]
    cases = [c <= r + case * TQ for case in range(WINDOW // TQ)]
    cases.append((c > r) & (c <= r + WINDOW))
    return jnp.asarray(np.where(np.stack(cases), 0.0, -BIG), dtype=f32)


def _layout_w_in(w):
    D = w.shape[0]
    o = 0
    wq = w[:, o:o + ATTN_WIDTH]; o += ATTN_WIDTH
    kcw, vcw, ksw, vsw, kww, vww = [w[:, o + j * KV_WIDTH:o + (j + 1) * KV_WIDTH] for j in range(6)]
    o += 6 * KV_WIDTH
    gaw = w[:, o:o + N_BRANCH * N_HEADS]; o += N_BRANCH * N_HEADS
    uw = w[:, o:o + POOL_WIDTH]; o += POOL_WIDTH
    gmw = w[:, o:]
    gaw = jnp.concatenate([gaw, jnp.zeros((D, LANE - N_BRANCH * N_HEADS), w.dtype)], axis=1)
    out = jnp.concatenate([wq, ksw, kww, vsw, vww, kcw, vcw, uw, gmw, gaw], axis=1)
    assert out.shape[1] == W_COLS
    return out.astype(bf16)


def _block_diag_kv(m):
    z = jnp.zeros_like(m)
    return jnp.concatenate([jnp.concatenate([m, z], axis=-1), jnp.concatenate([z, m], axis=-1)], axis=-2)


def kernel(x, norm_mix, w_in, cmp_pe_k, cmp_pe_v, cmp_k_w1, cmp_k_w2, cmp_v_w1, cmp_v_w2,
           w_branch_attn, pool_w, pool_scale, w_branch_pool, w_out, norm_mlp, w_ff1, w_ff2,
           norm_final):
    B, S, D = x.shape
    assert norm_mix.shape[0] == 1, "single-layer block: the final norm is fused into out_mlp"
    assert S % TS_IN == 0 and S % TQ == 0 and (B * S) % TS_OUT == 0
    assert TK % TQ == 0 and S % TK == 0 and WINDOW % TQ == 0 and N_KV == 2 and GROUP % 2 == 0
    n_chunk = S // CMP_STRIDE
    rc, rsa, rsb = _rope_tables(jnp.arange(S))
    crc, crsa, crsb = _rope_tables(jnp.arange(n_chunk) * CMP_STRIDE + CMP_BLOCK - 1)
    mt, n_cmp = _cmp_to_sel_t(S)

    (q, ks, kw, vs, vw, kc, vc, ga, pm, gm) = _in_proj(
        x, norm_mix[0][None, :], _layout_w_in(w_in[0]), rc, rsa, rsb,
        pool_w[0].astype(bf16), pool_scale[0][None, :])

    def first_layer(w1):
        return _block_diag_kv(w1.reshape(CMP_BLOCK, HEAD_DIM, CMP_HIDDEN)).astype(bf16)

    def second_layer(w2):
        return _block_diag_kv(jnp.concatenate([w2, jnp.zeros_like(w2)], axis=1)).astype(bf16)

    kcmp, vcmp = _compress(
        kc, vc, jnp.tile(cmp_pe_k[0], (1, N_KV)), jnp.tile(cmp_pe_v[0], (1, N_KV)),
        first_layer(cmp_k_w1[0]), second_layer(cmp_k_w2[0]),
        first_layer(cmp_v_w1[0]), second_layer(cmp_v_w2[0]), crc, crsa, crsb)

    ao = _attention(q, kcmp, vcmp, ks, vs, kw, vw, ga, mt, _window_bias(), n_cmp)

    y = _out_mlp(x.reshape(B * S, D), ao.reshape(B * S, ATTN_WIDTH), pm.reshape(B * S, POOL_WIDTH),
                 gm.reshape(B * S, 2 * D_MODEL), w_branch_attn[0].astype(bf16),
                 w_branch_pool[0].astype(bf16), w_out[0].astype(bf16), norm_mlp[0][None, :],
                 w_ff1[0].astype(bf16), w_ff2[0].astype(bf16), norm_final[None, :])
    return y.reshape(B, S, D)
```

```python
import functools
import math

import jax
import jax.numpy as jnp
import numpy as np
from jax import lax
from jax.experimental import pallas as pl
from jax.experimental.pallas import tpu as pltpu

f32 = jnp.float32
bf16 = jnp.bfloat16

D_MODEL = 1024
N_HEADS = 8
HEAD_DIM = 64
N_KV = 2
GROUP = N_HEADS // N_KV
ROT_DIM = HEAD_DIM // 4
ROT_HALF = ROT_DIM // 2
ROPE_THETA = 500000.0
CMP_BLOCK = 32
CMP_STRIDE = 16
CMP_RATIO = CMP_BLOCK // CMP_STRIDE
CMP_HIDDEN = 4 * HEAD_DIM
SEL_BLOCK = 64
SEL_SHIFT = 6
SEL_TOPK = 16
WINDOW = 512
N_BRANCH = 3
ATTN_WIDTH = N_HEADS * HEAD_DIM
KV_WIDTH = N_KV * HEAD_DIM
POOL_WIDTH = 512
POOL_WINDOWS = (2, 4, 8, 16)
POOL_GROUP = POOL_WIDTH // len(POOL_WINDOWS)
POOL_HALO = 16
D_FF = 4 * D_MODEL
EPS = 1e-6
NEG_INF = -1e30
FORCE_SCORE = 1e4
BIG = 2.0 ** 100
M_INIT = -3.0e38
Q_SCALE = HEAD_DIM ** -0.5 * math.log2(math.e)

LANE = 128
SUBLANE = 8
VMEM_LIMIT = 56 * 1024 * 1024

SEG_Q = (0, ATTN_WIDTH)
SEG_KS = (SEG_Q[1], SEG_Q[1] + KV_WIDTH)
SEG_KW = (SEG_KS[1], SEG_KS[1] + KV_WIDTH)
SEG_VS = (SEG_KW[1], SEG_KW[1] + KV_WIDTH)
SEG_VW = (SEG_VS[1], SEG_VS[1] + KV_WIDTH)
SEG_KC = (SEG_VW[1], SEG_VW[1] + KV_WIDTH)
SEG_VC = (SEG_KC[1], SEG_KC[1] + KV_WIDTH)
SEG_U = (SEG_VC[1], SEG_VC[1] + POOL_WIDTH)
SEG_GM = (SEG_U[1], SEG_U[1] + 2 * D_MODEL)
SEG_GA = (SEG_GM[1], SEG_GM[1] + LANE)
W_COLS = SEG_GA[1]

TS_IN = 512
TQ = 256
TK = 512
SWEEP_UNROLL = 4
TS_OUT = 512


def _dot(a, b):
    return jnp.dot(a, b, preferred_element_type=f32)


def _dot_nt(a, b):
    return lax.dot_general(a, b, (((1,), (1,)), ((), ())), preferred_element_type=f32)


def _swap_halves(x):
    return pltpu.roll(x, HEAD_DIM, axis=1)


def _rope_lane_tile(x, c, sa, sb):
    return (x * c + pltpu.roll(x, LANE - ROT_HALF, axis=1) * sa
            + pltpu.roll(x, ROT_HALF, axis=1) * sb)


def _rms(x, g):
    ms = jnp.mean(x * x, axis=-1, keepdims=True)
    return x * lax.rsqrt(ms + EPS) * g


def _lane_chunks(s):
    return [s[:, c * LANE:(c + 1) * LANE] for c in range(s.shape[1] // LANE)]


def _normalize(pv, lo):
    inv = 1.0 / jnp.where(lo, 1.0, pv)
    return pv * _swap_halves(inv)


def _in_proj_kernel(x_ref, g_ref, w_ref, rc_ref, rsa_ref, rsb_ref, pw_ref, ps_ref,
                    q_ref, ks_ref, kw_ref, vs_ref, vw_ref, kc_ref, vc_ref, ga_ref, pm_ref, gm_ref,
                    ubuf):
    si = pl.program_id(1)
    t0 = si * TS_IN

    @pl.when(si == 0)
    def _():
        ubuf[0:POOL_HALO, :] = jnp.zeros((POOL_HALO, POOL_WIDTH), f32)

    h = _rms(x_ref[0], g_ref[...]).astype(bf16)

    def seg(s):
        return _dot(h, w_ref[:, s[0]:s[1]])

    u = seg(SEG_U)
    ubuf[POOL_HALO:, :] = u
    tpos1 = (t0 + 1 + lax.broadcasted_iota(jnp.int32, (TS_IN, 1), 0)).astype(f32)

    def pool_steps():
        for gi, w in enumerate(POOL_WINDOWS):
            sl = slice(gi * POOL_GROUP, (gi + 1) * POOL_GROUP)
            acc = u[:, sl]
            for k in range(1, w):
                acc = acc + ubuf[POOL_HALO - k:POOL_HALO - k + TS_IN, sl]
                yield
            cnt = jnp.minimum(tpos1, float(w))
            pooled = (acc / cnt - u[:, sl]).astype(bf16)
            mixed = _dot(pooled, pw_ref[gi]) * ps_ref[:, sl]
            pm_ref[0, :, sl] = mixed.astype(bf16)
            yield
        ubuf[0:POOL_HALO, :] = ubuf[TS_IN:TS_IN + POOL_HALO, :]

    pool = pool_steps()

    def pool_advance(n):
        for _ in range(n):
            next(pool, None)

    rc, rsa, rsb = rc_ref[...], rsa_ref[...], rsb_ref[...]
    row_t = t0 + lax.broadcasted_iota(jnp.int32, (TS_IN, LANE), 0)
    lane = lax.broadcasted_iota(jnp.int32, (TS_IN, LANE), 1)
    lo = lane < HEAD_DIM

    def head_tiles(x):
        return (x, _swap_halves(x))

    q = seg(SEG_Q)
    pool_advance(4)
    for m, chunk in enumerate(_lane_chunks(q)):
        for half, xh in enumerate(head_tiles(chunk)):
            hh = 2 * m + half
            rot = _rope_lane_tile(xh, rc, rsa, rsb) * Q_SCALE
            q_ref[0, :, hh * LANE:(hh + 1) * LANE] = jnp.where(lo, rot, 0.0).astype(bf16)

    onehot = jnp.where(lane - HEAD_DIM == (row_t >> SEL_SHIFT), BIG, 0.0).astype(f32)
    k_both = _lane_chunks(seg((SEG_KS[0], SEG_KW[1])))
    pool_advance(2)
    v_both = _lane_chunks(seg((SEG_VS[0], SEG_VW[1])))
    pool_advance(2)
    c_both = _lane_chunks(seg((SEG_KC[0], SEG_VC[1])))
    pool_advance(2)
    for gg, (ksh, kwh, vsh, vwh) in enumerate(zip(head_tiles(k_both[0]), head_tiles(k_both[1]),
                                                   head_tiles(v_both[0]), head_tiles(v_both[1]))):
        sl = slice(gg * LANE, (gg + 1) * LANE)
        ks_ref[0, :, sl] = jnp.where(lo, _rope_lane_tile(ksh, rc, rsa, rsb), onehot).astype(bf16)
        kw_ref[0, :, sl] = jnp.where(lo, _rope_lane_tile(kwh, rc, rsa, rsb), 0.0).astype(bf16)
        vs_ref[0, :, sl] = jnp.where(lo, vsh, 1.0).astype(bf16)
        vw_ref[0, :, sl] = jnp.where(lo, vwh, 1.0).astype(bf16)

    kc_ref[0] = c_both[0]
    vc_ref[0] = c_both[1]
    ga_ref[0] = jax.nn.sigmoid(seg(SEG_GA))
    gm_cols = 2 * LANE * 2
    for c0 in range(SEG_GM[0], SEG_GM[1], gm_cols):
        part = seg((c0, c0 + gm_cols))
        pool_advance(5)
        gm_ref[0, :, c0 - SEG_GM[0]:c0 - SEG_GM[0] + gm_cols] = jax.nn.sigmoid(part).astype(bf16)
    for _ in pool:
        pass


def _in_proj(x, g, w, rc, rsa, rsb, pw, ps):
    B, S, D = x.shape
    grid = (B, S // TS_IN)
    tok = lambda width: pl.BlockSpec((1, TS_IN, width), lambda b, s: (b, s, 0))
    const = lambda shape: pl.BlockSpec(shape, lambda b, s: (0,) * len(shape),
                                       pipeline_mode=pl.Buffered(1))
    tab = pl.BlockSpec((TS_IN, LANE), lambda b, s: (s, 0))
    out_shapes = [
        jax.ShapeDtypeStruct((B, S, N_HEADS * LANE), bf16),
        jax.ShapeDtypeStruct((B, S, N_KV * LANE), bf16),
        jax.ShapeDtypeStruct((B, S, N_KV * LANE), bf16),
        jax.ShapeDtypeStruct((B, S, N_KV * LANE), bf16),
        jax.ShapeDtypeStruct((B, S, N_KV * LANE), bf16),
        jax.ShapeDtypeStruct((B, S, KV_WIDTH), f32),
        jax.ShapeDtypeStruct((B, S, KV_WIDTH), f32),
        jax.ShapeDtypeStruct((B, S, LANE), f32),
        jax.ShapeDtypeStruct((B, S, POOL_WIDTH), bf16),
        jax.ShapeDtypeStruct((B, S, 2 * D_MODEL), bf16),
    ]
    return pl.pallas_call(
        _in_proj_kernel,
        grid=grid,
        in_specs=[tok(D), const((1, D)), const((D, W_COLS)), tab, tab, tab,
                  const((len(POOL_WINDOWS), POOL_GROUP, POOL_GROUP)), const((1, POOL_WIDTH))],
        out_specs=[tok(s.shape[-1]) for s in out_shapes],
        out_shape=out_shapes,
        scratch_shapes=[pltpu.VMEM((POOL_HALO + TS_IN, POOL_WIDTH), f32)],
        compiler_params=pltpu.CompilerParams(
            dimension_semantics=("arbitrary", "arbitrary"), vmem_limit_bytes=VMEM_LIMIT),
        name="in_proj",
    )(x, g, w, rc, rsa, rsb, pw, ps)


def _compress_kernel(kc_ref, vc_ref, pek_ref, pev_ref, kw1_ref, kw2_ref, vw1_ref, vw2_ref,
                     rc_ref, rsa_ref, rsb_ref, kout_ref, vout_ref):
    n_chunk = kc_ref.shape[1] // CMP_STRIDE

    def hidden(src_ref, pe_ref, w1_ref):
        parts = []
        for part in range(CMP_RATIO):
            acc = None
            for tok in range(CMP_STRIDE):
                t = part * CMP_STRIDE + tok
                a = src_ref[0, pl.ds(tok, n_chunk, stride=CMP_STRIDE), :] + pe_ref[t:t + 1, :]
                d = _dot(a.astype(bf16), w1_ref[t])
                acc = d if acc is None else acc + d
            parts.append(acc)
        pre = parts[0] + pltpu.roll(parts[1], n_chunk - 1, axis=0)
        return jax.nn.gelu(pre, approximate=True).astype(bf16)

    lane = lax.broadcasted_iota(jnp.int32, (n_chunk, LANE), 1)
    lo = lane < HEAD_DIM
    kcmp = _dot(hidden(kc_ref, pek_ref, kw1_ref), kw2_ref[...])
    vcmp = _dot(hidden(vc_ref, pev_ref, vw1_ref), vw2_ref[...])
    for gg in range(N_KV):
        sl = slice(gg * LANE, (gg + 1) * LANE)
        kout_ref[0, gg] = _rope_lane_tile(kcmp[:, sl], rc_ref[...], rsa_ref[...], rsb_ref[...]).astype(bf16)
        vout_ref[0, gg] = jnp.where(lo, vcmp[:, sl], 1.0).astype(bf16)


def _compress(kc, vc, pek, pev, kw1, kw2, vw1, vw2, rc, rsa, rsb):
    B, S, W = kc.shape
    NC = S // CMP_STRIDE
    const = lambda shape: pl.BlockSpec(shape, lambda b: (0,) * len(shape))
    src = pl.BlockSpec((1, S, W), lambda b: (b, 0, 0))
    out = pl.BlockSpec((1, N_KV, NC, LANE), lambda b: (b, 0, 0, 0))
    return pl.pallas_call(
        _compress_kernel,
        grid=(B,),
        in_specs=[src, src, const(pek.shape), const(pev.shape), const(kw1.shape), const(kw2.shape),
                  const(vw1.shape), const(vw2.shape), const(rc.shape), const(rsa.shape), const(rsb.shape)],
        out_specs=[out, out],
        out_shape=[jax.ShapeDtypeStruct((B, N_KV, NC, LANE), bf16),
                   jax.ShapeDtypeStruct((B, N_KV, NC, LANE), bf16)],
        compiler_params=pltpu.CompilerParams(
            dimension_semantics=("arbitrary",), vmem_limit_bytes=VMEM_LIMIT),
        name="compress",
    )(kc, vc, pek, pev, kw1, kw2, vw1, vw2, rc, rsa, rsb)


def _attn_kernel(q_ref, kc_ref, vc_ref, ks_ref, vs_ref, kw_ref, vw_ref, ga_ref, mt_ref, wb_ref, o_ref,
                 qa_sc, s_sc, mx_sc, acc_sc, *, n_cmp, n_sel):
    g = pl.program_id(1)
    i = pl.program_id(2)
    t0 = i * TQ
    rows = GROUP * TQ
    ncp = kc_ref.shape[2]

    row_t = t0 + lax.broadcasted_iota(jnp.int32, (TQ, 1), 0)
    lane_t = t0 + lax.broadcasted_iota(jnp.int32, (1, TQ), 1)
    row4_t = t0 + (lax.broadcasted_iota(jnp.int32, (rows, 1), 0) & (TQ - 1))
    lo = lax.broadcasted_iota(jnp.int32, (TQ, LANE), 1) < HEAD_DIM
    lo4 = lax.broadcasted_iota(jnp.int32, (rows, LANE), 1) < HEAD_DIM

    q_heads = [q_ref[0, :, r * LANE:(r + 1) * LANE] for r in range(GROUP)]

    band = WINDOW + TQ
    w0 = pl.multiple_of(jnp.maximum(t0 - WINDOW, 0), TQ)
    wbias = wb_ref[jnp.minimum(i, WINDOW // TQ)]
    kwin = kw_ref[0, pl.ds(w0, band), :]
    vw = vw_ref[0, pl.ds(w0, band), :]
    kc = kc_ref[0, 0]
    vc = vc_ref[0, 0]
    cidx = lax.broadcasted_iota(jnp.int32, (1, ncp), 1)
    cvalid = ((cidx * CMP_STRIDE + (CMP_BLOCK - 1)) <= row_t) & (cidx < n_cmp)
    cbias = jnp.where(cvalid, 0.0, NEG_INF).astype(f32)
    any_valid = (row_t >= CMP_BLOCK - 1).astype(f32)
    head_rows = lambda a, r: a[r * TQ:(r + 1) * TQ]
    q_all = jnp.concatenate(q_heads, axis=0)
    s_cmp = _dot_nt(q_all, kc)
    s_win = _dot_nt(q_all, kwin)
    psum = jnp.zeros((TQ, ncp), f32)
    e_cmp = []
    inv_cmp = []
    for r in range(GROUP):
        s = head_rows(s_cmp, r) + cbias
        e = jnp.exp2(s - jnp.max(s, axis=-1, keepdims=True))
        inv = 1.0 / jnp.sum(e, axis=-1, keepdims=True)
        psum = psum + e * inv
        e_cmp.append(e.astype(bf16))
        inv_cmp.append(inv * any_valid)
    pv_cmp = _dot(jnp.concatenate(e_cmp, axis=0), vc)
    o_cmp = [head_rows(pv_cmp, r) * inv_cmp[r] for r in range(GROUP)]

    mt = mt_ref[...]
    p_hi = psum.astype(bf16)
    rem = psum - p_hi.astype(f32)
    p_mid = rem.astype(bf16)
    p_lo = (rem - p_mid.astype(f32)).astype(bf16)
    imp = _dot_nt(mt, p_hi) + _dot_nt(mt, p_mid) + _dot_nt(mt, p_lo)
    blk = lax.broadcasted_iota(jnp.int32, (n_sel, TQ), 0)
    cur = lane_t >> SEL_SHIFT
    imp = jnp.where(blk > cur, -FORCE_SCORE, imp)
    imp = jnp.where((blk == 0) | (blk == cur) | (blk == cur - 1), FORCE_SCORE, imp)

    sub = lax.broadcasted_iota(jnp.int32, (SUBLANE, TQ), 0)
    n_vrow = n_sel // SUBLANE
    vrows = [imp[v * SUBLANE:(v + 1) * SUBLANE, :] for v in range(n_vrow)]
    ahead = [jnp.zeros((SUBLANE, TQ), f32) for _ in range(n_vrow)]

    def rank_against(vj):
        for sj in range(SUBLANE):
            rival = jnp.broadcast_to(vrows[vj][sj:sj + 1, :], (SUBLANE, TQ))
            for v in range(n_vrow):
                if v > vj:
                    ahead[v] = ahead[v] + jnp.where(rival >= vrows[v], 1.0, 0.0)
                elif v < vj:
                    ahead[v] = ahead[v] + jnp.where(rival > vrows[v], 1.0, 0.0)
                else:
                    tie = jnp.where(rival == vrows[v], (sub > sj).astype(f32), 0.0)
                    ahead[v] = ahead[v] + jnp.where(rival > vrows[v], 1.0, tie)

    p_win = []
    for r in range(GROUP):
        s = head_rows(s_win, r) + wbias
        p_win.append(jnp.exp2((s - jnp.max(s, axis=-1, keepdims=True)).astype(bf16)))
        for vj in range(r * n_vrow // GROUP, (r + 1) * n_vrow // GROUP):
            rank_against(vj)
    o_win = _normalize(_dot(jnp.concatenate(p_win, axis=0), vw), lo4)
    sel_m1 = jnp.where(jnp.concatenate(ahead, axis=0) < float(min(SEL_TOPK, n_sel)), 0.0, -1.0)
    sel_pad = jnp.concatenate([jnp.zeros((LANE - n_sel, TQ), f32), sel_m1.astype(f32)], axis=0).T
    sel_pad = sel_pad.astype(bf16)
    qa_sc[...] = jnp.concatenate([qr + sel_pad for qr in q_heads], axis=0)

    n_tiles = (t0 + TQ + TK - 1) // TK
    mx_sc[...] = jnp.full((rows, LANE), M_INIT, f32)

    def score_tiles(kts, causal):
        chunks = []
        for kt in kts:
            k0 = pl.multiple_of(kt * TK, TK)
            s = _dot_nt(qa_sc[...], ks_ref[0, pl.ds(k0, TK), :])
            if causal:
                kpos = k0 + lax.broadcasted_iota(jnp.int32, (1, TK), 1)
                s = jnp.where(kpos <= row4_t, s, -BIG)
            s_sc[kt] = s
            chunks += _lane_chunks(s)
        mx_sc[...] = jnp.maximum(mx_sc[...], functools.reduce(jnp.maximum, chunks))

    def weight_tiles(kts):
        mb = mx_sc[...]
        total = None
        for kt in kts:
            k0 = pl.multiple_of(kt * TK, TK)
            p = jnp.concatenate([jnp.exp2((c - mb).astype(bf16)) for c in _lane_chunks(s_sc[kt])], axis=1)
            d = _dot(p, vs_ref[0, pl.ds(k0, TK), :])
            total = d if total is None else total + d
        acc_sc[...] += total

    def sweep(n, tiles_fn):
        def trip(j, carry):
            tiles_fn([SWEEP_UNROLL * j + u for u in range(SWEEP_UNROLL)])
            return carry

        lax.fori_loop(0, n // SWEEP_UNROLL, trip, 0)
        group = SWEEP_UNROLL // 2
        while group >= 1:
            start = (n // (2 * group)) * (2 * group)

            @pl.when((n // group) % 2 == 1)
            def _(start=start, group=group):
                tiles_fn([start + u for u in range(group)])
            group //= 2

    sweep(n_tiles - 1, lambda kts: score_tiles(kts, False))
    score_tiles([n_tiles - 1], True)

    mx_sc[...] = jnp.broadcast_to(jnp.max(mx_sc[...], axis=-1, keepdims=True), (rows, LANE))
    acc_sc[...] = jnp.zeros((rows, LANE), f32)
    sweep(n_tiles, weight_tiles)
    o_sel = _normalize(acc_sc[...], lo4)

    gates = ga_ref[0]
    gates = jnp.where(g == 0, gates, pltpu.roll(gates, LANE - GROUP, axis=1))
    mixed = []
    for r in range(GROUP):
        mixed.append(gates[:, r:r + 1] * o_cmp[r]
                     + gates[:, N_HEADS + r:N_HEADS + r + 1] * o_sel[r * TQ:(r + 1) * TQ]
                     + gates[:, 2 * N_HEADS + r:2 * N_HEADS + r + 1] * o_win[r * TQ:(r + 1) * TQ])
    for pair in range(GROUP // 2):
        packed = jnp.where(lo, mixed[2 * pair], _swap_halves(mixed[2 * pair + 1]))
        o_ref[0, :, pair * LANE:(pair + 1) * LANE] = packed.astype(bf16)


def _attention(q, kcmp, vcmp, ks, vs, kw, vw, ga, mt, wb, n_cmp):
    B, S, _ = q.shape
    NC = kcmp.shape[2]
    n_sel = S // SEL_BLOCK
    rows = GROUP * TQ
    grid = (B, N_KV, S // TQ)
    per_g = pl.BlockSpec((1, S, LANE), lambda b, g, i: (b, 0, g))
    cmp_g = pl.BlockSpec((1, 1, NC, LANE), lambda b, g, i: (b, g, 0, 0))
    return pl.pallas_call(
        functools.partial(_attn_kernel, n_cmp=n_cmp, n_sel=n_sel),
        grid=grid,
        in_specs=[
            pl.BlockSpec((1, TQ, GROUP * LANE), lambda b, g, i: (b, i, g)),
            cmp_g, cmp_g,
            per_g, per_g, per_g, per_g,
            pl.BlockSpec((1, TQ, LANE), lambda b, g, i: (b, i, 0)),
            pl.BlockSpec(mt.shape, lambda b, g, i: (0, 0)),
            pl.BlockSpec(wb.shape, lambda b, g, i: (0, 0, 0)),
        ],
        out_specs=pl.BlockSpec((1, TQ, GROUP * HEAD_DIM), lambda b, g, i: (b, i, g)),
        out_shape=jax.ShapeDtypeStruct((B, S, ATTN_WIDTH), bf16),
        scratch_shapes=[pltpu.VMEM((rows, LANE), bf16),
                        pltpu.VMEM((S // TK, rows, TK), f32),
                        pltpu.VMEM((rows, LANE), f32),
                        pltpu.VMEM((rows, LANE), f32)],
        compiler_params=pltpu.CompilerParams(
            dimension_semantics=("arbitrary", "arbitrary", "arbitrary"),
            vmem_limit_bytes=VMEM_LIMIT),
        name="attention",
    )(q, kcmp, vcmp, ks, vs, kw, vw, ga, mt, wb)


def _out_mlp_kernel(x_ref, ao_ref, pm_ref, gm_ref, wba_ref, wbp_ref, wo_ref, gn_ref,
                    w1_ref, w2_ref, gf_ref, o_ref):
    a = _dot(ao_ref[...], wba_ref[...])
    b = _dot(pm_ref[...], wbp_ref[...])
    gm = gm_ref[...].astype(f32)
    merged = (gm[:, :D_MODEL] * a + gm[:, D_MODEL:] * b).astype(bf16)
    x1 = x_ref[...] + _dot(merged, wo_ref[...])
    h = _rms(x1, gn_ref[...]).astype(bf16)
    f = jnp.square(jnp.maximum(_dot(h, w1_ref[...]), 0.0)).astype(bf16)
    x2 = x1 + _dot(f, w2_ref[...])
    o_ref[...] = _rms(x2, gf_ref[...])


def _out_mlp(x2d, ao, pm, gm, wba, wbp, wo, gn, w1, w2, gf):
    T, D = x2d.shape
    tok = lambda width: pl.BlockSpec((TS_OUT, width), lambda t: (t, 0))
    const = lambda arr: pl.BlockSpec(arr.shape, lambda t: (0, 0), pipeline_mode=pl.Buffered(1))
    return pl.pallas_call(
        _out_mlp_kernel,
        grid=(T // TS_OUT,),
        in_specs=[tok(D), tok(ATTN_WIDTH), tok(POOL_WIDTH), tok(2 * D_MODEL),
                  const(wba), const(wbp), const(wo), const(gn), const(w1), const(w2), const(gf)],
        out_specs=tok(D),
        out_shape=jax.ShapeDtypeStruct((T, D), f32),
        compiler_params=pltpu.CompilerParams(
            dimension_semantics=("arbitrary",), vmem_limit_bytes=VMEM_LIMIT),
        name="out_mlp",
    )(x2d, ao, pm, gm, wba, wbp, wo, gn, w1, w2, gf)


def _rope_tables(pos):
    inv = ROPE_THETA ** (-jnp.arange(0, ROT_DIM, 2, dtype=f32) / ROT_DIM)
    ang = pos.astype(f32)[:, None] * inv
    cos, sin = jnp.cos(ang), jnp.sin(ang)
    P = pos.shape[0]
    ones = jnp.ones((P, LANE - ROT_DIM), f32)
    zeros = jnp.zeros((P, LANE - ROT_HALF), f32)
    c = jnp.concatenate([cos, cos, ones], axis=1)
    sa = jnp.concatenate([-sin, zeros], axis=1)
    sb = jnp.concatenate([jnp.zeros((P, ROT_HALF), f32), sin,
                          jnp.zeros((P, LANE - ROT_DIM), f32)], axis=1)
    return c, sa, sb


def _cmp_to_sel_t(S):
    n_cmp = (S - CMP_BLOCK) // CMP_STRIDE + 1
    n_sel = S // SEL_BLOCK
    cs = np.arange(n_cmp)[:, None] * CMP_STRIDE
    js = np.arange(n_sel)[None, :] * SEL_BLOCK
    ov = np.clip(np.minimum(cs + CMP_BLOCK, js + SEL_BLOCK) - np.maximum(cs, js), 0, None)
    m = np.zeros((S // CMP_STRIDE, n_sel), np.float32)
    m[:n_cmp] = ov / CMP_STRIDE
    return jnp.asarray(m.T, dtype=bf16), n_cmp


def _window_bias():
    band = WINDOW + TQ
    r = np.arange(TQ)[:, None]
    c = np.arange(band)[None, :]
    cases = [c <= r + case * TQ for case in range(WINDOW // TQ)]
    cases.append((c > r) & (c <= r + WINDOW))
    return jnp.asarray(np.where(np.stack(cases), 0.0, -BIG), dtype=f32)


def _layout_w_in(w):
    D = w.shape[0]
    o = 0
    wq = w[:, o:o + ATTN_WIDTH]; o += ATTN_WIDTH
    kcw, vcw, ksw, vsw, kww, vww = [w[:, o + j * KV_WIDTH:o + (j + 1) * KV_WIDTH] for j in range(6)]
    o += 6 * KV_WIDTH
    gaw = w[:, o:o + N_BRANCH * N_HEADS]; o += N_BRANCH * N_HEADS
    uw = w[:, o:o + POOL_WIDTH]; o += POOL_WIDTH
    gmw = w[:, o:]
    gaw = jnp.concatenate([gaw, jnp.zeros((D, LANE - N_BRANCH * N_HEADS), w.dtype)], axis=1)
    out = jnp.concatenate([wq, ksw, kww, vsw, vww, kcw, vcw, uw, gmw, gaw], axis=1)
    assert out.shape[1] == W_COLS
    return out.astype(bf16)


def _block_diag_kv(m):
    z = jnp.zeros_like(m)
    return jnp.concatenate([jnp.concatenate([m, z], axis=-1), jnp.concatenate([z, m], axis=-1)], axis=-2)


def kernel(x, norm_mix, w_in, cmp_pe_k, cmp_pe_v, cmp_k_w1, cmp_k_w2, cmp_v_w1, cmp_v_w2,
           w_branch_attn, pool_w, pool_scale, w_branch_pool, w_out, norm_mlp, w_ff1, w_ff2,
           norm_final):
    B, S, D = x.shape
    assert norm_mix.shape[0] == 1, "single-layer block: the final norm is fused into out_mlp"
    assert S % TS_IN == 0 and S % TQ == 0 and (B * S) % TS_OUT == 0
    assert TK % TQ == 0 and S % TK == 0 and WINDOW % TQ == 0 and N_KV == 2 and GROUP % 2 == 0
    n_chunk = S // CMP_STRIDE
    rc, rsa, rsb = _rope_tables(jnp.arange(S))
    crc, crsa, crsb = _rope_tables(jnp.arange(n_chunk) * CMP_STRIDE + CMP_BLOCK - 1)
    mt, n_cmp = _cmp_to_sel_t(S)

    (q, ks, kw, vs, vw, kc, vc, ga, pm, gm) = _in_proj(
        x, norm_mix[0][None, :], _layout_w_in(w_in[0]), rc, rsa, rsb,
        pool_w[0].astype(bf16), pool_scale[0][None, :])

    def first_layer(w1):
        return _block_diag_kv(w1.reshape(CMP_BLOCK, HEAD_DIM, CMP_HIDDEN)).astype(bf16)

    def second_layer(w2):
        return _block_diag_kv(jnp.concatenate([w2, jnp.zeros_like(w2)], axis=1)).astype(bf16)

    kcmp, vcmp = _compress(
        kc, vc, jnp.tile(cmp_pe_k[0], (1, N_KV)), jnp.tile(cmp_pe_v[0], (1, N_KV)),
        first_layer(cmp_k_w1[0]), second_layer(cmp_k_w2[0]),
        first_layer(cmp_v_w1[0]), second_layer(cmp_v_w2[0]), crc, crsa, crsb)

    ao = _attention(q, kcmp, vcmp, ks, vs, kw, vw, ga, mt, _window_bias(), n_cmp)

    y = _out_mlp(x.reshape(B * S, D), ao.reshape(B * S, ATTN_WIDTH), pm.reshape(B * S, POOL_WIDTH),
                 gm.reshape(B * S, 2 * D_MODEL), w_branch_attn[0].astype(bf16),
                 w_branch_pool[0].astype(bf16), w_out[0].astype(bf16), norm_mlp[0][None, :],
                 w_ff1[0].astype(bf16), w_ff2[0].astype(bf16), norm_final[None, :])
    return y.reshape(B, S, D)
```

```python
import functools
import math

import jax
import jax.numpy as jnp
import numpy as np
from jax import lax
from jax.experimental import pallas as pl
from jax.experimental.pallas import tpu as pltpu

f32 = jnp.float32
bf16 = jnp.bfloat16

D_MODEL = 1024
N_HEADS = 8
HEAD_DIM = 64
N_KV = 2
GROUP = N_HEADS // N_KV
ROT_DIM = HEAD_DIM // 4
ROT_HALF = ROT_DIM // 2
ROPE_THETA = 500000.0
CMP_BLOCK = 32
CMP_STRIDE = 16
CMP_RATIO = CMP_BLOCK // CMP_STRIDE
CMP_HIDDEN = 4 * HEAD_DIM
SEL_BLOCK = 64
SEL_SHIFT = 6
SEL_TOPK = 16
WINDOW = 512
N_BRANCH = 3
ATTN_WIDTH = N_HEADS * HEAD_DIM
KV_WIDTH = N_KV * HEAD_DIM
POOL_WIDTH = 512
POOL_WINDOWS = (2, 4, 8, 16)
POOL_GROUP = POOL_WIDTH // len(POOL_WINDOWS)
POOL_HALO = 16
D_FF = 4 * D_MODEL
EPS = 1e-6
NEG_INF = -1e30
FORCE_SCORE = 1e4
BIG = 2.0 ** 100
M_INIT = -3.0e38
Q_SCALE = HEAD_DIM ** -0.5 * math.log2(math.e)

LANE = 128
SUBLANE = 8
VMEM_LIMIT = 56 * 1024 * 1024

SEG_Q = (0, ATTN_WIDTH)
SEG_KS = (SEG_Q[1], SEG_Q[1] + KV_WIDTH)
SEG_KW = (SEG_KS[1], SEG_KS[1] + KV_WIDTH)
SEG_VS = (SEG_KW[1], SEG_KW[1] + KV_WIDTH)
SEG_VW = (SEG_VS[1], SEG_VS[1] + KV_WIDTH)
SEG_KC = (SEG_VW[1], SEG_VW[1] + KV_WIDTH)
SEG_VC = (SEG_KC[1], SEG_KC[1] + KV_WIDTH)
SEG_U = (SEG_VC[1], SEG_VC[1] + POOL_WIDTH)
SEG_GM = (SEG_U[1], SEG_U[1] + 2 * D_MODEL)
SEG_GA = (SEG_GM[1], SEG_GM[1] + LANE)
W_COLS = SEG_GA[1]

TS_IN = 512
TQ = 256
TK = 512
SWEEP_UNROLL = 4
TS_OUT = 512


def _dot(a, b):
    return jnp.dot(a, b, preferred_element_type=f32)


def _dot_nt(a, b):
    return lax.dot_general(a, b, (((1,), (1,)), ((), ())), preferred_element_type=f32)


def _swap_halves(x):
    return pltpu.roll(x, HEAD_DIM, axis=1)


def _rope_lane_tile(x, c, sa, sb):
    return (x * c + pltpu.roll(x, LANE - ROT_HALF, axis=1) * sa
            + pltpu.roll(x, ROT_HALF, axis=1) * sb)


def _rms(x, g):
    ms = jnp.mean(x * x, axis=-1, keepdims=True)
    return x * lax.rsqrt(ms + EPS) * g


def _lane_chunks(s):
    return [s[:, c * LANE:(c + 1) * LANE] for c in range(s.shape[1] // LANE)]


def _value_tiles(v_head, lo):
    return jnp.concatenate([jnp.where(lo, v_head, 1.0), jnp.where(lo, 1.0, _swap_halves(v_head))], axis=1)


def _pair_split(pv, pair, lo, tq):
    xa = pv[2 * pair * tq:(2 * pair + 1) * tq]
    xb = pv[(2 * pair + 1) * tq:(2 * pair + 2) * tq]
    num = jnp.where(lo, xa[:, :LANE], xb[:, LANE:])
    den = jnp.where(lo, xa[:, LANE:], xb[:, :LANE])
    return num, den


def _in_proj_kernel(x_ref, g_ref, w_ref, rc_ref, rsa_ref, rsb_ref, pw_ref, ps_ref,
                    q_ref, ks_ref, kw_ref, vs_ref, vw_ref, kc_ref, vc_ref, ga_ref, pm_ref, gm_ref,
                    ubuf):
    si = pl.program_id(1)
    t0 = si * TS_IN

    @pl.when(si == 0)
    def _():
        ubuf[0:POOL_HALO, :] = jnp.zeros((POOL_HALO, POOL_WIDTH), f32)

    h = _rms(x_ref[0], g_ref[...]).astype(bf16)

    def seg(s):
        return _dot(h, w_ref[:, s[0]:s[1]])

    u = seg(SEG_U)
    ubuf[POOL_HALO:, :] = u
    tpos1 = (t0 + 1 + lax.broadcasted_iota(jnp.int32, (TS_IN, 1), 0)).astype(f32)

    def pool_steps():
        for gi, w in enumerate(POOL_WINDOWS):
            sl = slice(gi * POOL_GROUP, (gi + 1) * POOL_GROUP)
            acc = u[:, sl]
            for k in range(1, w):
                acc = acc + ubuf[POOL_HALO - k:POOL_HALO - k + TS_IN, sl]
                yield
            cnt = jnp.minimum(tpos1, float(w))
            pooled = (acc / cnt - u[:, sl]).astype(bf16)
            mixed = _dot(pooled, pw_ref[gi]) * ps_ref[:, sl]
            pm_ref[0, :, sl] = mixed.astype(bf16)
            yield
        ubuf[0:POOL_HALO, :] = ubuf[TS_IN:TS_IN + POOL_HALO, :]

    pool = pool_steps()

    def pool_advance(n):
        for _ in range(n):
            next(pool, None)

    rc, rsa, rsb = rc_ref[...], rsa_ref[...], rsb_ref[...]
    row_t = t0 + lax.broadcasted_iota(jnp.int32, (TS_IN, LANE), 0)
    lane = lax.broadcasted_iota(jnp.int32, (TS_IN, LANE), 1)
    lo = lane < HEAD_DIM

    def head_tiles(x):
        return (x, _swap_halves(x))

    q = seg(SEG_Q)
    pool_advance(4)
    for m, chunk in enumerate(_lane_chunks(q)):
        for half, xh in enumerate(head_tiles(chunk)):
            hh = 2 * m + half
            rot = _rope_lane_tile(xh, rc, rsa, rsb) * Q_SCALE
            q_ref[0, :, hh * LANE:(hh + 1) * LANE] = jnp.where(lo, rot, 0.0).astype(bf16)

    onehot = jnp.where(lane - HEAD_DIM == (row_t >> SEL_SHIFT), BIG, 0.0).astype(f32)
    k_both = _lane_chunks(seg((SEG_KS[0], SEG_KW[1])))
    pool_advance(2)
    v_both = _lane_chunks(seg((SEG_VS[0], SEG_VW[1])))
    pool_advance(2)
    c_both = _lane_chunks(seg((SEG_KC[0], SEG_VC[1])))
    pool_advance(2)
    for gg, (ksh, kwh, vsh, vwh) in enumerate(zip(head_tiles(k_both[0]), head_tiles(k_both[1]),
                                                   head_tiles(v_both[0]), head_tiles(v_both[1]))):
        sl = slice(gg * LANE, (gg + 1) * LANE)
        ks_ref[0, :, sl] = jnp.where(lo, _rope_lane_tile(ksh, rc, rsa, rsb), onehot).astype(bf16)
        kw_ref[0, :, sl] = jnp.where(lo, _rope_lane_tile(kwh, rc, rsa, rsb), 0.0).astype(bf16)
        sl2 = slice(gg * 2 * LANE, (gg + 1) * 2 * LANE)
        vs_ref[0, :, sl2] = _value_tiles(vsh, lo).astype(bf16)
        vw_ref[0, :, sl2] = _value_tiles(vwh, lo).astype(bf16)

    kc_ref[0] = c_both[0]
    vc_ref[0] = c_both[1]
    ga_ref[0] = jax.nn.sigmoid(seg(SEG_GA))
    gm_cols = 2 * LANE * 2
    for c0 in range(SEG_GM[0], SEG_GM[1], gm_cols):
        part = seg((c0, c0 + gm_cols))
        pool_advance(5)
        gm_ref[0, :, c0 - SEG_GM[0]:c0 - SEG_GM[0] + gm_cols] = jax.nn.sigmoid(part).astype(bf16)
    for _ in pool:
        pass


def _in_proj(x, g, w, rc, rsa, rsb, pw, ps):
    B, S, D = x.shape
    grid = (B, S // TS_IN)
    tok = lambda width: pl.BlockSpec((1, TS_IN, width), lambda b, s: (b, s, 0))
    const = lambda shape: pl.BlockSpec(shape, lambda b, s: (0,) * len(shape),
                                       pipeline_mode=pl.Buffered(1))
    tab = pl.BlockSpec((TS_IN, LANE), lambda b, s: (s, 0))
    out_shapes = [
        jax.ShapeDtypeStruct((B, S, N_HEADS * LANE), bf16),
        jax.ShapeDtypeStruct((B, S, N_KV * LANE), bf16),
        jax.ShapeDtypeStruct((B, S, N_KV * LANE), bf16),
        jax.ShapeDtypeStruct((B, S, N_KV * 2 * LANE), bf16),
        jax.ShapeDtypeStruct((B, S, N_KV * 2 * LANE), bf16),
        jax.ShapeDtypeStruct((B, S, KV_WIDTH), f32),
        jax.ShapeDtypeStruct((B, S, KV_WIDTH), f32),
        jax.ShapeDtypeStruct((B, S, LANE), f32),
        jax.ShapeDtypeStruct((B, S, POOL_WIDTH), bf16),
        jax.ShapeDtypeStruct((B, S, 2 * D_MODEL), bf16),
    ]
    return pl.pallas_call(
        _in_proj_kernel,
        grid=grid,
        in_specs=[tok(D), const((1, D)), const((D, W_COLS)), tab, tab, tab,
                  const((len(POOL_WINDOWS), POOL_GROUP, POOL_GROUP)), const((1, POOL_WIDTH))],
        out_specs=[tok(s.shape[-1]) for s in out_shapes],
        out_shape=out_shapes,
        scratch_shapes=[pltpu.VMEM((POOL_HALO + TS_IN, POOL_WIDTH), f32)],
        compiler_params=pltpu.CompilerParams(
            dimension_semantics=("arbitrary", "arbitrary"), vmem_limit_bytes=VMEM_LIMIT),
        name="in_proj",
    )(x, g, w, rc, rsa, rsb, pw, ps)


def _compress_kernel(kc_ref, vc_ref, pek_ref, pev_ref, kw1_ref, kw2_ref, vw1_ref, vw2_ref,
                     rc_ref, rsa_ref, rsb_ref, kout_ref, vout_ref):
    n_chunk = kc_ref.shape[1] // CMP_STRIDE

    def hidden(src_ref, pe_ref, w1_ref):
        parts = []
        for part in range(CMP_RATIO):
            acc = None
            for tok in range(CMP_STRIDE):
                t = part * CMP_STRIDE + tok
                a = src_ref[0, pl.ds(tok, n_chunk, stride=CMP_STRIDE), :] + pe_ref[t:t + 1, :]
                d = _dot(a.astype(bf16), w1_ref[t])
                acc = d if acc is None else acc + d
            parts.append(acc)
        pre = parts[0] + pltpu.roll(parts[1], n_chunk - 1, axis=0)
        return jax.nn.gelu(pre, approximate=True).astype(bf16)

    lane = lax.broadcasted_iota(jnp.int32, (n_chunk, LANE), 1)
    lo = lane < HEAD_DIM
    kcmp = _dot(hidden(kc_ref, pek_ref, kw1_ref), kw2_ref[...])
    vcmp = _dot(hidden(vc_ref, pev_ref, vw1_ref), vw2_ref[...])
    for gg in range(N_KV):
        sl = slice(gg * LANE, (gg + 1) * LANE)
        kout_ref[0, gg] = _rope_lane_tile(kcmp[:, sl], rc_ref[...], rsa_ref[...], rsb_ref[...]).astype(bf16)
        vout_ref[0, gg] = _value_tiles(vcmp[:, sl], lo).astype(bf16)


def _compress(kc, vc, pek, pev, kw1, kw2, vw1, vw2, rc, rsa, rsb):
    B, S, W = kc.shape
    NC = S // CMP_STRIDE
    const = lambda shape: pl.BlockSpec(shape, lambda b: (0,) * len(shape))
    src = pl.BlockSpec((1, S, W), lambda b: (b, 0, 0))
    out = lambda width: pl.BlockSpec((1, N_KV, NC, width), lambda b: (b, 0, 0, 0))
    return pl.pallas_call(
        _compress_kernel,
        grid=(B,),
        in_specs=[src, src, const(pek.shape), const(pev.shape), const(kw1.shape), const(kw2.shape),
                  const(vw1.shape), const(vw2.shape), const(rc.shape), const(rsa.shape), const(rsb.shape)],
        out_specs=[out(LANE), out(2 * LANE)],
        out_shape=[jax.ShapeDtypeStruct((B, N_KV, NC, LANE), bf16),
                   jax.ShapeDtypeStruct((B, N_KV, NC, 2 * LANE), bf16)],
        compiler_params=pltpu.CompilerParams(
            dimension_semantics=("arbitrary",), vmem_limit_bytes=VMEM_LIMIT),
        name="compress",
    )(kc, vc, pek, pev, kw1, kw2, vw1, vw2, rc, rsa, rsb)


def _attn_kernel(q_ref, kc_ref, vc_ref, ks_ref, vs_ref, kw_ref, vw_ref, ga_ref, mt_ref, wb_ref, gsel_ref,
                 o_ref, qa_sc, s_sc, mx_sc, acc_sc, *, n_cmp, n_sel):
    g = pl.program_id(1)
    i = pl.program_id(2)
    t0 = i * TQ
    rows = GROUP * TQ
    ncp = kc_ref.shape[2]

    row_t = t0 + lax.broadcasted_iota(jnp.int32, (TQ, 1), 0)
    lane_t = t0 + lax.broadcasted_iota(jnp.int32, (1, TQ), 1)
    row4_t = t0 + (lax.broadcasted_iota(jnp.int32, (rows, 1), 0) & (TQ - 1))
    lo = lax.broadcasted_iota(jnp.int32, (TQ, LANE), 1) < HEAD_DIM
    lo4 = lax.broadcasted_iota(jnp.int32, (rows, LANE), 1) < HEAD_DIM

    q_heads = [q_ref[0, :, r * LANE:(r + 1) * LANE] for r in range(GROUP)]

    band = WINDOW + TQ
    w0 = pl.multiple_of(jnp.maximum(t0 - WINDOW, 0), TQ)
    wbias = wb_ref[jnp.minimum(i, WINDOW // TQ)]
    kwin = kw_ref[0, pl.ds(w0, band), :]
    vw = vw_ref[0, pl.ds(w0, band), :]
    kc = kc_ref[0, 0]
    vc = vc_ref[0, 0]
    cidx = lax.broadcasted_iota(jnp.int32, (1, ncp), 1)
    cvalid = ((cidx * CMP_STRIDE + (CMP_BLOCK - 1)) <= row_t) & (cidx < n_cmp)
    cbias = jnp.where(cvalid, 0.0, NEG_INF).astype(f32)
    any_valid = (row_t >= CMP_BLOCK - 1).astype(f32)
    head_rows = lambda a, r: a[r * TQ:(r + 1) * TQ]
    q_all = jnp.concatenate(q_heads, axis=0)
    s_cmp = _dot_nt(q_all, kc)
    s_win = _dot_nt(q_all, kwin)
    psum = jnp.zeros((TQ, ncp), f32)
    e_cmp = []
    inv_cmp = []
    for r in range(GROUP):
        s = head_rows(s_cmp, r) + cbias
        e = jnp.exp2(s - jnp.max(s, axis=-1, keepdims=True))
        inv = 1.0 / jnp.sum(e, axis=-1, keepdims=True)
        psum = psum + e * inv
        e_cmp.append(e.astype(bf16))
        inv_cmp.append(inv * any_valid)
    pv_cmp = _dot(jnp.concatenate(e_cmp, axis=0), vc)

    mt = mt_ref[...]
    p_hi = psum.astype(bf16)
    rem = psum - p_hi.astype(f32)
    p_mid = rem.astype(bf16)
    p_lo = (rem - p_mid.astype(f32)).astype(bf16)
    imp = _dot_nt(mt, p_hi) + _dot_nt(mt, p_mid) + _dot_nt(mt, p_lo)
    blk = lax.broadcasted_iota(jnp.int32, (n_sel, TQ), 0)
    cur = lane_t >> SEL_SHIFT
    imp = jnp.where(blk > cur, -FORCE_SCORE, imp)
    imp = jnp.where((blk == 0) | (blk == cur) | (blk == cur - 1), FORCE_SCORE, imp)

    sub = lax.broadcasted_iota(jnp.int32, (SUBLANE, TQ), 0)
    n_vrow = n_sel // SUBLANE
    vrows = [imp[v * SUBLANE:(v + 1) * SUBLANE, :] for v in range(n_vrow)]
    ahead = [jnp.zeros((SUBLANE, TQ), f32) for _ in range(n_vrow)]

    def rank_against(vj):
        for sj in range(SUBLANE):
            rival = jnp.broadcast_to(vrows[vj][sj:sj + 1, :], (SUBLANE, TQ))
            for v in range(n_vrow):
                if v > vj:
                    ahead[v] = ahead[v] + jnp.where(rival >= vrows[v], 1.0, 0.0)
                elif v < vj:
                    ahead[v] = ahead[v] + jnp.where(rival > vrows[v], 1.0, 0.0)
                else:
                    tie = jnp.where(rival == vrows[v], (sub > sj).astype(f32), 0.0)
                    ahead[v] = ahead[v] + jnp.where(rival > vrows[v], 1.0, tie)

    p_win = []
    for r in range(GROUP):
        s = head_rows(s_win, r) + wbias
        p_win.append(jnp.exp2((s - jnp.max(s, axis=-1, keepdims=True)).astype(bf16)))
        for vj in range(r * n_vrow // GROUP, (r + 1) * n_vrow // GROUP):
            rank_against(vj)
    pv_win = _dot(jnp.concatenate(p_win, axis=0), vw)
    sel_m1 = jnp.where(jnp.concatenate(ahead, axis=0) < float(min(SEL_TOPK, n_sel)), 0.0, -1.0)
    sel_pad = jnp.concatenate([jnp.zeros((LANE - n_sel, TQ), f32), sel_m1.astype(f32)], axis=0).T
    sel_pad = sel_pad.astype(bf16)
    qa_sc[...] = jnp.concatenate([qr + sel_pad for qr in q_heads], axis=0)

    n_tiles = (t0 + TQ + TK - 1) // TK
    mx_sc[...] = jnp.full((rows, LANE), M_INIT, f32)

    def score_tiles(kts, causal):
        chunks = []
        for kt in kts:
            k0 = pl.multiple_of(kt * TK, TK)
            s = _dot_nt(qa_sc[...], ks_ref[0, pl.ds(k0, TK), :])
            if causal:
                kpos = k0 + lax.broadcasted_iota(jnp.int32, (1, TK), 1)
                s = jnp.where(kpos <= row4_t, s, -BIG)
            s_sc[kt] = s
            chunks += _lane_chunks(s)
        mx_sc[...] = jnp.maximum(mx_sc[...], functools.reduce(jnp.maximum, chunks))

    def weight_tiles(kts):
        mb = mx_sc[...]
        total = None
        for kt in kts:
            k0 = pl.multiple_of(kt * TK, TK)
            p = jnp.concatenate([jnp.exp2((c - mb).astype(bf16)) for c in _lane_chunks(s_sc[kt])], axis=1)
            d = _dot(p, vs_ref[0, pl.ds(k0, TK), :])
            total = d if total is None else total + d
        acc_sc[...] += total

    def sweep(n, tiles_fn):
        def trip(j, carry):
            tiles_fn([SWEEP_UNROLL * j + u for u in range(SWEEP_UNROLL)])
            return carry

        lax.fori_loop(0, n // SWEEP_UNROLL, trip, 0)
        group = SWEEP_UNROLL // 2
        while group >= 1:
            start = (n // (2 * group)) * (2 * group)

            @pl.when((n // group) % 2 == 1)
            def _(start=start, group=group):
                tiles_fn([start + u for u in range(group)])
            group //= 2

    sweep(n_tiles - 1, lambda kts: score_tiles(kts, False))
    score_tiles([n_tiles - 1], True)

    mx_sc[...] = jnp.broadcast_to(jnp.max(mx_sc[...], axis=-1, keepdims=True), (rows, LANE))
    acc_sc[...] = jnp.zeros((rows, 2 * LANE), f32)
    sweep(n_tiles, weight_tiles)
    pv_sel = acc_sc[...]

    gates = ga_ref[0]
    g_hi = gates.astype(bf16)
    g_lo = (gates - g_hi.astype(f32)).astype(bf16)
    spread = _dot(g_hi, gsel_ref[g]) + _dot(g_lo, gsel_ref[g])
    for pair in range(GROUP // 2):
        inv_c = jnp.where(lo, inv_cmp[2 * pair], inv_cmp[2 * pair + 1])
        total = None
        for br, pv in enumerate((pv_cmp, pv_sel, pv_win)):
            num, den = _pair_split(pv, pair, lo, TQ)
            scale = inv_c if br == 0 else 1.0 / den
            t = (br * (GROUP // 2) + pair) * LANE
            term = spread[:, t:t + LANE] * (num * scale)
            total = term if total is None else total + term
        o_ref[0, :, pair * LANE:(pair + 1) * LANE] = total.astype(bf16)


def _attention(q, kcmp, vcmp, ks, vs, kw, vw, ga, mt, wb, gsel, n_cmp):
    B, S, _ = q.shape
    NC = kcmp.shape[2]
    n_sel = S // SEL_BLOCK
    rows = GROUP * TQ
    grid = (B, N_KV, S // TQ)
    per_g = lambda width: pl.BlockSpec((1, S, width), lambda b, g, i: (b, 0, g))
    cmp_g = lambda width: pl.BlockSpec((1, 1, NC, width), lambda b, g, i: (b, g, 0, 0))
    const = lambda a: pl.BlockSpec(a.shape, lambda b, g, i: (0,) * a.ndim)
    return pl.pallas_call(
        functools.partial(_attn_kernel, n_cmp=n_cmp, n_sel=n_sel),
        grid=grid,
        in_specs=[
            pl.BlockSpec((1, TQ, GROUP * LANE), lambda b, g, i: (b, i, g)),
            cmp_g(LANE), cmp_g(2 * LANE),
            per_g(LANE), per_g(2 * LANE), per_g(LANE), per_g(2 * LANE),
            pl.BlockSpec((1, TQ, LANE), lambda b, g, i: (b, i, 0)),
            const(mt), const(wb), const(gsel),
        ],
        out_specs=pl.BlockSpec((1, TQ, GROUP * HEAD_DIM), lambda b, g, i: (b, i, g)),
        out_shape=jax.ShapeDtypeStruct((B, S, ATTN_WIDTH), bf16),
        scratch_shapes=[pltpu.VMEM((rows, LANE), bf16),
                        pltpu.VMEM((S // TK, rows, TK), f32),
                        pltpu.VMEM((rows, LANE), f32),
                        pltpu.VMEM((rows, 2 * LANE), f32)],
        compiler_params=pltpu.CompilerParams(
            dimension_semantics=("arbitrary", "arbitrary", "arbitrary"),
            vmem_limit_bytes=VMEM_LIMIT),
        name="attention",
    )(q, kcmp, vcmp, ks, vs, kw, vw, ga, mt, wb, gsel)


def _out_mlp_kernel(x_ref, ao_ref, pm_ref, gm_ref, wba_ref, wbp_ref, wo_ref, gn_ref,
                    w1_ref, w2_ref, gf_ref, o_ref):
    a = _dot(ao_ref[...], wba_ref[...])
    b = _dot(pm_ref[...], wbp_ref[...])
    gm = gm_ref[...].astype(f32)
    merged = (gm[:, :D_MODEL] * a + gm[:, D_MODEL:] * b).astype(bf16)
    x1 = x_ref[...] + _dot(merged, wo_ref[...])
    h = _rms(x1, gn_ref[...]).astype(bf16)
    f = jnp.square(jnp.maximum(_dot(h, w1_ref[...]), 0.0)).astype(bf16)
    x2 = x1 + _dot(f, w2_ref[...])
    o_ref[...] = _rms(x2, gf_ref[...])


def _out_mlp(x2d, ao, pm, gm, wba, wbp, wo, gn, w1, w2, gf):
    T, D = x2d.shape
    tok = lambda width: pl.BlockSpec((TS_OUT, width), lambda t: (t, 0))
    const = lambda arr: pl.BlockSpec(arr.shape, lambda t: (0, 0), pipeline_mode=pl.Buffered(1))
    return pl.pallas_call(
        _out_mlp_kernel,
        grid=(T // TS_OUT,),
        in_specs=[tok(D), tok(ATTN_WIDTH), tok(POOL_WIDTH), tok(2 * D_MODEL),
                  const(wba), const(wbp), const(wo), const(gn), const(w1), const(w2), const(gf)],
        out_specs=tok(D),
        out_shape=jax.ShapeDtypeStruct((T, D), f32),
        compiler_params=pltpu.CompilerParams(
            dimension_semantics=("arbitrary",), vmem_limit_bytes=VMEM_LIMIT),
        name="out_mlp",
    )(x2d, ao, pm, gm, wba, wbp, wo, gn, w1, w2, gf)


def _rope_tables(pos):
    inv = ROPE_THETA ** (-jnp.arange(0, ROT_DIM, 2, dtype=f32) / ROT_DIM)
    ang = pos.astype(f32)[:, None] * inv
    cos, sin = jnp.cos(ang), jnp.sin(ang)
    P = pos.shape[0]
    ones = jnp.ones((P, LANE - ROT_DIM), f32)
    zeros = jnp.zeros((P, LANE - ROT_HALF), f32)
    c = jnp.concatenate([cos, cos, ones], axis=1)
    sa = jnp.concatenate([-sin, zeros], axis=1)
    sb = jnp.concatenate([jnp.zeros((P, ROT_HALF), f32), sin,
                          jnp.zeros((P, LANE - ROT_DIM), f32)], axis=1)
    return c, sa, sb


def _cmp_to_sel_t(S):
    n_cmp = (S - CMP_BLOCK) // CMP_STRIDE + 1
    n_sel = S // SEL_BLOCK
    cs = np.arange(n_cmp)[:, None] * CMP_STRIDE
    js = np.arange(n_sel)[None, :] * SEL_BLOCK
    ov = np.clip(np.minimum(cs + CMP_BLOCK, js + SEL_BLOCK) - np.maximum(cs, js), 0, None)
    m = np.zeros((S // CMP_STRIDE, n_sel), np.float32)
    m[:n_cmp] = ov / CMP_STRIDE
    return jnp.asarray(m.T, dtype=bf16), n_cmp


def _window_bias():
    band = WINDOW + TQ
    r = np.arange(TQ)[:, None]
    c = np.arange(band)[None, :]
    cases = [c <= r + case * TQ for case in range(WINDOW // TQ)]
    cases.append((c > r) & (c <= r + WINDOW))
    return jnp.asarray(np.where(np.stack(cases), 0.0, -BIG), dtype=f32)


def _gate_selector():
    pairs = GROUP // 2
    sel = np.zeros((N_KV, LANE, N_BRANCH * pairs * LANE), np.float32)
    for g in range(N_KV):
        for br in range(N_BRANCH):
            for pair in range(pairs):
                for half in range(2):
                    src = br * N_HEADS + g * GROUP + 2 * pair + half
                    c0 = (br * pairs + pair) * LANE + half * HEAD_DIM
                    sel[g, src, c0:c0 + HEAD_DIM] = 1.0
    return jnp.asarray(sel, dtype=bf16)


def _layout_w_in(w):
    D = w.shape[0]
    o = 0
    wq = w[:, o:o + ATTN_WIDTH]; o += ATTN_WIDTH
    kcw, vcw, ksw, vsw, kww, vww = [w[:, o + j * KV_WIDTH:o + (j + 1) * KV_WIDTH] for j in range(6)]
    o += 6 * KV_WIDTH
    gaw = w[:, o:o + N_BRANCH * N_HEADS]; o += N_BRANCH * N_HEADS
    uw = w[:, o:o + POOL_WIDTH]; o += POOL_WIDTH
    gmw = w[:, o:]
    gaw = jnp.concatenate([gaw, jnp.zeros((D, LANE - N_BRANCH * N_HEADS), w.dtype)], axis=1)
    out = jnp.concatenate([wq, ksw, kww, vsw, vww, kcw, vcw, uw, gmw, gaw], axis=1)
    assert out.shape[1] == W_COLS
    return out.astype(bf16)


def _block_diag_kv(m):
    z = jnp.zeros_like(m)
    return jnp.concatenate([jnp.concatenate([m, z], axis=-1), jnp.concatenate([z, m], axis=-1)], axis=-2)


def kernel(x, norm_mix, w_in, cmp_pe_k, cmp_pe_v, cmp_k_w1, cmp_k_w2, cmp_v_w1, cmp_v_w2,
           w_branch_attn, pool_w, pool_scale, w_branch_pool, w_out, norm_mlp, w_ff1, w_ff2,
           norm_final):
    B, S, D = x.shape
    assert norm_mix.shape[0] == 1, "single-layer block: the final norm is fused into out_mlp"
    assert S % TS_IN == 0 and S % TQ == 0 and (B * S) % TS_OUT == 0
    assert TK % TQ == 0 and S % TK == 0 and WINDOW % TQ == 0 and N_KV == 2 and GROUP % 2 == 0
    n_chunk = S // CMP_STRIDE
    rc, rsa, rsb = _rope_tables(jnp.arange(S))
    crc, crsa, crsb = _rope_tables(jnp.arange(n_chunk) * CMP_STRIDE + CMP_BLOCK - 1)
    mt, n_cmp = _cmp_to_sel_t(S)

    (q, ks, kw, vs, vw, kc, vc, ga, pm, gm) = _in_proj(
        x, norm_mix[0][None, :], _layout_w_in(w_in[0]), rc, rsa, rsb,
        pool_w[0].astype(bf16), pool_scale[0][None, :])

    def first_layer(w1):
        return _block_diag_kv(w1.reshape(CMP_BLOCK, HEAD_DIM, CMP_HIDDEN)).astype(bf16)

    def second_layer(w2):
        return _block_diag_kv(jnp.concatenate([w2, jnp.zeros_like(w2)], axis=1)).astype(bf16)

    kcmp, vcmp = _compress(
        kc, vc, jnp.tile(cmp_pe_k[0], (1, N_KV)), jnp.tile(cmp_pe_v[0], (1, N_KV)),
        first_layer(cmp_k_w1[0]), second_layer(cmp_k_w2[0]),
        first_layer(cmp_v_w1[0]), second_layer(cmp_v_w2[0]), crc, crsa, crsb)

    ao = _attention(q, kcmp, vcmp, ks, vs, kw, vw, ga, mt, _window_bias(), _gate_selector(), n_cmp)

    y = _out_mlp(x.reshape(B * S, D), ao.reshape(B * S, ATTN_WIDTH), pm.reshape(B * S, POOL_WIDTH),
                 gm.reshape(B * S, 2 * D_MODEL), w_branch_attn[0].astype(bf16),
                 w_branch_pool[0].astype(bf16), w_out[0].astype(bf16), norm_mlp[0][None, :],
                 w_ff1[0].astype(bf16), w_ff2[0].astype(bf16), norm_final[None, :])
    return y.reshape(B, S, D)
```

```python
import functools
import math

import jax
import jax.numpy as jnp
import numpy as np
from jax import lax
from jax.experimental import pallas as pl
from jax.experimental.pallas import tpu as pltpu

f32 = jnp.float32
bf16 = jnp.bfloat16

D_MODEL = 1024
N_HEADS = 8
HEAD_DIM = 64
N_KV = 2
GROUP = N_HEADS // N_KV
ROT_DIM = HEAD_DIM // 4
ROT_HALF = ROT_DIM // 2
ROPE_THETA = 500000.0
CMP_BLOCK = 32
CMP_STRIDE = 16
CMP_RATIO = CMP_BLOCK // CMP_STRIDE
CMP_HIDDEN = 4 * HEAD_DIM
SEL_BLOCK = 64
SEL_SHIFT = 6
SEL_TOPK = 16
WINDOW = 512
N_BRANCH = 3
ATTN_WIDTH = N_HEADS * HEAD_DIM
KV_WIDTH = N_KV * HEAD_DIM
POOL_WIDTH = 512
POOL_WINDOWS = (2, 4, 8, 16)
POOL_GROUP = POOL_WIDTH // len(POOL_WINDOWS)
POOL_HALO = 16
D_FF = 4 * D_MODEL
EPS = 1e-6
NEG_INF = -1e30
FORCE_SCORE = 1e4
BIG = 2.0 ** 100
M_INIT = -3.0e38
Q_SCALE = HEAD_DIM ** -0.5 * math.log2(math.e)

LANE = 128
SUBLANE = 8
VMEM_LIMIT = 56 * 1024 * 1024

SEG_Q = (0, ATTN_WIDTH)
SEG_KS = (SEG_Q[1], SEG_Q[1] + KV_WIDTH)
SEG_KW = (SEG_KS[1], SEG_KS[1] + KV_WIDTH)
SEG_VS = (SEG_KW[1], SEG_KW[1] + KV_WIDTH)
SEG_VW = (SEG_VS[1], SEG_VS[1] + KV_WIDTH)
SEG_KC = (SEG_VW[1], SEG_VW[1] + KV_WIDTH)
SEG_VC = (SEG_KC[1], SEG_KC[1] + KV_WIDTH)
SEG_U = (SEG_VC[1], SEG_VC[1] + POOL_WIDTH)
SEG_GM = (SEG_U[1], SEG_U[1] + 2 * D_MODEL)
SEG_GA = (SEG_GM[1], SEG_GM[1] + LANE)
W_COLS = SEG_GA[1]

TS_IN = 512
TQ = 256
TK = 512
SWEEP_UNROLL = 4
TS_OUT = 512


def _dot(a, b):
    return jnp.dot(a, b, preferred_element_type=f32)


def _swap_halves(x):
    return pltpu.roll(x, HEAD_DIM, axis=1)


def _rope_lane_tile(x, c, sa, sb):
    return (x * c + pltpu.roll(x, LANE - ROT_HALF, axis=1) * sa
            + pltpu.roll(x, ROT_HALF, axis=1) * sb)


def _rms(x, g):
    ms = jnp.mean(x * x, axis=-1, keepdims=True)
    return x * lax.rsqrt(ms + EPS) * g


def _lane_chunks(s):
    return [s[:, c * LANE:(c + 1) * LANE] for c in range(s.shape[1] // LANE)]


def _tree(fn, items):
    items = list(items)
    while len(items) > 1:
        items = [fn(items[j], items[j + 1]) if j + 1 < len(items) else items[j]
                 for j in range(0, len(items), 2)]
    return items[0]


def _in_proj_kernel(x_ref, g_ref, w_ref, rc_ref, rsa_ref, rsb_ref, pw_ref, ps_ref,
                    q_ref, ks_ref, kw_ref, vs_ref, vw_ref, kc_ref, vc_ref, ga_ref, pm_ref, gm_ref,
                    ubuf):
    si = pl.program_id(1)
    t0 = si * TS_IN

    @pl.when(si == 0)
    def _():
        ubuf[0:POOL_HALO, :] = jnp.zeros((POOL_HALO, POOL_WIDTH), f32)

    h = _rms(x_ref[0], g_ref[...]).astype(bf16)

    def seg(s):
        return _dot(h, w_ref[:, s[0]:s[1]])

    u = seg(SEG_U)
    ubuf[POOL_HALO:, :] = u
    tpos1 = (t0 + 1 + lax.broadcasted_iota(jnp.int32, (TS_IN, 1), 0)).astype(f32)

    def pool_steps():
        for gi, w in enumerate(POOL_WINDOWS):
            sl = slice(gi * POOL_GROUP, (gi + 1) * POOL_GROUP)
            acc = u[:, sl]
            for k in range(1, w):
                acc = acc + ubuf[POOL_HALO - k:POOL_HALO - k + TS_IN, sl]
                yield
            cnt = jnp.minimum(tpos1, float(w))
            pooled = (acc / cnt - u[:, sl]).astype(bf16)
            mixed = _dot(pooled, pw_ref[gi]) * ps_ref[:, sl]
            pm_ref[0, :, sl] = mixed.astype(bf16)
            yield
        ubuf[0:POOL_HALO, :] = ubuf[TS_IN:TS_IN + POOL_HALO, :]

    pool = pool_steps()

    def pool_advance(n):
        for _ in range(n):
            next(pool, None)

    rc, rsa, rsb = rc_ref[...], rsa_ref[...], rsb_ref[...]
    row_t = t0 + lax.broadcasted_iota(jnp.int32, (TS_IN, LANE), 0)
    lane = lax.broadcasted_iota(jnp.int32, (TS_IN, LANE), 1)
    lo = lane < HEAD_DIM

    def head_tiles(x):
        return (x, _swap_halves(x))

    q = seg(SEG_Q)
    pool_advance(4)
    for m, chunk in enumerate(_lane_chunks(q)):
        for half, xh in enumerate(head_tiles(chunk)):
            rot = jnp.where(lo, _rope_lane_tile(xh, rc, rsa, rsb) * Q_SCALE, 0.0)
            q_ref[0, 2 * m + half] = rot.T.astype(bf16)

    onehot = jnp.where(lane - HEAD_DIM == (row_t >> SEL_SHIFT), BIG, 0.0).astype(f32)
    k_both = _lane_chunks(seg((SEG_KS[0], SEG_KW[1])))
    pool_advance(2)
    v_both = _lane_chunks(seg((SEG_VS[0], SEG_VW[1])))
    pool_advance(2)
    c_both = _lane_chunks(seg((SEG_KC[0], SEG_VC[1])))
    pool_advance(2)
    for gg, (ksh, kwh, vsh, vwh) in enumerate(zip(head_tiles(k_both[0]), head_tiles(k_both[1]),
                                                   head_tiles(v_both[0]), head_tiles(v_both[1]))):
        sl = slice(gg * LANE, (gg + 1) * LANE)
        ks_ref[0, :, sl] = jnp.where(lo, _rope_lane_tile(ksh, rc, rsa, rsb), onehot).astype(bf16)
        kw_ref[0, :, sl] = jnp.where(lo, _rope_lane_tile(kwh, rc, rsa, rsb), 0.0).astype(bf16)
        vs_ref[0, gg, 0] = jnp.where(lo, vsh, 1.0).T.astype(bf16)
        vw_t = jnp.where(lo, vwh, 1.0).T.astype(bf16)
        for j in range(TS_IN // TQ):
            vw_ref[0, gg, j] = vw_t[:, j * TQ:(j + 1) * TQ]

    kc_ref[0] = c_both[0]
    vc_ref[0] = c_both[1]
    ga_ref[0] = jax.nn.sigmoid(seg(SEG_GA))
    gm_cols = 2 * LANE * 2
    for c0 in range(SEG_GM[0], SEG_GM[1], gm_cols):
        part = seg((c0, c0 + gm_cols))
        pool_advance(5)
        gm_ref[0, :, c0 - SEG_GM[0]:c0 - SEG_GM[0] + gm_cols] = jax.nn.sigmoid(part).astype(bf16)
    for _ in pool:
        pass


def _in_proj(x, g, w, rc, rsa, rsb, pw, ps):
    B, S, D = x.shape
    grid = (B, S // TS_IN)
    tok = lambda width: pl.BlockSpec((1, TS_IN, width), lambda b, s: (b, s, 0))
    const = lambda shape: pl.BlockSpec(shape, lambda b, s: (0,) * len(shape),
                                       pipeline_mode=pl.Buffered(1))
    tab = pl.BlockSpec((TS_IN, LANE), lambda b, s: (s, 0))
    out_shapes = [
        jax.ShapeDtypeStruct((B, N_HEADS, LANE, S), bf16),
        jax.ShapeDtypeStruct((B, S, N_KV * LANE), bf16),
        jax.ShapeDtypeStruct((B, S, N_KV * LANE), bf16),
        jax.ShapeDtypeStruct((B, N_KV, S // TK, LANE, TK), bf16),
        jax.ShapeDtypeStruct((B, N_KV, S // TQ, LANE, TQ), bf16),
        jax.ShapeDtypeStruct((B, S, KV_WIDTH), f32),
        jax.ShapeDtypeStruct((B, S, KV_WIDTH), f32),
        jax.ShapeDtypeStruct((B, S, LANE), f32),
        jax.ShapeDtypeStruct((B, S, POOL_WIDTH), bf16),
        jax.ShapeDtypeStruct((B, S, 2 * D_MODEL), bf16),
    ]
    out_specs = [
        pl.BlockSpec((1, N_HEADS, LANE, TS_IN), lambda b, s: (b, 0, 0, s)),
        tok(N_KV * LANE), tok(N_KV * LANE),
        pl.BlockSpec((1, N_KV, TS_IN // TK, LANE, TK), lambda b, s: (b, 0, s, 0, 0)),
        pl.BlockSpec((1, N_KV, TS_IN // TQ, LANE, TQ), lambda b, s: (b, 0, s, 0, 0)),
        tok(KV_WIDTH), tok(KV_WIDTH), tok(LANE), tok(POOL_WIDTH), tok(2 * D_MODEL),
    ]
    return pl.pallas_call(
        _in_proj_kernel,
        grid=grid,
        in_specs=[tok(D), const((1, D)), const((D, W_COLS)), tab, tab, tab,
                  const((len(POOL_WINDOWS), POOL_GROUP, POOL_GROUP)), const((1, POOL_WIDTH))],
        out_specs=out_specs,
        out_shape=out_shapes,
        scratch_shapes=[pltpu.VMEM((POOL_HALO + TS_IN, POOL_WIDTH), f32)],
        compiler_params=pltpu.CompilerParams(
            dimension_semantics=("arbitrary", "arbitrary"), vmem_limit_bytes=VMEM_LIMIT),
        name="in_proj",
    )(x, g, w, rc, rsa, rsb, pw, ps)


def _compress_kernel(kc_ref, vc_ref, pek_ref, pev_ref, kw1_ref, kw2_ref, vw1_ref, vw2_ref,
                     rc_ref, rsa_ref, rsb_ref, kout_ref, vout_ref):
    n_chunk = kc_ref.shape[1] // CMP_STRIDE

    def hidden(src_ref, pe_ref, w1_ref):
        parts = []
        for part in range(CMP_RATIO):
            acc = None
            for tok in range(CMP_STRIDE):
                t = part * CMP_STRIDE + tok
                a = src_ref[0, pl.ds(tok, n_chunk, stride=CMP_STRIDE), :] + pe_ref[t:t + 1, :]
                d = _dot(a.astype(bf16), w1_ref[t])
                acc = d if acc is None else acc + d
            parts.append(acc)
        pre = parts[0] + pltpu.roll(parts[1], n_chunk - 1, axis=0)
        return jax.nn.gelu(pre, approximate=True).astype(bf16)

    lane = lax.broadcasted_iota(jnp.int32, (n_chunk, LANE), 1)
    lo = lane < HEAD_DIM
    kcmp = _dot(hidden(kc_ref, pek_ref, kw1_ref), kw2_ref[...])
    vcmp = _dot(hidden(vc_ref, pev_ref, vw1_ref), vw2_ref[...])
    for gg in range(N_KV):
        sl = slice(gg * LANE, (gg + 1) * LANE)
        kout_ref[0, gg] = _rope_lane_tile(kcmp[:, sl], rc_ref[...], rsa_ref[...], rsb_ref[...]).astype(bf16)
        vout_ref[0, gg] = jnp.where(lo, vcmp[:, sl], 1.0).T.astype(bf16)


def _compress(kc, vc, pek, pev, kw1, kw2, vw1, vw2, rc, rsa, rsb):
    B, S, W = kc.shape
    NC = S // CMP_STRIDE
    const = lambda shape: pl.BlockSpec(shape, lambda b: (0,) * len(shape))
    src = pl.BlockSpec((1, S, W), lambda b: (b, 0, 0))
    return pl.pallas_call(
        _compress_kernel,
        grid=(B,),
        in_specs=[src, src, const(pek.shape), const(pev.shape), const(kw1.shape), const(kw2.shape),
                  const(vw1.shape), const(vw2.shape), const(rc.shape), const(rsa.shape), const(rsb.shape)],
        out_specs=[pl.BlockSpec((1, N_KV, NC, LANE), lambda b: (b, 0, 0, 0)),
                   pl.BlockSpec((1, N_KV, LANE, NC), lambda b: (b, 0, 0, 0))],
        out_shape=[jax.ShapeDtypeStruct((B, N_KV, NC, LANE), bf16),
                   jax.ShapeDtypeStruct((B, N_KV, LANE, NC), bf16)],
        compiler_params=pltpu.CompilerParams(
            dimension_semantics=("arbitrary",), vmem_limit_bytes=VMEM_LIMIT),
        name="compress",
    )(kc, vc, pek, pev, kw1, kw2, vw1, vw2, rc, rsa, rsb)


def _attn_kernel(q_ref, kc_ref, vc_ref, ks_ref, vs_ref, kw_ref, vw_ref, ga_ref, mt_ref, wb_ref, o_ref,
                 qa_sc, s_sc, mx_sc, acc_sc, *, n_cmp, n_sel):
    g = pl.program_id(1)
    i = pl.program_id(2)
    t0 = i * TQ
    cols = GROUP * TQ
    ncp = kc_ref.shape[2]

    tok = t0 + lax.broadcasted_iota(jnp.int32, (1, TQ), 1)
    tok4 = t0 + (lax.broadcasted_iota(jnp.int32, (1, cols), 1) & (TQ - 1))
    head_cols = lambda a, r: a[:, r * TQ:(r + 1) * TQ]

    q_all = jnp.concatenate([q_ref[0, r] for r in range(GROUP)], axis=1)

    band = WINDOW + TQ
    w0 = pl.multiple_of(jnp.maximum(t0 - WINDOW, 0), TQ)
    wbias = wb_ref[jnp.minimum(i, WINDOW // TQ)]
    kwin = kw_ref[0, pl.ds(w0, band), :]
    vwin = jnp.concatenate([vw_ref[0, 0, w0 // TQ + j] for j in range(band // TQ)], axis=1)
    kc = kc_ref[0, 0]
    vc = vc_ref[0, 0]
    cpos = lax.broadcasted_iota(jnp.int32, (ncp, 1), 0)
    cvalid = ((cpos * CMP_STRIDE + (CMP_BLOCK - 1)) <= tok) & (cpos < n_cmp)
    cbias = jnp.where(cvalid, 0.0, NEG_INF).astype(f32)
    any_valid = (tok >= CMP_BLOCK - 1).astype(f32)
    s_cmp = _dot(kc, q_all)
    s_win = _dot(kwin, q_all)
    psum = jnp.zeros((ncp, TQ), f32)
    e_cmp = []
    inv_cmp = []
    for r in range(GROUP):
        s = head_cols(s_cmp, r) + cbias
        e = jnp.exp2(s - jnp.max(s, axis=0, keepdims=True))
        inv = 1.0 / jnp.sum(e, axis=0, keepdims=True)
        psum = psum + e * inv
        e_cmp.append(e.astype(bf16))
        inv_cmp.append(inv * any_valid)
    pv_cmp = _dot(vc, jnp.concatenate(e_cmp, axis=1))

    mt = mt_ref[...]
    p_hi = psum.astype(bf16)
    rem = psum - p_hi.astype(f32)
    p_mid = rem.astype(bf16)
    p_lo = (rem - p_mid.astype(f32)).astype(bf16)
    imp = _dot(mt, p_hi) + _dot(mt, p_mid) + _dot(mt, p_lo)
    blk = lax.broadcasted_iota(jnp.int32, (n_sel, TQ), 0)
    cur = tok >> SEL_SHIFT
    imp = jnp.where(blk > cur, -FORCE_SCORE, imp)
    imp = jnp.where((blk == 0) | (blk == cur) | (blk == cur - 1), FORCE_SCORE, imp)

    sub = lax.broadcasted_iota(jnp.int32, (SUBLANE, TQ), 0)
    n_vrow = n_sel // SUBLANE
    vrows = [imp[v * SUBLANE:(v + 1) * SUBLANE, :] for v in range(n_vrow)]
    ahead = [jnp.zeros((SUBLANE, TQ), f32) for _ in range(n_vrow)]

    def rank_against(vj):
        for sj in range(SUBLANE):
            rival = jnp.broadcast_to(vrows[vj][sj:sj + 1, :], (SUBLANE, TQ))
            for v in range(n_vrow):
                if v > vj:
                    ahead[v] = ahead[v] + jnp.where(rival >= vrows[v], 1.0, 0.0)
                elif v < vj:
                    ahead[v] = ahead[v] + jnp.where(rival > vrows[v], 1.0, 0.0)
                else:
                    tie = jnp.where(rival == vrows[v], (sub > sj).astype(f32), 0.0)
                    ahead[v] = ahead[v] + jnp.where(rival > vrows[v], 1.0, tie)

    p_win = []
    for r in range(GROUP):
        s = head_cols(s_win, r) + wbias
        p_win.append(jnp.exp2((s - jnp.max(s, axis=0, keepdims=True)).astype(bf16)))
        for vj in range(r * n_vrow // GROUP, (r + 1) * n_vrow // GROUP):
            rank_against(vj)
    pv_win = _dot(vwin, jnp.concatenate(p_win, axis=1))
    sel_m1 = jnp.where(jnp.concatenate(ahead, axis=0) < float(min(SEL_TOPK, n_sel)), 0.0, -1.0)
    helper = jnp.concatenate([jnp.zeros((LANE - n_sel, TQ), f32), sel_m1.astype(f32)], axis=0).astype(bf16)
    qa_sc[...] = q_all + jnp.concatenate([helper] * GROUP, axis=1)

    n_tiles = (t0 + TQ + TK - 1) // TK
    mx_sc[...] = jnp.full((SUBLANE, cols), M_INIT, f32)

    def score_tiles(kts, causal):
        slabs = []
        for kt in kts:
            k0 = pl.multiple_of(kt * TK, TK)
            s = _dot(ks_ref[0, pl.ds(k0, TK), :], qa_sc[...])
            if causal:
                kpos = k0 + lax.broadcasted_iota(jnp.int32, (TK, 1), 0)
                s = jnp.where(kpos <= tok4, s, -BIG)
            s_sc[kt] = s
            slabs += [s[j * SUBLANE:(j + 1) * SUBLANE] for j in range(TK // SUBLANE)]
        mx_sc[...] = jnp.maximum(mx_sc[...], _tree(jnp.maximum, slabs))

    def weight_tiles(kts):
        m = mx_sc[0:1, :]
        total = None
        for kt in kts:
            p = jnp.exp2((s_sc[kt] - m).astype(bf16))
            d = _dot(vs_ref[0, 0, kt], p)
            total = d if total is None else total + d
        acc_sc[...] += total

    def sweep(n, tiles_fn):
        def trip(j, carry):
            tiles_fn([SWEEP_UNROLL * j + u for u in range(SWEEP_UNROLL)])
            return carry

        lax.fori_loop(0, n // SWEEP_UNROLL, trip, 0)
        group = SWEEP_UNROLL // 2
        while group >= 1:
            start = (n // (2 * group)) * (2 * group)

            @pl.when((n // group) % 2 == 1)
            def _(start=start, group=group):
                tiles_fn([start + u for u in range(group)])
            group //= 2

    sweep(n_tiles - 1, lambda kts: score_tiles(kts, False))
    score_tiles([n_tiles - 1], True)

    mx_sc[...] = jnp.broadcast_to(jnp.max(mx_sc[...], axis=0, keepdims=True), (SUBLANE, cols))
    acc_sc[...] = jnp.zeros((LANE, cols), f32)
    sweep(n_tiles, weight_tiles)
    pv_sel = acc_sc[...]

    gates = ga_ref[0]
    gates = jnp.where(g == 0, gates, pltpu.roll(gates, LANE - GROUP, axis=1))
    gates_t = gates.T
    outs = []
    for r in range(GROUP):
        def branch(pv):
            x = head_cols(pv, r)
            return x[:HEAD_DIM], x[HEAD_DIM:HEAD_DIM + 1]

        num_c, _ = branch(pv_cmp)
        num_s, den_s = branch(pv_sel)
        num_w, den_w = branch(pv_win)
        gate = lambda br: gates_t[br * N_HEADS + r:br * N_HEADS + r + 1]
        outs.append((gate(0) * inv_cmp[r]) * num_c + (gate(1) / den_s) * num_s + (gate(2) / den_w) * num_w)
    o_ref[0] = jnp.concatenate(outs, axis=0).T.astype(bf16)


def _attention(q, kcmp, vcmp, ks, vs, kw, vw, ga, mt, wb, n_cmp):
    B, _, _, S = q.shape
    NC = kcmp.shape[2]
    n_sel = S // SEL_BLOCK
    cols = GROUP * TQ
    grid = (B, N_KV, S // TQ)
    keys_g = pl.BlockSpec((1, S, LANE), lambda b, g, i: (b, 0, g))
    const = lambda a: pl.BlockSpec(a.shape, lambda b, g, i: (0,) * a.ndim)
    return pl.pallas_call(
        functools.partial(_attn_kernel, n_cmp=n_cmp, n_sel=n_sel),
        grid=grid,
        in_specs=[
            pl.BlockSpec((1, GROUP, LANE, TQ), lambda b, g, i: (b, g, 0, i)),
            pl.BlockSpec((1, 1, NC, LANE), lambda b, g, i: (b, g, 0, 0)),
            pl.BlockSpec((1, 1, LANE, NC), lambda b, g, i: (b, g, 0, 0)),
            keys_g,
            pl.BlockSpec((1, 1, S // TK, LANE, TK), lambda b, g, i: (b, g, 0, 0, 0)),
            keys_g,
            pl.BlockSpec((1, 1, S // TQ, LANE, TQ), lambda b, g, i: (b, g, 0, 0, 0)),
            pl.BlockSpec((1, TQ, LANE), lambda b, g, i: (b, i, 0)),
            const(mt), const(wb),
        ],
        out_specs=pl.BlockSpec((1, TQ, GROUP * HEAD_DIM), lambda b, g, i: (b, i, g)),
        out_shape=jax.ShapeDtypeStruct((B, S, ATTN_WIDTH), bf16),
        scratch_shapes=[pltpu.VMEM((LANE, cols), bf16),
                        pltpu.VMEM((S // TK, TK, cols), f32),
                        pltpu.VMEM((SUBLANE, cols), f32),
                        pltpu.VMEM((LANE, cols), f32)],
        compiler_params=pltpu.CompilerParams(
            dimension_semantics=("arbitrary", "arbitrary", "arbitrary"),
            vmem_limit_bytes=VMEM_LIMIT),
        name="attention",
    )(q, kcmp, vcmp, ks, vs, kw, vw, ga, mt, wb)


def _out_mlp_kernel(x_ref, ao_ref, pm_ref, gm_ref, wba_ref, wbp_ref, wo_ref, gn_ref,
                    w1_ref, w2_ref, gf_ref, o_ref):
    a = _dot(ao_ref[...], wba_ref[...])
    b = _dot(pm_ref[...], wbp_ref[...])
    gm = gm_ref[...].astype(f32)
    merged = (gm[:, :D_MODEL] * a + gm[:, D_MODEL:] * b).astype(bf16)
    x1 = x_ref[...] + _dot(merged, wo_ref[...])
    h = _rms(x1, gn_ref[...]).astype(bf16)
    f = jnp.square(jnp.maximum(_dot(h, w1_ref[...]), 0.0)).astype(bf16)
    x2 = x1 + _dot(f, w2_ref[...])
    o_ref[...] = _rms(x2, gf_ref[...])


def _out_mlp(x2d, ao, pm, gm, wba, wbp, wo, gn, w1, w2, gf):
    T, D = x2d.shape
    tok = lambda width: pl.BlockSpec((TS_OUT, width), lambda t: (t, 0))
    const = lambda arr: pl.BlockSpec(arr.shape, lambda t: (0, 0), pipeline_mode=pl.Buffered(1))
    return pl.pallas_call(
        _out_mlp_kernel,
        grid=(T // TS_OUT,),
        in_specs=[tok(D), tok(ATTN_WIDTH), tok(POOL_WIDTH), tok(2 * D_MODEL),
                  const(wba), const(wbp), const(wo), const(gn), const(w1), const(w2), const(gf)],
        out_specs=tok(D),
        out_shape=jax.ShapeDtypeStruct((T, D), f32),
        compiler_params=pltpu.CompilerParams(
            dimension_semantics=("arbitrary",), vmem_limit_bytes=VMEM_LIMIT),
        name="out_mlp",
    )(x2d, ao, pm, gm, wba, wbp, wo, gn, w1, w2, gf)


def _rope_tables(pos):
    inv = ROPE_THETA ** (-jnp.arange(0, ROT_DIM, 2, dtype=f32) / ROT_DIM)
    ang = pos.astype(f32)[:, None] * inv
    cos, sin = jnp.cos(ang), jnp.sin(ang)
    P = pos.shape[0]
    ones = jnp.ones((P, LANE - ROT_DIM), f32)
    zeros = jnp.zeros((P, LANE - ROT_HALF), f32)
    c = jnp.concatenate([cos, cos, ones], axis=1)
    sa = jnp.concatenate([-sin, zeros], axis=1)
    sb = jnp.concatenate([jnp.zeros((P, ROT_HALF), f32), sin,
                          jnp.zeros((P, LANE - ROT_DIM), f32)], axis=1)
    return c, sa, sb


def _cmp_to_sel_t(S):
    n_cmp = (S - CMP_BLOCK) // CMP_STRIDE + 1
    n_sel = S // SEL_BLOCK
    cs = np.arange(n_cmp)[:, None] * CMP_STRIDE
    js = np.arange(n_sel)[None, :] * SEL_BLOCK
    ov = np.clip(np.minimum(cs + CMP_BLOCK, js + SEL_BLOCK) - np.maximum(cs, js), 0, None)
    m = np.zeros((S // CMP_STRIDE, n_sel), np.float32)
    m[:n_cmp] = ov / CMP_STRIDE
    return jnp.asarray(m.T, dtype=bf16), n_cmp


def _window_bias():
    band = WINDOW + TQ
    k = np.arange(band)[:, None]
    r = np.arange(TQ)[None, :]
    cases = [k <= r + case * TQ for case in range(WINDOW // TQ)]
    cases.append((k > r) & (k <= r + WINDOW))
    return jnp.asarray(np.where(np.stack(cases), 0.0, -BIG), dtype=f32)


def _layout_w_in(w):
    D = w.shape[0]
    w = w.astype(bf16)
    o = 0
    wq = w[:, o:o + ATTN_WIDTH]; o += ATTN_WIDTH
    kcw, vcw, ksw, vsw, kww, vww = [w[:, o + j * KV_WIDTH:o + (j + 1) * KV_WIDTH] for j in range(6)]
    o += 6 * KV_WIDTH
    gaw = w[:, o:o + N_BRANCH * N_HEADS]; o += N_BRANCH * N_HEADS
    uw = w[:, o:o + POOL_WIDTH]; o += POOL_WIDTH
    gmw = w[:, o:]
    gaw = jnp.concatenate([gaw, jnp.zeros((D, LANE - N_BRANCH * N_HEADS), w.dtype)], axis=1)
    out = jnp.concatenate([wq, ksw, kww, vsw, vww, kcw, vcw, uw, gmw, gaw], axis=1)
    assert out.shape[1] == W_COLS
    return out


def _block_diag_kv(m):
    z = jnp.zeros_like(m)
    return jnp.concatenate([jnp.concatenate([m, z], axis=-1), jnp.concatenate([z, m], axis=-1)], axis=-2)


def kernel(x, norm_mix, w_in, cmp_pe_k, cmp_pe_v, cmp_k_w1, cmp_k_w2, cmp_v_w1, cmp_v_w2,
           w_branch_attn, pool_w, pool_scale, w_branch_pool, w_out, norm_mlp, w_ff1, w_ff2,
           norm_final):
    B, S, D = x.shape
    assert norm_mix.shape[0] == 1, "single-layer block: the final norm is fused into out_mlp"
    assert S % TS_IN == 0 and S % TQ == 0 and (B * S) % TS_OUT == 0 and TS_IN == TK
    assert TK % TQ == 0 and S % TK == 0 and WINDOW % TQ == 0 and N_KV == 2
    assert SWEEP_UNROLL & (SWEEP_UNROLL - 1) == 0
    n_chunk = S // CMP_STRIDE
    rc, rsa, rsb = _rope_tables(jnp.arange(S))
    crc, crsa, crsb = _rope_tables(jnp.arange(n_chunk) * CMP_STRIDE + CMP_BLOCK - 1)
    mt, n_cmp = _cmp_to_sel_t(S)

    (q, ks, kw, vs, vw, kc, vc, ga, pm, gm) = _in_proj(
        x, norm_mix[0][None, :], _layout_w_in(w_in[0]), rc, rsa, rsb,
        pool_w[0].astype(bf16), pool_scale[0][None, :])

    def first_layer(w1):
        return _block_diag_kv(w1.reshape(CMP_BLOCK, HEAD_DIM, CMP_HIDDEN)).astype(bf16)

    def second_layer(w2):
        return _block_diag_kv(jnp.concatenate([w2, jnp.zeros_like(w2)], axis=1)).astype(bf16)

    kcmp, vcmp = _compress(
        kc, vc, jnp.tile(cmp_pe_k[0], (1, N_KV)), jnp.tile(cmp_pe_v[0], (1, N_KV)),
        first_layer(cmp_k_w1[0]), second_layer(cmp_k_w2[0]),
        first_layer(cmp_v_w1[0]), second_layer(cmp_v_w2[0]), crc, crsa, crsb)

    ao = _attention(q, kcmp, vcmp, ks, vs, kw, vw, ga, mt, _window_bias(), n_cmp)

    y = _out_mlp(x.reshape(B * S, D), ao.reshape(B * S, ATTN_WIDTH), pm.reshape(B * S, POOL_WIDTH),
                 gm.reshape(B * S, 2 * D_MODEL), w_branch_attn[0].astype(bf16),
                 w_branch_pool[0].astype(bf16), w_out[0].astype(bf16), norm_mlp[0][None, :],
                 w_ff1[0].astype(bf16), w_ff2[0].astype(bf16), norm_final[None, :])
    return y.reshape(B, S, D)
```

```python
import functools
import math

import jax
import jax.numpy as jnp
import numpy as np
from jax import lax
from jax.experimental import pallas as pl
from jax.experimental.pallas import tpu as pltpu

f32 = jnp.float32
bf16 = jnp.bfloat16

D_MODEL = 1024
N_HEADS = 8
HEAD_DIM = 64
N_KV = 2
GROUP = N_HEADS // N_KV
ROT_DIM = HEAD_DIM // 4
ROT_HALF = ROT_DIM // 2
ROPE_THETA = 500000.0
CMP_BLOCK = 32
CMP_STRIDE = 16
CMP_RATIO = CMP_BLOCK // CMP_STRIDE
CMP_HIDDEN = 4 * HEAD_DIM
SEL_BLOCK = 64
SEL_SHIFT = 6
SEL_TOPK = 16
WINDOW = 512
N_BRANCH = 3
ATTN_WIDTH = N_HEADS * HEAD_DIM
KV_WIDTH = N_KV * HEAD_DIM
POOL_WIDTH = 512
POOL_WINDOWS = (2, 4, 8, 16)
POOL_GROUP = POOL_WIDTH // len(POOL_WINDOWS)
POOL_HALO = 16
D_FF = 4 * D_MODEL
EPS = 1e-6
NEG_INF = -1e30
FORCE_SCORE = 1e4
BIG = 2.0 ** 100
M_INIT = -3.0e38
Q_SCALE = HEAD_DIM ** -0.5 * math.log2(math.e)

LANE = 128
SUBLANE = 8
VMEM_LIMIT = 56 * 1024 * 1024

SEG_Q = (0, ATTN_WIDTH)
SEG_KS = (SEG_Q[1], SEG_Q[1] + KV_WIDTH)
SEG_KW = (SEG_KS[1], SEG_KS[1] + KV_WIDTH)
SEG_VS = (SEG_KW[1], SEG_KW[1] + KV_WIDTH)
SEG_VW = (SEG_VS[1], SEG_VS[1] + KV_WIDTH)
SEG_KC = (SEG_VW[1], SEG_VW[1] + KV_WIDTH)
SEG_VC = (SEG_KC[1], SEG_KC[1] + KV_WIDTH)
SEG_U = (SEG_VC[1], SEG_VC[1] + POOL_WIDTH)
SEG_GM = (SEG_U[1], SEG_U[1] + 2 * D_MODEL)
SEG_GA = (SEG_GM[1], SEG_GM[1] + LANE)
W_COLS = SEG_GA[1]

TS_IN = 512
TQ = 256
TK = 512
SWEEP_UNROLL = 4
TS_OUT = 512


def _dot(a, b):
    return jnp.dot(a, b, preferred_element_type=f32)


def _swap_halves(x):
    return pltpu.roll(x, HEAD_DIM, axis=1)


def _rope_lane_tile(x, c, sa, sb):
    return (x * c + pltpu.roll(x, LANE - ROT_HALF, axis=1) * sa
            + pltpu.roll(x, ROT_HALF, axis=1) * sb)


def _rms(x, g):
    ms = jnp.mean(x * x, axis=-1, keepdims=True)
    return x * lax.rsqrt(ms + EPS) * g


def _lane_chunks(s):
    return [s[:, c * LANE:(c + 1) * LANE] for c in range(s.shape[1] // LANE)]


def _tree(fn, items):
    items = list(items)
    while len(items) > 1:
        items = [fn(items[j], items[j + 1]) if j + 1 < len(items) else items[j]
                 for j in range(0, len(items), 2)]
    return items[0]


def _in_proj_kernel(x_ref, g_ref, w_ref, rc_ref, rsa_ref, rsb_ref, pw_ref, ps_ref,
                    q_ref, ks_ref, kw_ref, vs_ref, vw_ref, kc_ref, vc_ref, ga_ref, pm_ref, gm_ref,
                    ubuf):
    si = pl.program_id(1)
    t0 = si * TS_IN

    @pl.when(si == 0)
    def _():
        ubuf[0:POOL_HALO, :] = jnp.zeros((POOL_HALO, POOL_WIDTH), f32)

    h = _rms(x_ref[0], g_ref[...]).astype(bf16)

    def seg(s):
        return _dot(h, w_ref[:, s[0]:s[1]])

    u = seg(SEG_U)
    ubuf[POOL_HALO:, :] = u
    tpos1 = (t0 + 1 + lax.broadcasted_iota(jnp.int32, (TS_IN, 1), 0)).astype(f32)

    def pool_steps():
        for gi, w in enumerate(POOL_WINDOWS):
            sl = slice(gi * POOL_GROUP, (gi + 1) * POOL_GROUP)
            acc = u[:, sl]
            for k in range(1, w):
                acc = acc + ubuf[POOL_HALO - k:POOL_HALO - k + TS_IN, sl]
                yield
            cnt = jnp.minimum(tpos1, float(w))
            pooled = (acc / cnt - u[:, sl]).astype(bf16)
            mixed = _dot(pooled, pw_ref[gi]) * ps_ref[:, sl]
            pm_ref[0, :, sl] = mixed.astype(bf16)
            yield
        ubuf[0:POOL_HALO, :] = ubuf[TS_IN:TS_IN + POOL_HALO, :]

    pool = pool_steps()

    def pool_advance(n):
        for _ in range(n):
            next(pool, None)

    rc, rsa, rsb = rc_ref[...], rsa_ref[...], rsb_ref[...]
    row_t = t0 + lax.broadcasted_iota(jnp.int32, (TS_IN, LANE), 0)
    lane = lax.broadcasted_iota(jnp.int32, (TS_IN, LANE), 1)
    lo = lane < HEAD_DIM

    def head_tiles(x):
        return (x, _swap_halves(x))

    q = seg(SEG_Q)
    pool_advance(4)
    for m, chunk in enumerate(_lane_chunks(q)):
        for half, xh in enumerate(head_tiles(chunk)):
            rot = jnp.where(lo, _rope_lane_tile(xh, rc, rsa, rsb) * Q_SCALE, 0.0)
            q_ref[0, 2 * m + half] = rot.T.astype(bf16)

    onehot = jnp.where(lane - HEAD_DIM == (row_t >> SEL_SHIFT), BIG, 0.0).astype(f32)
    k_both = _lane_chunks(seg((SEG_KS[0], SEG_KW[1])))
    pool_advance(2)
    v_both = _lane_chunks(seg((SEG_VS[0], SEG_VW[1])))
    pool_advance(2)
    c_both = _lane_chunks(seg((SEG_KC[0], SEG_VC[1])))
    pool_advance(2)
    for gg, (ksh, kwh, vsh, vwh) in enumerate(zip(head_tiles(k_both[0]), head_tiles(k_both[1]),
                                                   head_tiles(v_both[0]), head_tiles(v_both[1]))):
        sl = slice(gg * LANE, (gg + 1) * LANE)
        ks_ref[0, :, sl] = jnp.where(lo, _rope_lane_tile(ksh, rc, rsa, rsb), onehot).astype(bf16)
        kw_ref[0, :, sl] = jnp.where(lo, _rope_lane_tile(kwh, rc, rsa, rsb), 0.0).astype(bf16)
        vs_ref[0, gg, 0] = jnp.where(lo, vsh, 1.0).T.astype(bf16)
        vw_t = jnp.where(lo, vwh, 1.0).T.astype(bf16)
        for j in range(TS_IN // TQ):
            vw_ref[0, gg, j] = vw_t[:, j * TQ:(j + 1) * TQ]

    kc_ref[0] = c_both[0]
    vc_ref[0] = c_both[1]
    ga_ref[0] = jax.nn.sigmoid(seg(SEG_GA))
    gm_cols = 2 * LANE * 2
    for c0 in range(SEG_GM[0], SEG_GM[1], gm_cols):
        part = seg((c0, c0 + gm_cols))
        pool_advance(5)
        gm_ref[0, :, c0 - SEG_GM[0]:c0 - SEG_GM[0] + gm_cols] = jax.nn.sigmoid(part).astype(bf16)
    for _ in pool:
        pass


def _in_proj(x, g, w, rc, rsa, rsb, pw, ps):
    B, S, D = x.shape
    grid = (B, S // TS_IN)
    tok = lambda width: pl.BlockSpec((1, TS_IN, width), lambda b, s: (b, s, 0))
    const = lambda shape: pl.BlockSpec(shape, lambda b, s: (0,) * len(shape),
                                       pipeline_mode=pl.Buffered(1))
    tab = pl.BlockSpec((TS_IN, LANE), lambda b, s: (s, 0))
    out_shapes = [
        jax.ShapeDtypeStruct((B, N_HEADS, LANE, S), bf16),
        jax.ShapeDtypeStruct((B, S, N_KV * LANE), bf16),
        jax.ShapeDtypeStruct((B, S, N_KV * LANE), bf16),
        jax.ShapeDtypeStruct((B, N_KV, S // TK, LANE, TK), bf16),
        jax.ShapeDtypeStruct((B, N_KV, S // TQ, LANE, TQ), bf16),
        jax.ShapeDtypeStruct((B, S, KV_WIDTH), f32),
        jax.ShapeDtypeStruct((B, S, KV_WIDTH), f32),
        jax.ShapeDtypeStruct((B, S, LANE), f32),
        jax.ShapeDtypeStruct((B, S, POOL_WIDTH), bf16),
        jax.ShapeDtypeStruct((B, S, 2 * D_MODEL), bf16),
    ]
    out_specs = [
        pl.BlockSpec((1, N_HEADS, LANE, TS_IN), lambda b, s: (b, 0, 0, s)),
        tok(N_KV * LANE), tok(N_KV * LANE),
        pl.BlockSpec((1, N_KV, TS_IN // TK, LANE, TK), lambda b, s: (b, 0, s, 0, 0)),
        pl.BlockSpec((1, N_KV, TS_IN // TQ, LANE, TQ), lambda b, s: (b, 0, s, 0, 0)),
        tok(KV_WIDTH), tok(KV_WIDTH), tok(LANE), tok(POOL_WIDTH), tok(2 * D_MODEL),
    ]
    return pl.pallas_call(
        _in_proj_kernel,
        grid=grid,
        in_specs=[tok(D), const((1, D)), const((D, W_COLS)), tab, tab, tab,
                  const((len(POOL_WINDOWS), POOL_GROUP, POOL_GROUP)), const((1, POOL_WIDTH))],
        out_specs=out_specs,
        out_shape=out_shapes,
        scratch_shapes=[pltpu.VMEM((POOL_HALO + TS_IN, POOL_WIDTH), f32)],
        compiler_params=pltpu.CompilerParams(
            dimension_semantics=("arbitrary", "arbitrary"), vmem_limit_bytes=VMEM_LIMIT),
        name="in_proj",
    )(x, g, w, rc, rsa, rsb, pw, ps)


def _compress_kernel(kc_ref, vc_ref, pek_ref, pev_ref, kw1_ref, kw2_ref, vw1_ref, vw2_ref,
                     rc_ref, rsa_ref, rsb_ref, kout_ref, vout_ref):
    n_chunk = kc_ref.shape[1] // CMP_STRIDE

    def hidden(src_ref, pe_ref, w1_ref):
        parts = []
        for part in range(CMP_RATIO):
            acc = None
            for tok in range(CMP_STRIDE):
                t = part * CMP_STRIDE + tok
                a = src_ref[0, pl.ds(tok, n_chunk, stride=CMP_STRIDE), :] + pe_ref[t:t + 1, :]
                d = _dot(a.astype(bf16), w1_ref[t])
                acc = d if acc is None else acc + d
            parts.append(acc)
        pre = parts[0] + pltpu.roll(parts[1], n_chunk - 1, axis=0)
        return jax.nn.gelu(pre, approximate=True).astype(bf16)

    lane = lax.broadcasted_iota(jnp.int32, (n_chunk, LANE), 1)
    lo = lane < HEAD_DIM
    kcmp = _dot(hidden(kc_ref, pek_ref, kw1_ref), kw2_ref[...])
    vcmp = _dot(hidden(vc_ref, pev_ref, vw1_ref), vw2_ref[...])
    for gg in range(N_KV):
        sl = slice(gg * LANE, (gg + 1) * LANE)
        kout_ref[0, gg] = _rope_lane_tile(kcmp[:, sl], rc_ref[...], rsa_ref[...], rsb_ref[...]).astype(bf16)
        vout_ref[0, gg] = jnp.where(lo, vcmp[:, sl], 1.0).T.astype(bf16)


def _compress(kc, vc, pek, pev, kw1, kw2, vw1, vw2, rc, rsa, rsb):
    B, S, W = kc.shape
    NC = S // CMP_STRIDE
    const = lambda shape: pl.BlockSpec(shape, lambda b: (0,) * len(shape))
    src = pl.BlockSpec((1, S, W), lambda b: (b, 0, 0))
    return pl.pallas_call(
        _compress_kernel,
        grid=(B,),
        in_specs=[src, src, const(pek.shape), const(pev.shape), const(kw1.shape), const(kw2.shape),
                  const(vw1.shape), const(vw2.shape), const(rc.shape), const(rsa.shape), const(rsb.shape)],
        out_specs=[pl.BlockSpec((1, N_KV, NC, LANE), lambda b: (b, 0, 0, 0)),
                   pl.BlockSpec((1, N_KV, LANE, NC), lambda b: (b, 0, 0, 0))],
        out_shape=[jax.ShapeDtypeStruct((B, N_KV, NC, LANE), bf16),
                   jax.ShapeDtypeStruct((B, N_KV, LANE, NC), bf16)],
        compiler_params=pltpu.CompilerParams(
            dimension_semantics=("arbitrary",), vmem_limit_bytes=VMEM_LIMIT),
        name="compress",
    )(kc, vc, pek, pev, kw1, kw2, vw1, vw2, rc, rsa, rsb)


def _attn_kernel(q_ref, kc_ref, vc_ref, ks_ref, vs_ref, kw_ref, vw_ref, ga_ref, mt_ref, wb_ref, o_ref,
                 qa_sc, s_sc, mx_sc, acc_sc, imp_sc, sel_sc, *, n_cmp, n_sel):
    g = pl.program_id(1)
    i = pl.program_id(2)
    t0 = i * TQ
    cols = GROUP * TQ
    ncp = kc_ref.shape[2]

    tok = t0 + lax.broadcasted_iota(jnp.int32, (1, TQ), 1)
    tok4 = t0 + (lax.broadcasted_iota(jnp.int32, (1, cols), 1) & (TQ - 1))
    head_cols = lambda a, r: a[:, r * TQ:(r + 1) * TQ]

    q_all = jnp.concatenate([q_ref[0, r] for r in range(GROUP)], axis=1)

    kc = kc_ref[0, 0]
    vc = vc_ref[0, 0]
    cpos = lax.broadcasted_iota(jnp.int32, (ncp, 1), 0)
    cvalid = ((cpos * CMP_STRIDE + (CMP_BLOCK - 1)) <= tok) & (cpos < n_cmp)
    cbias = jnp.where(cvalid, 0.0, NEG_INF).astype(f32)
    any_valid = (tok >= CMP_BLOCK - 1).astype(f32)
    s_cmp = _dot(kc, q_all)
    band = WINDOW + TQ
    w0 = pl.multiple_of(jnp.maximum(t0 - WINDOW, 0), TQ)
    wbias = wb_ref[jnp.minimum(i, WINDOW // TQ)]
    kwin = kw_ref[0, pl.ds(w0, band), :]
    vwin = jnp.concatenate([vw_ref[0, 0, w0 // TQ + j] for j in range(band // TQ)], axis=1)
    s_win = _dot(kwin, q_all).astype(bf16)
    psum = jnp.zeros((ncp, TQ), f32)
    e_cmp = []
    inv_cmp = []
    for r in range(GROUP):
        s = head_cols(s_cmp, r) + cbias
        e = jnp.exp2(s - jnp.max(s, axis=0, keepdims=True))
        inv = 1.0 / jnp.sum(e, axis=0, keepdims=True)
        psum = psum + e * inv
        e_cmp.append(e.astype(bf16))
        inv_cmp.append(inv * any_valid)
    pv_cmp = _dot(vc, jnp.concatenate(e_cmp, axis=1))

    mt = mt_ref[...]
    p_hi = psum.astype(bf16)
    rem = psum - p_hi.astype(f32)
    p_mid = rem.astype(bf16)
    p_lo = (rem - p_mid.astype(f32)).astype(bf16)
    imp = _dot(mt, p_hi) + _dot(mt, p_mid) + _dot(mt, p_lo)
    blk = lax.broadcasted_iota(jnp.int32, (n_sel, TQ), 0)
    cur = tok >> SEL_SHIFT
    imp = jnp.where(blk > cur, -FORCE_SCORE, imp)
    imp = jnp.where((blk == 0) | (blk == cur) | (blk == cur - 1), FORCE_SCORE, imp)

    p_win = []
    for r in range(GROUP):
        s = head_cols(s_win, r) + wbias
        p_win.append(jnp.exp2(s - jnp.max(s, axis=0, keepdims=True)))
    pv_win = _dot(vwin, jnp.concatenate(p_win, axis=1))

    imp_sc[...] = imp
    sel_sc[...] = jnp.full((n_sel, TQ), -1.0, f32)
    sub = lax.broadcasted_iota(jnp.int32, (SUBLANE, TQ), 0)
    row_keys = SUBLANE * SEL_BLOCK
    rows_in_play = (t0 + TQ + row_keys - 1) // row_keys
    for n_rows in range(1, n_sel // SUBLANE + 1):
        @pl.when(rows_in_play == n_rows)
        def _(n_rows=n_rows):
            vrows = [imp_sc[v * SUBLANE:(v + 1) * SUBLANE, :] for v in range(n_rows)]
            ahead = [jnp.zeros((SUBLANE, TQ), f32) for _ in range(n_rows)]
            for vj in range(n_rows):
                for sj in range(SUBLANE):
                    rival = jnp.broadcast_to(vrows[vj][sj:sj + 1, :], (SUBLANE, TQ))
                    for v in range(n_rows):
                        if v > vj:
                            ahead[v] = ahead[v] + jnp.where(rival >= vrows[v], 1.0, 0.0)
                        elif v < vj:
                            ahead[v] = ahead[v] + jnp.where(rival > vrows[v], 1.0, 0.0)
                        else:
                            tie = jnp.where(rival == vrows[v], (sub > sj).astype(f32), 0.0)
                            ahead[v] = ahead[v] + jnp.where(rival > vrows[v], 1.0, tie)
            for v in range(n_rows):
                sel_sc[v * SUBLANE:(v + 1) * SUBLANE, :] = jnp.where(
                    ahead[v] < float(min(SEL_TOPK, n_sel)), 0.0, -1.0)

    helper = jnp.concatenate([jnp.zeros((LANE - n_sel, TQ), f32), sel_sc[...]], axis=0).astype(bf16)
    qa_sc[...] = q_all + jnp.concatenate([helper] * GROUP, axis=1)


    n_tiles = (t0 + TQ + TK - 1) // TK
    mx_sc[...] = jnp.full((SUBLANE, cols), M_INIT, f32)

    def score_tiles(kts, causal):
        slabs = []
        for kt in kts:
            k0 = pl.multiple_of(kt * TK, TK)
            s = _dot(ks_ref[0, pl.ds(k0, TK), :], qa_sc[...])
            if causal:
                kpos = k0 + lax.broadcasted_iota(jnp.int32, (TK, 1), 0)
                s = jnp.where(kpos <= tok4, s, -BIG)
            s_sc[kt] = s
            slabs += [s[j * SUBLANE:(j + 1) * SUBLANE] for j in range(TK // SUBLANE)]
        mx_sc[...] = jnp.maximum(mx_sc[...], _tree(jnp.maximum, slabs))

    def weight_tiles(kts):
        m = mx_sc[0:1, :]
        total = None
        for kt in kts:
            p = jnp.exp2((s_sc[kt] - m).astype(bf16))
            d = _dot(vs_ref[0, 0, kt], p)
            total = d if total is None else total + d
        acc_sc[...] += total

    def sweep(n, tiles_fn):
        def trip(j, carry):
            tiles_fn([SWEEP_UNROLL * j + u for u in range(SWEEP_UNROLL)])
            return carry

        lax.fori_loop(0, n // SWEEP_UNROLL, trip, 0)
        group = SWEEP_UNROLL // 2
        while group >= 1:
            start = (n // (2 * group)) * (2 * group)

            @pl.when((n // group) % 2 == 1)
            def _(start=start, group=group):
                tiles_fn([start + u for u in range(group)])
            group //= 2

    sweep(n_tiles - 1, lambda kts: score_tiles(kts, False))
    score_tiles([n_tiles - 1], True)

    mx_sc[...] = jnp.broadcast_to(jnp.max(mx_sc[...], axis=0, keepdims=True), (SUBLANE, cols))
    acc_sc[...] = jnp.zeros((LANE, cols), f32)
    sweep(n_tiles, weight_tiles)
    pv_sel = acc_sc[...]

    gates = ga_ref[0]
    gates = jnp.where(g == 0, gates, pltpu.roll(gates, LANE - GROUP, axis=1))
    gates_t = gates.T
    outs = []
    for r in range(GROUP):
        def branch(pv):
            x = head_cols(pv, r)
            return x[:HEAD_DIM], x[HEAD_DIM:HEAD_DIM + 1]

        num_c, _ = branch(pv_cmp)
        num_s, den_s = branch(pv_sel)
        num_w, den_w = branch(pv_win)
        gate = lambda br: gates_t[br * N_HEADS + r:br * N_HEADS + r + 1]
        outs.append((gate(0) * inv_cmp[r]) * num_c + (gate(1) / den_s) * num_s + (gate(2) / den_w) * num_w)
    o_ref[0] = jnp.concatenate(outs, axis=0).T.astype(bf16)


def _attention(q, kcmp, vcmp, ks, vs, kw, vw, ga, mt, wb, n_cmp):
    B, _, _, S = q.shape
    NC = kcmp.shape[2]
    n_sel = S // SEL_BLOCK
    cols = GROUP * TQ
    grid = (B, N_KV, S // TQ)
    keys_g = pl.BlockSpec((1, S, LANE), lambda b, g, i: (b, 0, g))
    const = lambda a: pl.BlockSpec(a.shape, lambda b, g, i: (0,) * a.ndim)
    return pl.pallas_call(
        functools.partial(_attn_kernel, n_cmp=n_cmp, n_sel=n_sel),
        grid=grid,
        in_specs=[
            pl.BlockSpec((1, GROUP, LANE, TQ), lambda b, g, i: (b, g, 0, i)),
            pl.BlockSpec((1, 1, NC, LANE), lambda b, g, i: (b, g, 0, 0)),
            pl.BlockSpec((1, 1, LANE, NC), lambda b, g, i: (b, g, 0, 0)),
            keys_g,
            pl.BlockSpec((1, 1, S // TK, LANE, TK), lambda b, g, i: (b, g, 0, 0, 0)),
            keys_g,
            pl.BlockSpec((1, 1, S // TQ, LANE, TQ), lambda b, g, i: (b, g, 0, 0, 0)),
            pl.BlockSpec((1, TQ, LANE), lambda b, g, i: (b, i, 0)),
            const(mt), const(wb),
        ],
        out_specs=pl.BlockSpec((1, TQ, GROUP * HEAD_DIM), lambda b, g, i: (b, i, g)),
        out_shape=jax.ShapeDtypeStruct((B, S, ATTN_WIDTH), bf16),
        scratch_shapes=[pltpu.VMEM((LANE, cols), bf16),
                        pltpu.VMEM((S // TK, TK, cols), f32),
                        pltpu.VMEM((SUBLANE, cols), f32),
                        pltpu.VMEM((LANE, cols), f32),
                        pltpu.VMEM((n_sel, TQ), f32),
                        pltpu.VMEM((n_sel, TQ), f32)],
        compiler_params=pltpu.CompilerParams(
            dimension_semantics=("arbitrary", "arbitrary", "arbitrary"),
            vmem_limit_bytes=VMEM_LIMIT),
        name="attention",
    )(q, kcmp, vcmp, ks, vs, kw, vw, ga, mt, wb)


def _out_mlp_kernel(x_ref, ao_ref, pm_ref, gm_ref, wba_ref, wbp_ref, wo_ref, gn_ref,
                    w1_ref, w2_ref, gf_ref, o_ref):
    a = _dot(ao_ref[...], wba_ref[...])
    b = _dot(pm_ref[...], wbp_ref[...])
    gm = gm_ref[...].astype(f32)
    merged = (gm[:, :D_MODEL] * a + gm[:, D_MODEL:] * b).astype(bf16)
    x1 = x_ref[...] + _dot(merged, wo_ref[...])
    h = _rms(x1, gn_ref[...]).astype(bf16)
    f = jnp.square(jnp.maximum(_dot(h, w1_ref[...]), 0.0)).astype(bf16)
    x2 = x1 + _dot(f, w2_ref[...])
    o_ref[...] = _rms(x2, gf_ref[...])


def _out_mlp(x2d, ao, pm, gm, wba, wbp, wo, gn, w1, w2, gf):
    T, D = x2d.shape
    tok = lambda width: pl.BlockSpec((TS_OUT, width), lambda t: (t, 0))
    const = lambda arr: pl.BlockSpec(arr.shape, lambda t: (0, 0), pipeline_mode=pl.Buffered(1))
    return pl.pallas_call(
        _out_mlp_kernel,
        grid=(T // TS_OUT,),
        in_specs=[tok(D), tok(ATTN_WIDTH), tok(POOL_WIDTH), tok(2 * D_MODEL),
                  const(wba), const(wbp), const(wo), const(gn), const(w1), const(w2), const(gf)],
        out_specs=tok(D),
        out_shape=jax.ShapeDtypeStruct((T, D), f32),
        compiler_params=pltpu.CompilerParams(
            dimension_semantics=("arbitrary",), vmem_limit_bytes=VMEM_LIMIT),
        name="out_mlp",
    )(x2d, ao, pm, gm, wba, wbp, wo, gn, w1, w2, gf)


def _rope_tables(pos):
    inv = ROPE_THETA ** (-jnp.arange(0, ROT_DIM, 2, dtype=f32) / ROT_DIM)
    ang = pos.astype(f32)[:, None] * inv
    cos, sin = jnp.cos(ang), jnp.sin(ang)
    P = pos.shape[0]
    ones = jnp.ones((P, LANE - ROT_DIM), f32)
    zeros = jnp.zeros((P, LANE - ROT_HALF), f32)
    c = jnp.concatenate([cos, cos, ones], axis=1)
    sa = jnp.concatenate([-sin, zeros], axis=1)
    sb = jnp.concatenate([jnp.zeros((P, ROT_HALF), f32), sin,
                          jnp.zeros((P, LANE - ROT_DIM), f32)], axis=1)
    return c, sa, sb


def _cmp_to_sel_t(S):
    n_cmp = (S - CMP_BLOCK) // CMP_STRIDE + 1
    n_sel = S // SEL_BLOCK
    cs = np.arange(n_cmp)[:, None] * CMP_STRIDE
    js = np.arange(n_sel)[None, :] * SEL_BLOCK
    ov = np.clip(np.minimum(cs + CMP_BLOCK, js + SEL_BLOCK) - np.maximum(cs, js), 0, None)
    m = np.zeros((S // CMP_STRIDE, n_sel), np.float32)
    m[:n_cmp] = ov / CMP_STRIDE
    return jnp.asarray(m.T, dtype=bf16), n_cmp


def _window_bias():
    band = WINDOW + TQ
    k = np.arange(band)[:, None]
    r = np.arange(TQ)[None, :]
    cases = [k <= r + case * TQ for case in range(WINDOW // TQ)]
    cases.append((k > r) & (k <= r + WINDOW))
    return jnp.asarray(np.where(np.stack(cases), 0.0, -BIG), dtype=bf16)


def _layout_w_in(w):
    D = w.shape[0]
    w = w.astype(bf16)
    o = 0
    wq = w[:, o:o + ATTN_WIDTH]; o += ATTN_WIDTH
    kcw, vcw, ksw, vsw, kww, vww = [w[:, o + j * KV_WIDTH:o + (j + 1) * KV_WIDTH] for j in range(6)]
    o += 6 * KV_WIDTH
    gaw = w[:, o:o + N_BRANCH * N_HEADS]; o += N_BRANCH * N_HEADS
    uw = w[:, o:o + POOL_WIDTH]; o += POOL_WIDTH
    gmw = w[:, o:]
    gaw = jnp.concatenate([gaw, jnp.zeros((D, LANE - N_BRANCH * N_HEADS), w.dtype)], axis=1)
    out = jnp.concatenate([wq, ksw, kww, vsw, vww, kcw, vcw, uw, gmw, gaw], axis=1)
    assert out.shape[1] == W_COLS
    return out


def _block_diag_kv(m):
    z = jnp.zeros_like(m)
    return jnp.concatenate([jnp.concatenate([m, z], axis=-1), jnp.concatenate([z, m], axis=-1)], axis=-2)


def kernel(x, norm_mix, w_in, cmp_pe_k, cmp_pe_v, cmp_k_w1, cmp_k_w2, cmp_v_w1, cmp_v_w2,
           w_branch_attn, pool_w, pool_scale, w_branch_pool, w_out, norm_mlp, w_ff1, w_ff2,
           norm_final):
    B, S, D = x.shape
    assert norm_mix.shape[0] == 1, "single-layer block: the final norm is fused into out_mlp"
    assert S % TS_IN == 0 and S % TQ == 0 and (B * S) % TS_OUT == 0 and TS_IN == TK
    assert TK % TQ == 0 and S % TK == 0 and WINDOW % TQ == 0 and N_KV == 2
    assert SWEEP_UNROLL & (SWEEP_UNROLL - 1) == 0
    n_chunk = S // CMP_STRIDE
    rc, rsa, rsb = _rope_tables(jnp.arange(S))
    crc, crsa, crsb = _rope_tables(jnp.arange(n_chunk) * CMP_STRIDE + CMP_BLOCK - 1)
    mt, n_cmp = _cmp_to_sel_t(S)

    (q, ks, kw, vs, vw, kc, vc, ga, pm, gm) = _in_proj(
        x, norm_mix[0][None, :], _layout_w_in(w_in[0]), rc, rsa, rsb,
        pool_w[0].astype(bf16), pool_scale[0][None, :])

    def first_layer(w1):
        return _block_diag_kv(w1.reshape(CMP_BLOCK, HEAD_DIM, CMP_HIDDEN)).astype(bf16)

    def second_layer(w2):
        return _block_diag_kv(jnp.concatenate([w2, jnp.zeros_like(w2)], axis=1)).astype(bf16)

    kcmp, vcmp = _compress(
        kc, vc, jnp.tile(cmp_pe_k[0], (1, N_KV)), jnp.tile(cmp_pe_v[0], (1, N_KV)),
        first_layer(cmp_k_w1[0]), second_layer(cmp_k_w2[0]),
        first_layer(cmp_v_w1[0]), second_layer(cmp_v_w2[0]), crc, crsa, crsb)

    ao = _attention(q, kcmp, vcmp, ks, vs, kw, vw, ga, mt, _window_bias(), n_cmp)

    y = _out_mlp(x.reshape(B * S, D), ao.reshape(B * S, ATTN_WIDTH), pm.reshape(B * S, POOL_WIDTH),
                 gm.reshape(B * S, 2 * D_MODEL), w_branch_attn[0].astype(bf16),
                 w_branch_pool[0].astype(bf16), w_out[0].astype(bf16), norm_mlp[0][None, :],
                 w_ff1[0].astype(bf16), w_ff2[0].astype(bf16), norm_final[None, :])
    return y.reshape(B, S, D)
```

```python
import functools
import math

import jax
import jax.numpy as jnp
import numpy as np
from jax import lax
from jax.experimental import pallas as pl
from jax.experimental.pallas import tpu as pltpu

f32 = jnp.float32
bf16 = jnp.bfloat16

D_MODEL = 1024
N_HEADS = 8
HEAD_DIM = 64
N_KV = 2
GROUP = N_HEADS // N_KV
ROT_DIM = HEAD_DIM // 4
ROT_HALF = ROT_DIM // 2
ROPE_THETA = 500000.0
CMP_BLOCK = 32
CMP_STRIDE = 16
CMP_RATIO = CMP_BLOCK // CMP_STRIDE
CMP_HIDDEN = 4 * HEAD_DIM
SEL_BLOCK = 64
SEL_SHIFT = 6
SEL_TOPK = 16
WINDOW = 512
N_BRANCH = 3
ATTN_WIDTH = N_HEADS * HEAD_DIM
KV_WIDTH = N_KV * HEAD_DIM
POOL_WIDTH = 512
POOL_WINDOWS = (2, 4, 8, 16)
POOL_GROUP = POOL_WIDTH // len(POOL_WINDOWS)
POOL_HALO = 16
D_FF = 4 * D_MODEL
EPS = 1e-6
NEG_INF = -1e30
FORCE_SCORE = 1e4
BIG = 2.0 ** 100
M_INIT = -3.0e38
Q_SCALE = HEAD_DIM ** -0.5 * math.log2(math.e)

LANE = 128
SUBLANE = 8
VMEM_LIMIT = 56 * 1024 * 1024

SEG_Q = (0, ATTN_WIDTH)
SEG_KS = (SEG_Q[1], SEG_Q[1] + KV_WIDTH)
SEG_KW = (SEG_KS[1], SEG_KS[1] + KV_WIDTH)
SEG_VS = (SEG_KW[1], SEG_KW[1] + KV_WIDTH)
SEG_VW = (SEG_VS[1], SEG_VS[1] + KV_WIDTH)
SEG_KC = (SEG_VW[1], SEG_VW[1] + KV_WIDTH)
SEG_VC = (SEG_KC[1], SEG_KC[1] + KV_WIDTH)
SEG_U = (SEG_VC[1], SEG_VC[1] + POOL_WIDTH)
SEG_GM = (SEG_U[1], SEG_U[1] + 2 * D_MODEL)
SEG_GA = (SEG_GM[1], SEG_GM[1] + LANE)
W_COLS = SEG_GA[1]

TS_IN = 512
TQ = 256
TK = 512
SWEEP_UNROLL = 4
TS_OUT = 512


def _dot(a, b):
    return jnp.dot(a, b, preferred_element_type=f32)


def _swap_halves(x):
    return pltpu.roll(x, HEAD_DIM, axis=1)


def _rope_lane_tile(x, c, sa, sb):
    return (x * c + pltpu.roll(x, LANE - ROT_HALF, axis=1) * sa
            + pltpu.roll(x, ROT_HALF, axis=1) * sb)


def _rms(x, g):
    ms = jnp.mean(x * x, axis=-1, keepdims=True)
    return x * lax.rsqrt(ms + EPS) * g


def _lane_chunks(s):
    return [s[:, c * LANE:(c + 1) * LANE] for c in range(s.shape[1] // LANE)]


def _tree(fn, items):
    items = list(items)
    while len(items) > 1:
        items = [fn(items[j], items[j + 1]) if j + 1 < len(items) else items[j]
                 for j in range(0, len(items), 2)]
    return items[0]


def _in_proj_kernel(x_ref, g_ref, w_ref, rc_ref, rsa_ref, rsb_ref, pw_ref, ps_ref,
                    q_ref, ks_ref, kw_ref, vs_ref, vw_ref, kc_ref, vc_ref, ga_ref, pm_ref, gm_ref,
                    ubuf):
    si = pl.program_id(1)
    t0 = si * TS_IN

    @pl.when(si == 0)
    def _():
        ubuf[0:POOL_HALO, :] = jnp.zeros((POOL_HALO, POOL_WIDTH), f32)

    h = _rms(x_ref[0], g_ref[...]).astype(bf16)

    def seg(s):
        return _dot(h, w_ref[:, s[0]:s[1]])

    u = seg(SEG_U)
    ubuf[POOL_HALO:, :] = u
    tpos1 = (t0 + 1 + lax.broadcasted_iota(jnp.int32, (TS_IN, 1), 0)).astype(f32)

    def pool_steps():
        for gi, w in enumerate(POOL_WINDOWS):
            sl = slice(gi * POOL_GROUP, (gi + 1) * POOL_GROUP)
            acc = u[:, sl]
            for k in range(1, w):
                acc = acc + ubuf[POOL_HALO - k:POOL_HALO - k + TS_IN, sl]
                yield
            cnt = jnp.minimum(tpos1, float(w))
            pooled = (acc / cnt - u[:, sl]).astype(bf16)
            mixed = _dot(pooled, pw_ref[gi]) * ps_ref[:, sl]
            pm_ref[0, :, sl] = mixed.astype(bf16)
            yield
        ubuf[0:POOL_HALO, :] = ubuf[TS_IN:TS_IN + POOL_HALO, :]

    pool = pool_steps()

    def pool_advance(n):
        for _ in range(n):
            next(pool, None)

    rc, rsa, rsb = rc_ref[...], rsa_ref[...], rsb_ref[...]
    row_t = t0 + lax.broadcasted_iota(jnp.int32, (TS_IN, LANE), 0)
    lane = lax.broadcasted_iota(jnp.int32, (TS_IN, LANE), 1)
    lo = lane < HEAD_DIM

    def head_tiles(x):
        return (x, _swap_halves(x))

    q = seg(SEG_Q)
    pool_advance(4)
    for m, chunk in enumerate(_lane_chunks(q)):
        for half, xh in enumerate(head_tiles(chunk)):
            rot = jnp.where(lo, _rope_lane_tile(xh, rc, rsa, rsb) * Q_SCALE, 0.0)
            q_ref[0, 2 * m + half] = rot.T.astype(bf16)

    onehot = jnp.where(lane - HEAD_DIM == (row_t >> SEL_SHIFT), BIG, 0.0).astype(f32)
    k_both = _lane_chunks(seg((SEG_KS[0], SEG_KW[1])))
    pool_advance(2)
    v_both = _lane_chunks(seg((SEG_VS[0], SEG_VW[1])))
    pool_advance(2)
    c_both = _lane_chunks(seg((SEG_KC[0], SEG_VC[1])))
    pool_advance(2)
    for gg, (ksh, kwh, vsh, vwh) in enumerate(zip(head_tiles(k_both[0]), head_tiles(k_both[1]),
                                                   head_tiles(v_both[0]), head_tiles(v_both[1]))):
        sl = slice(gg * LANE, (gg + 1) * LANE)
        ks_ref[0, :, sl] = jnp.where(lo, _rope_lane_tile(ksh, rc, rsa, rsb), onehot).astype(bf16)
        kw_ref[0, :, sl] = jnp.where(lo, _rope_lane_tile(kwh, rc, rsa, rsb), 0.0).astype(bf16)
        vs_ref[0, gg, 0] = jnp.where(lo, vsh, 1.0).T.astype(bf16)
        vw_t = jnp.where(lo, vwh, 1.0).T.astype(bf16)
        for j in range(TS_IN // TQ):
            vw_ref[0, gg, j] = vw_t[:, j * TQ:(j + 1) * TQ]

    kc_ref[0] = c_both[0]
    vc_ref[0] = c_both[1]
    ga_ref[0] = jax.nn.sigmoid(seg(SEG_GA)).T
    gm_cols = 2 * LANE * 2
    for c0 in range(SEG_GM[0], SEG_GM[1], gm_cols):
        part = seg((c0, c0 + gm_cols))
        pool_advance(5)
        gm_ref[0, :, c0 - SEG_GM[0]:c0 - SEG_GM[0] + gm_cols] = jax.nn.sigmoid(part).astype(bf16)
    for _ in pool:
        pass


def _in_proj(x, g, w, rc, rsa, rsb, pw, ps):
    B, S, D = x.shape
    grid = (B, S // TS_IN)
    tok = lambda width: pl.BlockSpec((1, TS_IN, width), lambda b, s: (b, s, 0))
    const = lambda shape: pl.BlockSpec(shape, lambda b, s: (0,) * len(shape),
                                       pipeline_mode=pl.Buffered(1))
    tab = pl.BlockSpec((TS_IN, LANE), lambda b, s: (s, 0))
    out_shapes = [
        jax.ShapeDtypeStruct((B, N_HEADS, LANE, S), bf16),
        jax.ShapeDtypeStruct((B, S, N_KV * LANE), bf16),
        jax.ShapeDtypeStruct((B, S, N_KV * LANE), bf16),
        jax.ShapeDtypeStruct((B, N_KV, S // TK, LANE, TK), bf16),
        jax.ShapeDtypeStruct((B, N_KV, S // TQ, LANE, TQ), bf16),
        jax.ShapeDtypeStruct((B, S, KV_WIDTH), f32),
        jax.ShapeDtypeStruct((B, S, KV_WIDTH), f32),
        jax.ShapeDtypeStruct((B, LANE, S), f32),
        jax.ShapeDtypeStruct((B, S, POOL_WIDTH), bf16),
        jax.ShapeDtypeStruct((B, S, 2 * D_MODEL), bf16),
    ]
    out_specs = [
        pl.BlockSpec((1, N_HEADS, LANE, TS_IN), lambda b, s: (b, 0, 0, s)),
        tok(N_KV * LANE), tok(N_KV * LANE),
        pl.BlockSpec((1, N_KV, TS_IN // TK, LANE, TK), lambda b, s: (b, 0, s, 0, 0)),
        pl.BlockSpec((1, N_KV, TS_IN // TQ, LANE, TQ), lambda b, s: (b, 0, s, 0, 0)),
        tok(KV_WIDTH), tok(KV_WIDTH),
        pl.BlockSpec((1, LANE, TS_IN), lambda b, s: (b, 0, s)),
        tok(POOL_WIDTH), tok(2 * D_MODEL),
    ]
    return pl.pallas_call(
        _in_proj_kernel,
        grid=grid,
        in_specs=[tok(D), const((1, D)), const((D, W_COLS)), tab, tab, tab,
                  const((len(POOL_WINDOWS), POOL_GROUP, POOL_GROUP)), const((1, POOL_WIDTH))],
        out_specs=out_specs,
        out_shape=out_shapes,
        scratch_shapes=[pltpu.VMEM((POOL_HALO + TS_IN, POOL_WIDTH), f32)],
        compiler_params=pltpu.CompilerParams(
            dimension_semantics=("arbitrary", "arbitrary"), vmem_limit_bytes=VMEM_LIMIT),
        name="in_proj",
    )(x, g, w, rc, rsa, rsb, pw, ps)


def _compress_kernel(kc_ref, vc_ref, pek_ref, pev_ref, kw1_ref, kw2_ref, vw1_ref, vw2_ref,
                     rc_ref, rsa_ref, rsb_ref, kout_ref, vout_ref):
    n_chunk = kc_ref.shape[1] // CMP_STRIDE

    def hidden(src_ref, pe_ref, w1_ref):
        parts = []
        for part in range(CMP_RATIO):
            acc = None
            for tok in range(0, CMP_STRIDE, 2):
                t = part * CMP_STRIDE + tok
                a = jnp.concatenate(
                    [(src_ref[0, pl.ds(tok + u, n_chunk, stride=CMP_STRIDE), :]
                      + pe_ref[t + u:t + u + 1, :]).astype(bf16) for u in range(2)], axis=1)
                d = _dot(a, w1_ref[t // 2])
                acc = d if acc is None else acc + d
            parts.append(acc)
        pre = parts[0] + pltpu.roll(parts[1], n_chunk - 1, axis=0)
        return jax.nn.gelu(pre, approximate=True).astype(bf16)

    lane = lax.broadcasted_iota(jnp.int32, (n_chunk, LANE), 1)
    lo = lane < HEAD_DIM
    kcmp = _dot(hidden(kc_ref, pek_ref, kw1_ref), kw2_ref[...])
    vcmp = _dot(hidden(vc_ref, pev_ref, vw1_ref), vw2_ref[...])
    for gg in range(N_KV):
        sl = slice(gg * LANE, (gg + 1) * LANE)
        kout_ref[0, gg] = _rope_lane_tile(kcmp[:, sl], rc_ref[...], rsa_ref[...], rsb_ref[...]).astype(bf16)
        vout_ref[0, gg] = jnp.where(lo, vcmp[:, sl], 1.0).T.astype(bf16)


def _compress(kc, vc, pek, pev, kw1, kw2, vw1, vw2, rc, rsa, rsb):
    B, S, W = kc.shape
    NC = S // CMP_STRIDE
    const = lambda shape: pl.BlockSpec(shape, lambda b: (0,) * len(shape))
    src = pl.BlockSpec((1, S, W), lambda b: (b, 0, 0))
    return pl.pallas_call(
        _compress_kernel,
        grid=(B,),
        in_specs=[src, src, const(pek.shape), const(pev.shape), const(kw1.shape), const(kw2.shape),
                  const(vw1.shape), const(vw2.shape), const(rc.shape), const(rsa.shape), const(rsb.shape)],
        out_specs=[pl.BlockSpec((1, N_KV, NC, LANE), lambda b: (b, 0, 0, 0)),
                   pl.BlockSpec((1, N_KV, LANE, NC), lambda b: (b, 0, 0, 0))],
        out_shape=[jax.ShapeDtypeStruct((B, N_KV, NC, LANE), bf16),
                   jax.ShapeDtypeStruct((B, N_KV, LANE, NC), bf16)],
        compiler_params=pltpu.CompilerParams(
            dimension_semantics=("arbitrary",), vmem_limit_bytes=VMEM_LIMIT),
        name="compress",
    )(kc, vc, pek, pev, kw1, kw2, vw1, vw2, rc, rsa, rsb)


def _attn_kernel(q_ref, kc_ref, vc_ref, ks_ref, vs_ref, kw_ref, vw_ref, ga_ref, mt_ref, wb_ref, o_ref,
                 qa_sc, s_sc, mx_sc, acc_sc, imp_sc, sel_sc, *, n_cmp, n_sel):
    g = pl.program_id(1)
    i = pl.program_id(2)
    t0 = i * TQ
    cols = GROUP * TQ
    ncp = kc_ref.shape[2]

    tok = t0 + lax.broadcasted_iota(jnp.int32, (1, TQ), 1)
    tok4 = t0 + (lax.broadcasted_iota(jnp.int32, (1, cols), 1) & (TQ - 1))
    head_cols = lambda a, r: a[:, r * TQ:(r + 1) * TQ]

    q_all = jnp.concatenate([q_ref[0, r] for r in range(GROUP)], axis=1)

    kc = kc_ref[0, 0]
    vc = vc_ref[0, 0]
    cpos = lax.broadcasted_iota(jnp.int32, (ncp, 1), 0)
    cvalid = ((cpos * CMP_STRIDE + (CMP_BLOCK - 1)) <= tok) & (cpos < n_cmp)
    cbias = jnp.where(cvalid, 0.0, NEG_INF).astype(f32)
    any_valid = (tok >= CMP_BLOCK - 1).astype(f32)
    s_cmp = _dot(kc, q_all)
    band = WINDOW + TQ
    w0 = pl.multiple_of(jnp.maximum(t0 - WINDOW, 0), TQ)
    wbias = wb_ref[jnp.minimum(i, WINDOW // TQ)]
    kwin = kw_ref[0, pl.ds(w0, band), :]
    vwin = jnp.concatenate([vw_ref[0, 0, w0 // TQ + j] for j in range(band // TQ)], axis=1)
    s_win = _dot(kwin, q_all).astype(bf16)
    psum = jnp.zeros((ncp, TQ), f32)
    e_cmp = []
    inv_cmp = []
    for r in range(GROUP):
        s = head_cols(s_cmp, r) + cbias
        e = jnp.exp2(s - jnp.max(s, axis=0, keepdims=True))
        inv = 1.0 / jnp.sum(e, axis=0, keepdims=True)
        psum = psum + e * inv
        e_cmp.append(e.astype(bf16))
        inv_cmp.append(inv * any_valid)
    pv_cmp = _dot(vc, jnp.concatenate(e_cmp, axis=1))

    mt = mt_ref[...]
    p_hi = psum.astype(bf16)
    rem = psum - p_hi.astype(f32)
    p_mid = rem.astype(bf16)
    p_lo = (rem - p_mid.astype(f32)).astype(bf16)
    imp = _dot(mt, p_hi) + _dot(mt, p_mid) + _dot(mt, p_lo)
    blk = lax.broadcasted_iota(jnp.int32, (n_sel, TQ), 0)
    cur = tok >> SEL_SHIFT
    imp = jnp.where(blk > cur, -FORCE_SCORE, imp)
    imp = jnp.where((blk == 0) | (blk == cur) | (blk == cur - 1), FORCE_SCORE, imp)

    p_win = []
    for r in range(GROUP):
        s = head_cols(s_win, r) + wbias
        p_win.append(jnp.exp2(s - jnp.max(s, axis=0, keepdims=True)))
    pv_win = _dot(vwin, jnp.concatenate(p_win, axis=1))

    imp_sc[...] = imp
    sel_sc[...] = jnp.full((n_sel, TQ), -1.0, f32)
    sub = lax.broadcasted_iota(jnp.int32, (SUBLANE, TQ), 0)
    row_keys = SUBLANE * SEL_BLOCK
    rows_in_play = (t0 + TQ + row_keys - 1) // row_keys
    for n_rows in range(1, n_sel // SUBLANE + 1):
        @pl.when(rows_in_play == n_rows)
        def _(n_rows=n_rows):
            vrows = [imp_sc[v * SUBLANE:(v + 1) * SUBLANE, :] for v in range(n_rows)]
            ahead = [jnp.zeros((SUBLANE, TQ), f32) for _ in range(n_rows)]
            for vj in range(n_rows):
                for sj in range(SUBLANE):
                    rival = jnp.broadcast_to(vrows[vj][sj:sj + 1, :], (SUBLANE, TQ))
                    for v in range(n_rows):
                        if v > vj:
                            ahead[v] = ahead[v] + jnp.where(rival >= vrows[v], 1.0, 0.0)
                        elif v < vj:
                            ahead[v] = ahead[v] + jnp.where(rival > vrows[v], 1.0, 0.0)
                        else:
                            tie = jnp.where(rival == vrows[v], (sub > sj).astype(f32), 0.0)
                            ahead[v] = ahead[v] + jnp.where(rival > vrows[v], 1.0, tie)
            for v in range(n_rows):
                sel_sc[v * SUBLANE:(v + 1) * SUBLANE, :] = jnp.where(
                    ahead[v] < float(min(SEL_TOPK, n_sel)), 0.0, -1.0)

    helper = jnp.concatenate([jnp.zeros((LANE - n_sel, TQ), f32), sel_sc[...]], axis=0).astype(bf16)
    qa_sc[...] = q_all + jnp.concatenate([helper] * GROUP, axis=1)


    n_tiles = (t0 + TQ + TK - 1) // TK
    mx_sc[...] = jnp.full((SUBLANE, cols), M_INIT, f32)

    def score_tiles(kts, causal):
        slabs = []
        for kt in kts:
            k0 = pl.multiple_of(kt * TK, TK)
            s = _dot(ks_ref[0, pl.ds(k0, TK), :], qa_sc[...])
            if causal:
                kpos = k0 + lax.broadcasted_iota(jnp.int32, (TK, 1), 0)
                s = jnp.where(kpos <= tok4, s, -BIG)
            s_sc[kt] = s.astype(bf16)
            slabs += [s[j * SUBLANE:(j + 1) * SUBLANE] for j in range(TK // SUBLANE)]
        mx_sc[...] = jnp.maximum(mx_sc[...], _tree(jnp.maximum, slabs))

    def weight_tiles(kts):
        m = mx_sc[0:1, :].astype(bf16)
        total = None
        for kt in kts:
            p = jnp.exp2(s_sc[kt] - m)
            d = _dot(vs_ref[0, 0, kt], p)
            total = d if total is None else total + d
        acc_sc[...] += total

    def sweep(n, tiles_fn):
        def trip(j, carry):
            tiles_fn([SWEEP_UNROLL * j + u for u in range(SWEEP_UNROLL)])
            return carry

        lax.fori_loop(0, n // SWEEP_UNROLL, trip, 0)
        group = SWEEP_UNROLL // 2
        while group >= 1:
            start = (n // (2 * group)) * (2 * group)

            @pl.when((n // group) % 2 == 1)
            def _(start=start, group=group):
                tiles_fn([start + u for u in range(group)])
            group //= 2

    sweep(n_tiles - 1, lambda kts: score_tiles(kts, False))
    score_tiles([n_tiles - 1], True)

    mx_sc[...] = jnp.broadcast_to(jnp.max(mx_sc[...], axis=0, keepdims=True), (SUBLANE, cols))
    acc_sc[...] = jnp.zeros((LANE, cols), f32)
    sweep(n_tiles, weight_tiles)
    pv_sel = acc_sc[...]

    outs = []
    for r in range(GROUP):
        def branch(pv):
            x = head_cols(pv, r)
            return x[:HEAD_DIM], x[HEAD_DIM:HEAD_DIM + 1]

        num_c, _ = branch(pv_cmp)
        num_s, den_s = branch(pv_sel)
        num_w, den_w = branch(pv_win)
        gate = lambda br: ga_ref[0, pl.ds(br * N_HEADS + g * GROUP + r, 1), :]
        outs.append((gate(0) * inv_cmp[r]) * num_c + (gate(1) / den_s) * num_s + (gate(2) / den_w) * num_w)
    o_ref[0] = jnp.concatenate(outs, axis=0).T.astype(bf16)


def _attention(q, kcmp, vcmp, ks, vs, kw, vw, ga, mt, wb, n_cmp):
    B, _, _, S = q.shape
    NC = kcmp.shape[2]
    n_sel = S // SEL_BLOCK
    cols = GROUP * TQ
    grid = (B, N_KV, S // TQ)
    keys_g = pl.BlockSpec((1, S, LANE), lambda b, g, i: (b, 0, g))
    const = lambda a: pl.BlockSpec(a.shape, lambda b, g, i: (0,) * a.ndim)
    return pl.pallas_call(
        functools.partial(_attn_kernel, n_cmp=n_cmp, n_sel=n_sel),
        grid=grid,
        in_specs=[
            pl.BlockSpec((1, GROUP, LANE, TQ), lambda b, g, i: (b, g, 0, i)),
            pl.BlockSpec((1, 1, NC, LANE), lambda b, g, i: (b, g, 0, 0)),
            pl.BlockSpec((1, 1, LANE, NC), lambda b, g, i: (b, g, 0, 0)),
            keys_g,
            pl.BlockSpec((1, 1, S // TK, LANE, TK), lambda b, g, i: (b, g, 0, 0, 0)),
            keys_g,
            pl.BlockSpec((1, 1, S // TQ, LANE, TQ), lambda b, g, i: (b, g, 0, 0, 0)),
            pl.BlockSpec((1, LANE, TQ), lambda b, g, i: (b, 0, i)),
            const(mt), const(wb),
        ],
        out_specs=pl.BlockSpec((1, TQ, GROUP * HEAD_DIM), lambda b, g, i: (b, i, g)),
        out_shape=jax.ShapeDtypeStruct((B, S, ATTN_WIDTH), bf16),
        scratch_shapes=[pltpu.VMEM((LANE, cols), bf16),
                        pltpu.VMEM((S // TK, TK, cols), bf16),
                        pltpu.VMEM((SUBLANE, cols), f32),
                        pltpu.VMEM((LANE, cols), f32),
                        pltpu.VMEM((n_sel, TQ), f32),
                        pltpu.VMEM((n_sel, TQ), f32)],
        compiler_params=pltpu.CompilerParams(
            dimension_semantics=("arbitrary", "arbitrary", "arbitrary"),
            vmem_limit_bytes=VMEM_LIMIT),
        name="attention",
    )(q, kcmp, vcmp, ks, vs, kw, vw, ga, mt, wb)


def _out_mlp_kernel(x_ref, ao_ref, pm_ref, gm_ref, wba_ref, wbp_ref, wo_ref, gn_ref,
                    w1_ref, w2_ref, gf_ref, o_ref):
    a = _dot(ao_ref[...], wba_ref[...])
    b = _dot(pm_ref[...], wbp_ref[...])
    gm = gm_ref[...].astype(f32)
    merged = (gm[:, :D_MODEL] * a + gm[:, D_MODEL:] * b).astype(bf16)
    x1 = x_ref[...] + _dot(merged, wo_ref[...])
    h = _rms(x1, gn_ref[...]).astype(bf16)
    f = jnp.square(jnp.maximum(_dot(h, w1_ref[...]), 0.0)).astype(bf16)
    x2 = x1 + _dot(f, w2_ref[...])
    o_ref[...] = _rms(x2, gf_ref[...])


def _out_mlp(x2d, ao, pm, gm, wba, wbp, wo, gn, w1, w2, gf):
    T, D = x2d.shape
    tok = lambda width: pl.BlockSpec((TS_OUT, width), lambda t: (t, 0))
    const = lambda arr: pl.BlockSpec(arr.shape, lambda t: (0, 0), pipeline_mode=pl.Buffered(1))
    return pl.pallas_call(
        _out_mlp_kernel,
        grid=(T // TS_OUT,),
        in_specs=[tok(D), tok(ATTN_WIDTH), tok(POOL_WIDTH), tok(2 * D_MODEL),
                  const(wba), const(wbp), const(wo), const(gn), const(w1), const(w2), const(gf)],
        out_specs=tok(D),
        out_shape=jax.ShapeDtypeStruct((T, D), f32),
        compiler_params=pltpu.CompilerParams(
            dimension_semantics=("arbitrary",), vmem_limit_bytes=VMEM_LIMIT),
        name="out_mlp",
    )(x2d, ao, pm, gm, wba, wbp, wo, gn, w1, w2, gf)


def _rope_tables(pos):
    inv = ROPE_THETA ** (-jnp.arange(0, ROT_DIM, 2, dtype=f32) / ROT_DIM)
    ang = pos.astype(f32)[:, None] * inv
    cos, sin = jnp.cos(ang), jnp.sin(ang)
    P = pos.shape[0]
    ones = jnp.ones((P, LANE - ROT_DIM), f32)
    zeros = jnp.zeros((P, LANE - ROT_HALF), f32)
    c = jnp.concatenate([cos, cos, ones], axis=1)
    sa = jnp.concatenate([-sin, zeros], axis=1)
    sb = jnp.concatenate([jnp.zeros((P, ROT_HALF), f32), sin,
                          jnp.zeros((P, LANE - ROT_DIM), f32)], axis=1)
    return c, sa, sb


def _cmp_to_sel_t(S):
    n_cmp = (S - CMP_BLOCK) // CMP_STRIDE + 1
    n_sel = S // SEL_BLOCK
    cs = np.arange(n_cmp)[:, None] * CMP_STRIDE
    js = np.arange(n_sel)[None, :] * SEL_BLOCK
    ov = np.clip(np.minimum(cs + CMP_BLOCK, js + SEL_BLOCK) - np.maximum(cs, js), 0, None)
    m = np.zeros((S // CMP_STRIDE, n_sel), np.float32)
    m[:n_cmp] = ov / CMP_STRIDE
    return jnp.asarray(m.T, dtype=bf16), n_cmp


def _window_bias():
    band = WINDOW + TQ
    k = np.arange(band)[:, None]
    r = np.arange(TQ)[None, :]
    cases = [k <= r + case * TQ for case in range(WINDOW // TQ)]
    cases.append((k > r) & (k <= r + WINDOW))
    return jnp.asarray(np.where(np.stack(cases), 0.0, -BIG), dtype=bf16)


def _layout_w_in(w):
    D = w.shape[0]
    w = w.astype(bf16)
    o = 0
    wq = w[:, o:o + ATTN_WIDTH]; o += ATTN_WIDTH
    kcw, vcw, ksw, vsw, kww, vww = [w[:, o + j * KV_WIDTH:o + (j + 1) * KV_WIDTH] for j in range(6)]
    o += 6 * KV_WIDTH
    gaw = w[:, o:o + N_BRANCH * N_HEADS]; o += N_BRANCH * N_HEADS
    uw = w[:, o:o + POOL_WIDTH]; o += POOL_WIDTH
    gmw = w[:, o:]
    gaw = jnp.concatenate([gaw, jnp.zeros((D, LANE - N_BRANCH * N_HEADS), w.dtype)], axis=1)
    out = jnp.concatenate([wq, ksw, kww, vsw, vww, kcw, vcw, uw, gmw, gaw], axis=1)
    assert out.shape[1] == W_COLS
    return out


def _block_diag_kv(m):
    z = jnp.zeros_like(m)
    return jnp.concatenate([jnp.concatenate([m, z], axis=-1), jnp.concatenate([z, m], axis=-1)], axis=-2)


def kernel(x, norm_mix, w_in, cmp_pe_k, cmp_pe_v, cmp_k_w1, cmp_k_w2, cmp_v_w1, cmp_v_w2,
           w_branch_attn, pool_w, pool_scale, w_branch_pool, w_out, norm_mlp, w_ff1, w_ff2,
           norm_final):
    B, S, D = x.shape
    assert norm_mix.shape[0] == 1, "single-layer block: the final norm is fused into out_mlp"
    assert S % TS_IN == 0 and S % TQ == 0 and (B * S) % TS_OUT == 0 and TS_IN == TK
    assert TK % TQ == 0 and S % TK == 0 and WINDOW % TQ == 0 and N_KV == 2
    assert SWEEP_UNROLL & (SWEEP_UNROLL - 1) == 0
    n_chunk = S // CMP_STRIDE
    rc, rsa, rsb = _rope_tables(jnp.arange(S))
    crc, crsa, crsb = _rope_tables(jnp.arange(n_chunk) * CMP_STRIDE + CMP_BLOCK - 1)
    mt, n_cmp = _cmp_to_sel_t(S)

    (q, ks, kw, vs, vw, kc, vc, ga, pm, gm) = _in_proj(
        x, norm_mix[0][None, :], _layout_w_in(w_in[0]), rc, rsa, rsb,
        pool_w[0].astype(bf16), pool_scale[0][None, :])

    def first_layer(w1):
        per_token = _block_diag_kv(w1.reshape(CMP_BLOCK, HEAD_DIM, CMP_HIDDEN)).astype(bf16)
        return per_token.reshape(CMP_BLOCK // 2, 2 * N_KV * HEAD_DIM, N_KV * CMP_HIDDEN)

    def second_layer(w2):
        return _block_diag_kv(jnp.concatenate([w2, jnp.zeros_like(w2)], axis=1)).astype(bf16)

    kcmp, vcmp = _compress(
        kc, vc, jnp.tile(cmp_pe_k[0], (1, N_KV)), jnp.tile(cmp_pe_v[0], (1, N_KV)),
        first_layer(cmp_k_w1[0]), second_layer(cmp_k_w2[0]),
        first_layer(cmp_v_w1[0]), second_layer(cmp_v_w2[0]), crc, crsa, crsb)

    ao = _attention(q, kcmp, vcmp, ks, vs, kw, vw, ga, mt, _window_bias(), n_cmp)

    y = _out_mlp(x.reshape(B * S, D), ao.reshape(B * S, ATTN_WIDTH), pm.reshape(B * S, POOL_WIDTH),
                 gm.reshape(B * S, 2 * D_MODEL), w_branch_attn[0].astype(bf16),
                 w_branch_pool[0].astype(bf16), w_out[0].astype(bf16), norm_mlp[0][None, :],
                 w_ff1[0].astype(bf16), w_ff2[0].astype(bf16), norm_final[None, :])
    return y.reshape(B, S, D)
```

```python
import functools
import math

import jax
import jax.numpy as jnp
import numpy as np
from jax import lax
from jax.experimental import pallas as pl
from jax.experimental.pallas import tpu as pltpu

f32 = jnp.float32
bf16 = jnp.bfloat16

D_MODEL = 1024
N_HEADS = 8
HEAD_DIM = 64
N_KV = 2
GROUP = N_HEADS // N_KV
ROT_DIM = HEAD_DIM // 4
ROT_HALF = ROT_DIM // 2
ROPE_THETA = 500000.0
CMP_BLOCK = 32
CMP_STRIDE = 16
CMP_RATIO = CMP_BLOCK // CMP_STRIDE
CMP_HIDDEN = 4 * HEAD_DIM
SEL_BLOCK = 64
SEL_SHIFT = 6
SEL_TOPK = 16
WINDOW = 512
N_BRANCH = 3
ATTN_WIDTH = N_HEADS * HEAD_DIM
KV_WIDTH = N_KV * HEAD_DIM
POOL_WIDTH = 512
POOL_WINDOWS = (2, 4, 8, 16)
POOL_GROUP = POOL_WIDTH // len(POOL_WINDOWS)
POOL_HALO = 16
D_FF = 4 * D_MODEL
EPS = 1e-6
NEG_INF = -1e30
FORCE_SCORE = 1e4
BIG = 2.0 ** 100
M_INIT = -3.0e38
Q_SCALE = HEAD_DIM ** -0.5 * math.log2(math.e)

LANE = 128
SUBLANE = 8
VMEM_LIMIT = 56 * 1024 * 1024

SEG_Q = (0, ATTN_WIDTH)
SEG_KS = (SEG_Q[1], SEG_Q[1] + KV_WIDTH)
SEG_KW = (SEG_KS[1], SEG_KS[1] + KV_WIDTH)
SEG_VS = (SEG_KW[1], SEG_KW[1] + KV_WIDTH)
SEG_VW = (SEG_VS[1], SEG_VS[1] + KV_WIDTH)
SEG_KC = (SEG_VW[1], SEG_VW[1] + KV_WIDTH)
SEG_VC = (SEG_KC[1], SEG_KC[1] + KV_WIDTH)
SEG_U = (SEG_VC[1], SEG_VC[1] + POOL_WIDTH)
SEG_GM = (SEG_U[1], SEG_U[1] + 2 * D_MODEL)
SEG_GA = (SEG_GM[1], SEG_GM[1] + LANE)
W_COLS = SEG_GA[1]

TS_IN = 512
TQ = 256
TK = 512
SWEEP_UNROLL = 4
TS_OUT = 512


def _dot(a, b):
    return jnp.dot(a, b, preferred_element_type=f32)


def _swap_halves(x):
    return pltpu.roll(x, HEAD_DIM, axis=1)


def _rope_lane_tile(x, c, sa, sb):
    return (x * c + pltpu.roll(x, LANE - ROT_HALF, axis=1) * sa
            + pltpu.roll(x, ROT_HALF, axis=1) * sb)


def _rms(x, g):
    ms = jnp.mean(x * x, axis=-1, keepdims=True)
    return x * lax.rsqrt(ms + EPS) * g


def _lane_chunks(s):
    return [s[:, c * LANE:(c + 1) * LANE] for c in range(s.shape[1] // LANE)]


def _tree(fn, items):
    items = list(items)
    while len(items) > 1:
        items = [fn(items[j], items[j + 1]) if j + 1 < len(items) else items[j]
                 for j in range(0, len(items), 2)]
    return items[0]


def _in_proj_kernel(x_ref, g_ref, w_ref, rc_ref, rsa_ref, rsb_ref, pw_ref, ps_ref,
                    q_ref, ks_ref, kw_ref, vs_ref, vw_ref, kc_ref, vc_ref, ga_ref, pm_ref, gm_ref,
                    ubuf):
    si = pl.program_id(1)
    t0 = si * TS_IN

    @pl.when(si == 0)
    def _():
        ubuf[0:POOL_HALO, :] = jnp.zeros((POOL_HALO, POOL_WIDTH), f32)

    h = _rms(x_ref[0], g_ref[...]).astype(bf16)

    def seg(s):
        return _dot(h, w_ref[:, s[0]:s[1]])

    u = seg(SEG_U)
    ubuf[POOL_HALO:, :] = u
    tpos1 = (t0 + 1 + lax.broadcasted_iota(jnp.int32, (TS_IN, 1), 0)).astype(f32)

    def pool_steps():
        for gi, w in enumerate(POOL_WINDOWS):
            sl = slice(gi * POOL_GROUP, (gi + 1) * POOL_GROUP)
            acc = u[:, sl]
            for k in range(1, w):
                acc = acc + ubuf[POOL_HALO - k:POOL_HALO - k + TS_IN, sl]
                yield
            cnt = jnp.minimum(tpos1, float(w))
            pooled = (acc / cnt - u[:, sl]).astype(bf16)
            mixed = _dot(pooled, pw_ref[gi]) * ps_ref[:, sl]
            pm_ref[0, :, sl] = mixed.astype(bf16)
            yield
        ubuf[0:POOL_HALO, :] = ubuf[TS_IN:TS_IN + POOL_HALO, :]

    pool = pool_steps()

    def pool_advance(n):
        for _ in range(n):
            next(pool, None)

    rc, rsa, rsb = rc_ref[...], rsa_ref[...], rsb_ref[...]
    row_t = t0 + lax.broadcasted_iota(jnp.int32, (TS_IN, LANE), 0)
    lane = lax.broadcasted_iota(jnp.int32, (TS_IN, LANE), 1)
    lo = lane < HEAD_DIM

    def head_tiles(x):
        return (x, _swap_halves(x))

    q = seg(SEG_Q)
    pool_advance(4)
    for m, chunk in enumerate(_lane_chunks(q)):
        for half, xh in enumerate(head_tiles(chunk)):
            rot = jnp.where(lo, _rope_lane_tile(xh, rc, rsa, rsb) * Q_SCALE, 0.0)
            q_ref[0, 2 * m + half] = rot.T.astype(bf16)

    onehot = jnp.where(lane - HEAD_DIM == (row_t >> SEL_SHIFT), BIG, 0.0).astype(f32)
    k_both = _lane_chunks(seg((SEG_KS[0], SEG_KW[1])))
    pool_advance(2)
    v_both = _lane_chunks(seg((SEG_VS[0], SEG_VW[1])))
    pool_advance(2)
    c_both = _lane_chunks(seg((SEG_KC[0], SEG_VC[1])))
    pool_advance(2)
    for gg, (ksh, kwh, vsh, vwh) in enumerate(zip(head_tiles(k_both[0]), head_tiles(k_both[1]),
                                                   head_tiles(v_both[0]), head_tiles(v_both[1]))):
        sl = slice(gg * LANE, (gg + 1) * LANE)
        ks_ref[0, :, sl] = jnp.where(lo, _rope_lane_tile(ksh, rc, rsa, rsb), onehot).astype(bf16)
        kw_ref[0, :, sl] = jnp.where(lo, _rope_lane_tile(kwh, rc, rsa, rsb), 0.0).astype(bf16)
        vs_ref[0, gg, 0] = jnp.where(lo, vsh, 1.0).T.astype(bf16)
        vw_t = jnp.where(lo, vwh, 1.0).T.astype(bf16)
        for j in range(TS_IN // TQ):
            vw_ref[0, gg, j] = vw_t[:, j * TQ:(j + 1) * TQ]

    kc_ref[0] = c_both[0]
    vc_ref[0] = c_both[1]
    ga_ref[0] = jax.nn.sigmoid(seg(SEG_GA)).T
    gm_cols = 2 * LANE * 2
    for c0 in range(SEG_GM[0], SEG_GM[1], gm_cols):
        part = seg((c0, c0 + gm_cols))
        pool_advance(5)
        gm_ref[0, :, c0 - SEG_GM[0]:c0 - SEG_GM[0] + gm_cols] = jax.nn.sigmoid(part).astype(bf16)
    for _ in pool:
        pass


def _in_proj(x, g, w, rc, rsa, rsb, pw, ps):
    B, S, D = x.shape
    grid = (B, S // TS_IN)
    tok = lambda width: pl.BlockSpec((1, TS_IN, width), lambda b, s: (b, s, 0))
    const = lambda shape: pl.BlockSpec(shape, lambda b, s: (0,) * len(shape),
                                       pipeline_mode=pl.Buffered(1))
    tab = pl.BlockSpec((TS_IN, LANE), lambda b, s: (s, 0))
    out_shapes = [
        jax.ShapeDtypeStruct((B, N_HEADS, LANE, S), bf16),
        jax.ShapeDtypeStruct((B, S, N_KV * LANE), bf16),
        jax.ShapeDtypeStruct((B, S, N_KV * LANE), bf16),
        jax.ShapeDtypeStruct((B, N_KV, S // TK, LANE, TK), bf16),
        jax.ShapeDtypeStruct((B, N_KV, S // TQ, LANE, TQ), bf16),
        jax.ShapeDtypeStruct((B, S, KV_WIDTH), f32),
        jax.ShapeDtypeStruct((B, S, KV_WIDTH), f32),
        jax.ShapeDtypeStruct((B, LANE, S), f32),
        jax.ShapeDtypeStruct((B, S, POOL_WIDTH), bf16),
        jax.ShapeDtypeStruct((B, S, 2 * D_MODEL), bf16),
    ]
    out_specs = [
        pl.BlockSpec((1, N_HEADS, LANE, TS_IN), lambda b, s: (b, 0, 0, s)),
        tok(N_KV * LANE), tok(N_KV * LANE),
        pl.BlockSpec((1, N_KV, TS_IN // TK, LANE, TK), lambda b, s: (b, 0, s, 0, 0)),
        pl.BlockSpec((1, N_KV, TS_IN // TQ, LANE, TQ), lambda b, s: (b, 0, s, 0, 0)),
        tok(KV_WIDTH), tok(KV_WIDTH),
        pl.BlockSpec((1, LANE, TS_IN), lambda b, s: (b, 0, s)),
        tok(POOL_WIDTH), tok(2 * D_MODEL),
    ]
    return pl.pallas_call(
        _in_proj_kernel,
        grid=grid,
        in_specs=[tok(D), const((1, D)), const((D, W_COLS)), tab, tab, tab,
                  const((len(POOL_WINDOWS), POOL_GROUP, POOL_GROUP)), const((1, POOL_WIDTH))],
        out_specs=out_specs,
        out_shape=out_shapes,
        scratch_shapes=[pltpu.VMEM((POOL_HALO + TS_IN, POOL_WIDTH), f32)],
        compiler_params=pltpu.CompilerParams(
            dimension_semantics=("arbitrary", "arbitrary"), vmem_limit_bytes=VMEM_LIMIT),
        name="in_proj",
    )(x, g, w, rc, rsa, rsb, pw, ps)


def _compress_kernel(kc_ref, vc_ref, pek_ref, pev_ref, kw1_ref, kw2_ref, vw1_ref, vw2_ref,
                     rc_ref, rsa_ref, rsb_ref, kout_ref, vout_ref):
    n_chunk = kc_ref.shape[1] // CMP_STRIDE

    def hidden(src_ref, pe_ref, w1_ref):
        parts = []
        for part in range(CMP_RATIO):
            acc = None
            for tok in range(0, CMP_STRIDE, 2):
                t = part * CMP_STRIDE + tok
                a = jnp.concatenate(
                    [(src_ref[0, pl.ds(tok + u, n_chunk, stride=CMP_STRIDE), :]
                      + pe_ref[t + u:t + u + 1, :]).astype(bf16) for u in range(2)], axis=1)
                d = _dot(a, w1_ref[t // 2])
                acc = d if acc is None else acc + d
            parts.append(acc)
        pre = parts[0] + pltpu.roll(parts[1], n_chunk - 1, axis=0)
        return jax.nn.gelu(pre, approximate=True).astype(bf16)

    lane = lax.broadcasted_iota(jnp.int32, (n_chunk, LANE), 1)
    lo = lane < HEAD_DIM
    kcmp = _dot(hidden(kc_ref, pek_ref, kw1_ref), kw2_ref[...])
    vcmp = _dot(hidden(vc_ref, pev_ref, vw1_ref), vw2_ref[...])
    for gg in range(N_KV):
        sl = slice(gg * LANE, (gg + 1) * LANE)
        kout_ref[0, gg] = _rope_lane_tile(kcmp[:, sl], rc_ref[...], rsa_ref[...], rsb_ref[...]).astype(bf16)
        vout_ref[0, gg] = jnp.where(lo, vcmp[:, sl], 1.0).T.astype(bf16)


def _compress(kc, vc, pek, pev, kw1, kw2, vw1, vw2, rc, rsa, rsb):
    B, S, W = kc.shape
    NC = S // CMP_STRIDE
    const = lambda shape: pl.BlockSpec(shape, lambda b: (0,) * len(shape))
    src = pl.BlockSpec((1, S, W), lambda b: (b, 0, 0))
    return pl.pallas_call(
        _compress_kernel,
        grid=(B,),
        in_specs=[src, src, const(pek.shape), const(pev.shape), const(kw1.shape), const(kw2.shape),
                  const(vw1.shape), const(vw2.shape), const(rc.shape), const(rsa.shape), const(rsb.shape)],
        out_specs=[pl.BlockSpec((1, N_KV, NC, LANE), lambda b: (b, 0, 0, 0)),
                   pl.BlockSpec((1, N_KV, LANE, NC), lambda b: (b, 0, 0, 0))],
        out_shape=[jax.ShapeDtypeStruct((B, N_KV, NC, LANE), bf16),
                   jax.ShapeDtypeStruct((B, N_KV, LANE, NC), bf16)],
        compiler_params=pltpu.CompilerParams(
            dimension_semantics=("arbitrary",), vmem_limit_bytes=VMEM_LIMIT),
        name="compress",
    )(kc, vc, pek, pev, kw1, kw2, vw1, vw2, rc, rsa, rsb)


def _attn_kernel(q_ref, kc_ref, vc_ref, ks_ref, vs_ref, kw_ref, vw_ref, ga_ref, mt_ref, wb_ref, o_ref,
                 qa_sc, s_sc, mx_sc, acc_sc, imp_sc, sel_sc, *, n_cmp, n_sel):
    g = pl.program_id(1)
    i = pl.program_id(2)
    t0 = i * TQ
    cols = GROUP * TQ
    ncp = kc_ref.shape[2]

    tok = t0 + lax.broadcasted_iota(jnp.int32, (1, TQ), 1)
    tok4 = t0 + (lax.broadcasted_iota(jnp.int32, (1, cols), 1) & (TQ - 1))
    head_cols = lambda a, r: a[:, r * TQ:(r + 1) * TQ]

    q_all = jnp.concatenate([q_ref[0, r] for r in range(GROUP)], axis=1)

    kc = kc_ref[0, 0]
    vc = vc_ref[0, 0]
    cpos = lax.broadcasted_iota(jnp.int32, (ncp, 1), 0)
    cvalid = ((cpos * CMP_STRIDE + (CMP_BLOCK - 1)) <= tok) & (cpos < n_cmp)
    cbias = jnp.where(cvalid, 0.0, NEG_INF).astype(f32)
    any_valid = (tok >= CMP_BLOCK - 1).astype(f32)
    s_cmp = _dot(kc, q_all)
    band = WINDOW + TQ
    w0 = pl.multiple_of(jnp.maximum(t0 - WINDOW, 0), TQ)
    wbias = wb_ref[jnp.minimum(i, WINDOW // TQ)]
    kwin = kw_ref[0, pl.ds(w0, band), :]
    vwin = jnp.concatenate([vw_ref[0, 0, w0 // TQ + j] for j in range(band // TQ)], axis=1)
    s_win = _dot(kwin, q_all).astype(bf16)
    psum = jnp.zeros((ncp, TQ), f32)
    e_cmp = []
    inv_cmp = []
    for r in range(GROUP):
        s = head_cols(s_cmp, r) + cbias
        e = jnp.exp2(s - jnp.max(s, axis=0, keepdims=True))
        inv = 1.0 / jnp.sum(e, axis=0, keepdims=True)
        psum = psum + e * inv
        e_cmp.append(e.astype(bf16))
        inv_cmp.append(inv * any_valid)
    pv_cmp = _dot(vc, jnp.concatenate(e_cmp, axis=1))

    mt = mt_ref[...]
    p_hi = psum.astype(bf16)
    rem = psum - p_hi.astype(f32)
    p_mid = rem.astype(bf16)
    p_lo = (rem - p_mid.astype(f32)).astype(bf16)
    imp = _dot(mt, p_hi) + _dot(mt, p_mid) + _dot(mt, p_lo)
    blk = lax.broadcasted_iota(jnp.int32, (n_sel, TQ), 0)
    cur = tok >> SEL_SHIFT
    imp = jnp.where(blk > cur, -FORCE_SCORE, imp)
    imp = jnp.where((blk == 0) | (blk == cur) | (blk == cur - 1), FORCE_SCORE, imp)

    p_win = []
    for r in range(GROUP):
        s = head_cols(s_win, r) + wbias
        p_win.append(jnp.exp2(s - jnp.max(s, axis=0, keepdims=True)))
    pv_win = _dot(vwin, jnp.concatenate(p_win, axis=1))

    imp_sc[...] = imp
    sel_sc[...] = jnp.full((n_sel, TQ), -1.0, f32)
    sub = lax.broadcasted_iota(jnp.int32, (SUBLANE, TQ), 0)
    row_keys = SUBLANE * SEL_BLOCK
    rows_in_play = (t0 + TQ + row_keys - 1) // row_keys
    for n_rows in range(1, n_sel // SUBLANE + 1):
        @pl.when(rows_in_play == n_rows)
        def _(n_rows=n_rows):
            vrows = [imp_sc[v * SUBLANE:(v + 1) * SUBLANE, :] for v in range(n_rows)]
            ahead = [jnp.zeros((SUBLANE, TQ), f32) for _ in range(n_rows)]
            for vj in range(n_rows):
                for sj in range(SUBLANE):
                    rival = jnp.broadcast_to(vrows[vj][sj:sj + 1, :], (SUBLANE, TQ))
                    for v in range(n_rows):
                        if v > vj:
                            ahead[v] = ahead[v] + jnp.where(rival >= vrows[v], 1.0, 0.0)
                        elif v < vj:
                            ahead[v] = ahead[v] + jnp.where(rival > vrows[v], 1.0, 0.0)
                        else:
                            tie = jnp.where(rival == vrows[v], (sub > sj).astype(f32), 0.0)
                            ahead[v] = ahead[v] + jnp.where(rival > vrows[v], 1.0, tie)
            for v in range(n_rows):
                sel_sc[v * SUBLANE:(v + 1) * SUBLANE, :] = jnp.where(
                    ahead[v] < float(min(SEL_TOPK, n_sel)), 0.0, -1.0)

    helper = jnp.concatenate([jnp.zeros((LANE - n_sel, TQ), f32), sel_sc[...]], axis=0).astype(bf16)
    qa_sc[...] = q_all + jnp.concatenate([helper] * GROUP, axis=1)


    n_tiles = (t0 + TQ + TK - 1) // TK
    mx_sc[...] = jnp.full((SUBLANE, cols), M_INIT, f32)

    def score_tiles(kts, ends_sweep):
        slabs = []
        for idx, kt in enumerate(kts):
            k0 = pl.multiple_of(kt * TK, TK)
            s = _dot(ks_ref[0, pl.ds(k0, TK), :], qa_sc[...])
            if ends_sweep and idx == len(kts) - 1:
                kpos = k0 + lax.broadcasted_iota(jnp.int32, (TK, 1), 0)
                s = jnp.where(kpos <= tok4, s, -BIG)
            s_sc[kt] = s.astype(bf16)
            slabs += [s[j * SUBLANE:(j + 1) * SUBLANE] for j in range(TK // SUBLANE)]
        mx_sc[...] = jnp.maximum(mx_sc[...], _tree(jnp.maximum, slabs))

    def weight_tiles(kts, ends_sweep):
        m = mx_sc[0:1, :].astype(bf16)
        total = None
        for kt in kts:
            p = jnp.exp2(s_sc[kt] - m)
            d = _dot(vs_ref[0, 0, kt], p)
            total = d if total is None else total + d
        acc_sc[...] += total

    def sweep(n, tiles_fn):
        def trip(j, carry):
            tiles_fn([SWEEP_UNROLL * j + u for u in range(SWEEP_UNROLL)], False)
            return carry

        trips = (n - 1) // SWEEP_UNROLL
        lax.fori_loop(0, trips, trip, 0)
        done = trips * SWEEP_UNROLL
        for left in range(1, SWEEP_UNROLL + 1):
            @pl.when(n - done == left)
            def _(left=left):
                tiles_fn([done + u for u in range(left)], True)

    sweep(n_tiles, score_tiles)
    mx_sc[...] = jnp.broadcast_to(jnp.max(mx_sc[...], axis=0, keepdims=True), (SUBLANE, cols))
    acc_sc[...] = jnp.zeros((LANE, cols), f32)
    sweep(n_tiles, weight_tiles)
    pv_sel = acc_sc[...]

    outs = []
    for r in range(GROUP):
        def branch(pv):
            x = head_cols(pv, r)
            return x[:HEAD_DIM], x[HEAD_DIM:HEAD_DIM + 1]

        num_c, _ = branch(pv_cmp)
        num_s, den_s = branch(pv_sel)
        num_w, den_w = branch(pv_win)
        gate = lambda br: ga_ref[0, pl.ds(br * N_HEADS + g * GROUP + r, 1), :]
        outs.append((gate(0) * inv_cmp[r]) * num_c + (gate(1) / den_s) * num_s + (gate(2) / den_w) * num_w)
    o_ref[0] = jnp.concatenate(outs, axis=0).T.astype(bf16)


def _attention(q, kcmp, vcmp, ks, vs, kw, vw, ga, mt, wb, n_cmp):
    B, _, _, S = q.shape
    NC = kcmp.shape[2]
    n_sel = S // SEL_BLOCK
    cols = GROUP * TQ
    grid = (B, N_KV, S // TQ)
    keys_g = pl.BlockSpec((1, S, LANE), lambda b, g, i: (b, 0, g))
    const = lambda a: pl.BlockSpec(a.shape, lambda b, g, i: (0,) * a.ndim)
    return pl.pallas_call(
        functools.partial(_attn_kernel, n_cmp=n_cmp, n_sel=n_sel),
        grid=grid,
        in_specs=[
            pl.BlockSpec((1, GROUP, LANE, TQ), lambda b, g, i: (b, g, 0, i)),
            pl.BlockSpec((1, 1, NC, LANE), lambda b, g, i: (b, g, 0, 0)),
            pl.BlockSpec((1, 1, LANE, NC), lambda b, g, i: (b, g, 0, 0)),
            keys_g,
            pl.BlockSpec((1, 1, S // TK, LANE, TK), lambda b, g, i: (b, g, 0, 0, 0)),
            keys_g,
            pl.BlockSpec((1, 1, S // TQ, LANE, TQ), lambda b, g, i: (b, g, 0, 0, 0)),
            pl.BlockSpec((1, LANE, TQ), lambda b, g, i: (b, 0, i)),
            const(mt), const(wb),
        ],
        out_specs=pl.BlockSpec((1, TQ, GROUP * HEAD_DIM), lambda b, g, i: (b, i, g)),
        out_shape=jax.ShapeDtypeStruct((B, S, ATTN_WIDTH), bf16),
        scratch_shapes=[pltpu.VMEM((LANE, cols), bf16),
                        pltpu.VMEM((S // TK, TK, cols), bf16),
                        pltpu.VMEM((SUBLANE, cols), f32),
                        pltpu.VMEM((LANE, cols), f32),
                        pltpu.VMEM((n_sel, TQ), f32),
                        pltpu.VMEM((n_sel, TQ), f32)],
        compiler_params=pltpu.CompilerParams(
            dimension_semantics=("arbitrary", "arbitrary", "arbitrary"),
            vmem_limit_bytes=VMEM_LIMIT),
        name="attention",
    )(q, kcmp, vcmp, ks, vs, kw, vw, ga, mt, wb)


def _out_mlp_kernel(x_ref, ao_ref, pm_ref, gm_ref, wba_ref, wbp_ref, wo_ref, gn_ref,
                    w1_ref, w2_ref, gf_ref, o_ref):
    a = _dot(ao_ref[...], wba_ref[...])
    b = _dot(pm_ref[...], wbp_ref[...])
    gm = gm_ref[...].astype(f32)
    merged = (gm[:, :D_MODEL] * a + gm[:, D_MODEL:] * b).astype(bf16)
    x1 = x_ref[...] + _dot(merged, wo_ref[...])
    h = _rms(x1, gn_ref[...]).astype(bf16)
    f = jnp.square(jnp.maximum(_dot(h, w1_ref[...]), 0.0)).astype(bf16)
    x2 = x1 + _dot(f, w2_ref[...])
    o_ref[...] = _rms(x2, gf_ref[...])


def _out_mlp(x2d, ao, pm, gm, wba, wbp, wo, gn, w1, w2, gf):
    T, D = x2d.shape
    tok = lambda width: pl.BlockSpec((TS_OUT, width), lambda t: (t, 0))
    const = lambda arr: pl.BlockSpec(arr.shape, lambda t: (0, 0), pipeline_mode=pl.Buffered(1))
    return pl.pallas_call(
        _out_mlp_kernel,
        grid=(T // TS_OUT,),
        in_specs=[tok(D), tok(ATTN_WIDTH), tok(POOL_WIDTH), tok(2 * D_MODEL),
                  const(wba), const(wbp), const(wo), const(gn), const(w1), const(w2), const(gf)],
        out_specs=tok(D),
        out_shape=jax.ShapeDtypeStruct((T, D), f32),
        compiler_params=pltpu.CompilerParams(
            dimension_semantics=("arbitrary",), vmem_limit_bytes=VMEM_LIMIT),
        name="out_mlp",
    )(x2d, ao, pm, gm, wba, wbp, wo, gn, w1, w2, gf)


def _rope_tables(pos):
    inv = ROPE_THETA ** (-jnp.arange(0, ROT_DIM, 2, dtype=f32) / ROT_DIM)
    ang = pos.astype(f32)[:, None] * inv
    cos, sin = jnp.cos(ang), jnp.sin(ang)
    P = pos.shape[0]
    ones = jnp.ones((P, LANE - ROT_DIM), f32)
    zeros = jnp.zeros((P, LANE - ROT_HALF), f32)
    c = jnp.concatenate([cos, cos, ones], axis=1)
    sa = jnp.concatenate([-sin, zeros], axis=1)
    sb = jnp.concatenate([jnp.zeros((P, ROT_HALF), f32), sin,
                          jnp.zeros((P, LANE - ROT_DIM), f32)], axis=1)
    return c, sa, sb


def _cmp_to_sel_t(S):
    n_cmp = (S - CMP_BLOCK) // CMP_STRIDE + 1
    n_sel = S // SEL_BLOCK
    cs = np.arange(n_cmp)[:, None] * CMP_STRIDE
    js = np.arange(n_sel)[None, :] * SEL_BLOCK
    ov = np.clip(np.minimum(cs + CMP_BLOCK, js + SEL_BLOCK) - np.maximum(cs, js), 0, None)
    m = np.zeros((S // CMP_STRIDE, n_sel), np.float32)
    m[:n_cmp] = ov / CMP_STRIDE
    return jnp.asarray(m.T, dtype=bf16), n_cmp


def _window_bias():
    band = WINDOW + TQ
    k = np.arange(band)[:, None]
    r = np.arange(TQ)[None, :]
    cases = [k <= r + case * TQ for case in range(WINDOW // TQ)]
    cases.append((k > r) & (k <= r + WINDOW))
    return jnp.asarray(np.where(np.stack(cases), 0.0, -BIG), dtype=bf16)


def _layout_w_in(w):
    D = w.shape[0]
    w = w.astype(bf16)
    o = 0
    wq = w[:, o:o + ATTN_WIDTH]; o += ATTN_WIDTH
    kcw, vcw, ksw, vsw, kww, vww = [w[:, o + j * KV_WIDTH:o + (j + 1) * KV_WIDTH] for j in range(6)]
    o += 6 * KV_WIDTH
    gaw = w[:, o:o + N_BRANCH * N_HEADS]; o += N_BRANCH * N_HEADS
    uw = w[:, o:o + POOL_WIDTH]; o += POOL_WIDTH
    gmw = w[:, o:]
    gaw = jnp.concatenate([gaw, jnp.zeros((D, LANE - N_BRANCH * N_HEADS), w.dtype)], axis=1)
    out = jnp.concatenate([wq, ksw, kww, vsw, vww, kcw, vcw, uw, gmw, gaw], axis=1)
    assert out.shape[1] == W_COLS
    return out


def _block_diag_kv(m):
    z = jnp.zeros_like(m)
    return jnp.concatenate([jnp.concatenate([m, z], axis=-1), jnp.concatenate([z, m], axis=-1)], axis=-2)


def kernel(x, norm_mix, w_in, cmp_pe_k, cmp_pe_v, cmp_k_w1, cmp_k_w2, cmp_v_w1, cmp_v_w2,
           w_branch_attn, pool_w, pool_scale, w_branch_pool, w_out, norm_mlp, w_ff1, w_ff2,
           norm_final):
    B, S, D = x.shape
    assert norm_mix.shape[0] == 1, "single-layer block: the final norm is fused into out_mlp"
    assert S % TS_IN == 0 and S % TQ == 0 and (B * S) % TS_OUT == 0 and TS_IN == TK
    assert TK % TQ == 0 and S % TK == 0 and WINDOW % TQ == 0 and N_KV == 2
    assert SWEEP_UNROLL & (SWEEP_UNROLL - 1) == 0
    n_chunk = S // CMP_STRIDE
    rc, rsa, rsb = _rope_tables(jnp.arange(S))
    crc, crsa, crsb = _rope_tables(jnp.arange(n_chunk) * CMP_STRIDE + CMP_BLOCK - 1)
    mt, n_cmp = _cmp_to_sel_t(S)

    (q, ks, kw, vs, vw, kc, vc, ga, pm, gm) = _in_proj(
        x, norm_mix[0][None, :], _layout_w_in(w_in[0]), rc, rsa, rsb,
        pool_w[0].astype(bf16), pool_scale[0][None, :])

    def first_layer(w1):
        per_token = _block_diag_kv(w1.reshape(CMP_BLOCK, HEAD_DIM, CMP_HIDDEN)).astype(bf16)
        return per_token.reshape(CMP_BLOCK // 2, 2 * N_KV * HEAD_DIM, N_KV * CMP_HIDDEN)

    def second_layer(w2):
        return _block_diag_kv(jnp.concatenate([w2, jnp.zeros_like(w2)], axis=1)).astype(bf16)

    kcmp, vcmp = _compress(
        kc, vc, jnp.tile(cmp_pe_k[0], (1, N_KV)), jnp.tile(cmp_pe_v[0], (1, N_KV)),
        first_layer(cmp_k_w1[0]), second_layer(cmp_k_w2[0]),
        first_layer(cmp_v_w1[0]), second_layer(cmp_v_w2[0]), crc, crsa, crsb)

    ao = _attention(q, kcmp, vcmp, ks, vs, kw, vw, ga, mt, _window_bias(), n_cmp)

    y = _out_mlp(x.reshape(B * S, D), ao.reshape(B * S, ATTN_WIDTH), pm.reshape(B * S, POOL_WIDTH),
                 gm.reshape(B * S, 2 * D_MODEL), w_branch_attn[0].astype(bf16),
                 w_branch_pool[0].astype(bf16), w_out[0].astype(bf16), norm_mlp[0][None, :],
                 w_ff1[0].astype(bf16), w_ff2[0].astype(bf16), norm_final[None, :])
    return y.reshape(B, S, D)
```

```python
import functools
import math

import jax
import jax.numpy as jnp
import numpy as np
from jax import lax
from jax.experimental import pallas as pl
from jax.experimental.pallas import tpu as pltpu

f32 = jnp.float32
bf16 = jnp.bfloat16

D_MODEL = 1024
N_HEADS = 8
HEAD_DIM = 64
N_KV = 2
GROUP = N_HEADS // N_KV
ROT_DIM = HEAD_DIM // 4
ROT_HALF = ROT_DIM // 2
ROPE_THETA = 500000.0
CMP_BLOCK = 32
CMP_STRIDE = 16
CMP_RATIO = CMP_BLOCK // CMP_STRIDE
CMP_HIDDEN = 4 * HEAD_DIM
SEL_BLOCK = 64
SEL_SHIFT = 6
SEL_TOPK = 16
WINDOW = 512
N_BRANCH = 3
ATTN_WIDTH = N_HEADS * HEAD_DIM
KV_WIDTH = N_KV * HEAD_DIM
POOL_WIDTH = 512
POOL_WINDOWS = (2, 4, 8, 16)
POOL_GROUP = POOL_WIDTH // len(POOL_WINDOWS)
POOL_HALO = 16
D_FF = 4 * D_MODEL
EPS = 1e-6
NEG_INF = -1e30
FORCE_SCORE = 1e4
BIG = 2.0 ** 100
M_INIT = -3.0e38
Q_SCALE = HEAD_DIM ** -0.5 * math.log2(math.e)

LANE = 128
SUBLANE = 8
VMEM_LIMIT = 56 * 1024 * 1024

SEG_Q = (0, ATTN_WIDTH)
SEG_KS = (SEG_Q[1], SEG_Q[1] + KV_WIDTH)
SEG_KW = (SEG_KS[1], SEG_KS[1] + KV_WIDTH)
SEG_VS = (SEG_KW[1], SEG_KW[1] + KV_WIDTH)
SEG_VW = (SEG_VS[1], SEG_VS[1] + KV_WIDTH)
SEG_KC = (SEG_VW[1], SEG_VW[1] + KV_WIDTH)
SEG_VC = (SEG_KC[1], SEG_KC[1] + KV_WIDTH)
SEG_U = (SEG_VC[1], SEG_VC[1] + POOL_WIDTH)
SEG_GM = (SEG_U[1], SEG_U[1] + 2 * D_MODEL)
SEG_GA = (SEG_GM[1], SEG_GM[1] + LANE)
W_COLS = SEG_GA[1]

TS_IN = 512
TQ = 256
TK = 512
SWEEP_UNROLL = 8
TS_OUT = 512


def _dot(a, b):
    return jnp.dot(a, b, preferred_element_type=f32)


def _swap_halves(x):
    return pltpu.roll(x, HEAD_DIM, axis=1)


def _rope_lane_tile(x, c, sa, sb):
    return (x * c + pltpu.roll(x, LANE - ROT_HALF, axis=1) * sa
            + pltpu.roll(x, ROT_HALF, axis=1) * sb)


def _rms(x, g):
    ms = jnp.mean(x * x, axis=-1, keepdims=True)
    return x * lax.rsqrt(ms + EPS) * g


def _lane_chunks(s):
    return [s[:, c * LANE:(c + 1) * LANE] for c in range(s.shape[1] // LANE)]


def _tree(fn, items):
    items = list(items)
    while len(items) > 1:
        items = [fn(items[j], items[j + 1]) if j + 1 < len(items) else items[j]
                 for j in range(0, len(items), 2)]
    return items[0]


def _in_proj_kernel(x_ref, g_ref, w_ref, rc_ref, rsa_ref, rsb_ref, pw_ref, ps_ref,
                    q_ref, ks_ref, kw_ref, vs_ref, vw_ref, kc_ref, vc_ref, ga_ref, pm_ref, gm_ref,
                    ubuf):
    si = pl.program_id(1)
    t0 = si * TS_IN

    @pl.when(si == 0)
    def _():
        ubuf[0:POOL_HALO, :] = jnp.zeros((POOL_HALO, POOL_WIDTH), f32)

    h = _rms(x_ref[0], g_ref[...]).astype(bf16)

    def seg(s):
        return _dot(h, w_ref[:, s[0]:s[1]])

    u = seg(SEG_U)
    ubuf[POOL_HALO:, :] = u
    tpos1 = (t0 + 1 + lax.broadcasted_iota(jnp.int32, (TS_IN, 1), 0)).astype(f32)

    def pool_steps():
        for gi, w in enumerate(POOL_WINDOWS):
            sl = slice(gi * POOL_GROUP, (gi + 1) * POOL_GROUP)
            acc = u[:, sl]
            for k in range(1, w):
                acc = acc + ubuf[POOL_HALO - k:POOL_HALO - k + TS_IN, sl]
                yield
            cnt = jnp.minimum(tpos1, float(w))
            pooled = (acc / cnt - u[:, sl]).astype(bf16)
            mixed = _dot(pooled, pw_ref[gi]) * ps_ref[:, sl]
            pm_ref[0, :, sl] = mixed.astype(bf16)
            yield
        ubuf[0:POOL_HALO, :] = ubuf[TS_IN:TS_IN + POOL_HALO, :]

    pool = pool_steps()

    def pool_advance(n):
        for _ in range(n):
            next(pool, None)

    rc, rsa, rsb = rc_ref[...], rsa_ref[...], rsb_ref[...]
    row_t = t0 + lax.broadcasted_iota(jnp.int32, (TS_IN, LANE), 0)
    lane = lax.broadcasted_iota(jnp.int32, (TS_IN, LANE), 1)
    lo = lane < HEAD_DIM

    def head_tiles(x):
        return (x, _swap_halves(x))

    q = seg(SEG_Q)
    pool_advance(4)
    for m, chunk in enumerate(_lane_chunks(q)):
        for half, xh in enumerate(head_tiles(chunk)):
            rot = jnp.where(lo, _rope_lane_tile(xh, rc, rsa, rsb) * Q_SCALE, 0.0)
            q_ref[0, 2 * m + half] = rot.T.astype(bf16)

    onehot = jnp.where(lane - HEAD_DIM == (row_t >> SEL_SHIFT), BIG, 0.0).astype(f32)
    k_both = _lane_chunks(seg((SEG_KS[0], SEG_KW[1])))
    pool_advance(2)
    v_both = _lane_chunks(seg((SEG_VS[0], SEG_VW[1])))
    pool_advance(2)
    c_both = _lane_chunks(seg((SEG_KC[0], SEG_VC[1])))
    pool_advance(2)
    for gg, (ksh, kwh, vsh, vwh) in enumerate(zip(head_tiles(k_both[0]), head_tiles(k_both[1]),
                                                   head_tiles(v_both[0]), head_tiles(v_both[1]))):
        sl = slice(gg * LANE, (gg + 1) * LANE)
        ks_ref[0, :, sl] = jnp.where(lo, _rope_lane_tile(ksh, rc, rsa, rsb), onehot).astype(bf16)
        kw_ref[0, :, sl] = jnp.where(lo, _rope_lane_tile(kwh, rc, rsa, rsb), 0.0).astype(bf16)
        vs_ref[0, gg, 0] = jnp.where(lo, vsh, 1.0).T.astype(bf16)
        vw_t = jnp.where(lo, vwh, 1.0).T.astype(bf16)
        for j in range(TS_IN // TQ):
            vw_ref[0, gg, j] = vw_t[:, j * TQ:(j + 1) * TQ]

    kc_ref[0] = c_both[0]
    vc_ref[0] = c_both[1]
    ga_ref[0] = jax.nn.sigmoid(seg(SEG_GA)).T
    gm_cols = 2 * LANE * 2
    for c0 in range(SEG_GM[0], SEG_GM[1], gm_cols):
        part = seg((c0, c0 + gm_cols))
        pool_advance(5)
        gm_ref[0, :, c0 - SEG_GM[0]:c0 - SEG_GM[0] + gm_cols] = jax.nn.sigmoid(part).astype(bf16)
    for _ in pool:
        pass


def _in_proj(x, g, w, rc, rsa, rsb, pw, ps):
    B, S, D = x.shape
    grid = (B, S // TS_IN)
    tok = lambda width: pl.BlockSpec((1, TS_IN, width), lambda b, s: (b, s, 0))
    const = lambda shape: pl.BlockSpec(shape, lambda b, s: (0,) * len(shape),
                                       pipeline_mode=pl.Buffered(1))
    tab = pl.BlockSpec((TS_IN, LANE), lambda b, s: (s, 0))
    out_shapes = [
        jax.ShapeDtypeStruct((B, N_HEADS, LANE, S), bf16),
        jax.ShapeDtypeStruct((B, S, N_KV * LANE), bf16),
        jax.ShapeDtypeStruct((B, S, N_KV * LANE), bf16),
        jax.ShapeDtypeStruct((B, N_KV, S // TK, LANE, TK), bf16),
        jax.ShapeDtypeStruct((B, N_KV, S // TQ, LANE, TQ), bf16),
        jax.ShapeDtypeStruct((B, S, KV_WIDTH), f32),
        jax.ShapeDtypeStruct((B, S, KV_WIDTH), f32),
        jax.ShapeDtypeStruct((B, LANE, S), f32),
        jax.ShapeDtypeStruct((B, S, POOL_WIDTH), bf16),
        jax.ShapeDtypeStruct((B, S, 2 * D_MODEL), bf16),
    ]
    out_specs = [
        pl.BlockSpec((1, N_HEADS, LANE, TS_IN), lambda b, s: (b, 0, 0, s)),
        tok(N_KV * LANE), tok(N_KV * LANE),
        pl.BlockSpec((1, N_KV, TS_IN // TK, LANE, TK), lambda b, s: (b, 0, s, 0, 0)),
        pl.BlockSpec((1, N_KV, TS_IN // TQ, LANE, TQ), lambda b, s: (b, 0, s, 0, 0)),
        tok(KV_WIDTH), tok(KV_WIDTH),
        pl.BlockSpec((1, LANE, TS_IN), lambda b, s: (b, 0, s)),
        tok(POOL_WIDTH), tok(2 * D_MODEL),
    ]
    return pl.pallas_call(
        _in_proj_kernel,
        grid=grid,
        in_specs=[tok(D), const((1, D)), const((D, W_COLS)), tab, tab, tab,
                  const((len(POOL_WINDOWS), POOL_GROUP, POOL_GROUP)), const((1, POOL_WIDTH))],
        out_specs=out_specs,
        out_shape=out_shapes,
        scratch_shapes=[pltpu.VMEM((POOL_HALO + TS_IN, POOL_WIDTH), f32)],
        compiler_params=pltpu.CompilerParams(
            dimension_semantics=("arbitrary", "arbitrary"), vmem_limit_bytes=VMEM_LIMIT),
        name="in_proj",
    )(x, g, w, rc, rsa, rsb, pw, ps)


def _compress_kernel(kc_ref, vc_ref, pek_ref, pev_ref, kw1_ref, kw2_ref, vw1_ref, vw2_ref,
                     rc_ref, rsa_ref, rsb_ref, kout_ref, vout_ref):
    n_chunk = kc_ref.shape[1] // CMP_STRIDE

    def hidden(src_ref, pe_ref, w1_ref):
        parts = []
        for part in range(CMP_RATIO):
            acc = None
            for tok in range(0, CMP_STRIDE, 2):
                t = part * CMP_STRIDE + tok
                a = jnp.concatenate(
                    [(src_ref[0, pl.ds(tok + u, n_chunk, stride=CMP_STRIDE), :]
                      + pe_ref[t + u:t + u + 1, :]).astype(bf16) for u in range(2)], axis=1)
                d = _dot(a, w1_ref[t // 2])
                acc = d if acc is None else acc + d
            parts.append(acc)
        pre = parts[0] + pltpu.roll(parts[1], n_chunk - 1, axis=0)
        return jax.nn.gelu(pre, approximate=True).astype(bf16)

    lane = lax.broadcasted_iota(jnp.int32, (n_chunk, LANE), 1)
    lo = lane < HEAD_DIM
    kcmp = _dot(hidden(kc_ref, pek_ref, kw1_ref), kw2_ref[...])
    vcmp = _dot(hidden(vc_ref, pev_ref, vw1_ref), vw2_ref[...])
    for gg in range(N_KV):
        sl = slice(gg * LANE, (gg + 1) * LANE)
        kout_ref[0, gg] = _rope_lane_tile(kcmp[:, sl], rc_ref[...], rsa_ref[...], rsb_ref[...]).astype(bf16)
        vout_ref[0, gg] = jnp.where(lo, vcmp[:, sl], 1.0).T.astype(bf16)


def _compress(kc, vc, pek, pev, kw1, kw2, vw1, vw2, rc, rsa, rsb):
    B, S, W = kc.shape
    NC = S // CMP_STRIDE
    const = lambda shape: pl.BlockSpec(shape, lambda b: (0,) * len(shape))
    src = pl.BlockSpec((1, S, W), lambda b: (b, 0, 0))
    return pl.pallas_call(
        _compress_kernel,
        grid=(B,),
        in_specs=[src, src, const(pek.shape), const(pev.shape), const(kw1.shape), const(kw2.shape),
                  const(vw1.shape), const(vw2.shape), const(rc.shape), const(rsa.shape), const(rsb.shape)],
        out_specs=[pl.BlockSpec((1, N_KV, NC, LANE), lambda b: (b, 0, 0, 0)),
                   pl.BlockSpec((1, N_KV, LANE, NC), lambda b: (b, 0, 0, 0))],
        out_shape=[jax.ShapeDtypeStruct((B, N_KV, NC, LANE), bf16),
                   jax.ShapeDtypeStruct((B, N_KV, LANE, NC), bf16)],
        compiler_params=pltpu.CompilerParams(
            dimension_semantics=("arbitrary",), vmem_limit_bytes=VMEM_LIMIT),
        name="compress",
    )(kc, vc, pek, pev, kw1, kw2, vw1, vw2, rc, rsa, rsb)


def _attn_kernel(q_ref, kc_ref, vc_ref, ks_ref, vs_ref, kw_ref, vw_ref, ga_ref, mt_ref, wb_ref, o_ref,
                 qa_sc, s_sc, mx_sc, acc_sc, imp_sc, sel_sc, *, n_cmp, n_sel):
    g = pl.program_id(1)
    i = pl.program_id(2)
    t0 = i * TQ
    cols = GROUP * TQ
    ncp = kc_ref.shape[2]

    tok = t0 + lax.broadcasted_iota(jnp.int32, (1, TQ), 1)
    tok4 = t0 + (lax.broadcasted_iota(jnp.int32, (1, cols), 1) & (TQ - 1))
    head_cols = lambda a, r: a[:, r * TQ:(r + 1) * TQ]

    q_all = jnp.concatenate([q_ref[0, r] for r in range(GROUP)], axis=1)

    kc = kc_ref[0, 0]
    vc = vc_ref[0, 0]
    cpos = lax.broadcasted_iota(jnp.int32, (ncp, 1), 0)
    cvalid = ((cpos * CMP_STRIDE + (CMP_BLOCK - 1)) <= tok) & (cpos < n_cmp)
    cbias = jnp.where(cvalid, 0.0, NEG_INF).astype(f32)
    any_valid = (tok >= CMP_BLOCK - 1).astype(f32)
    s_cmp = _dot(kc, q_all)
    band = WINDOW + TQ
    w0 = pl.multiple_of(jnp.maximum(t0 - WINDOW, 0), TQ)
    wbias = wb_ref[jnp.minimum(i, WINDOW // TQ)]
    kwin = kw_ref[0, pl.ds(w0, band), :]
    vwin = jnp.concatenate([vw_ref[0, 0, w0 // TQ + j] for j in range(band // TQ)], axis=1)
    s_win = _dot(kwin, q_all).astype(bf16)
    psum = jnp.zeros((ncp, TQ), f32)
    e_cmp = []
    inv_cmp = []
    for r in range(GROUP):
        s = head_cols(s_cmp, r) + cbias
        e = jnp.exp2(s - jnp.max(s, axis=0, keepdims=True))
        inv = 1.0 / jnp.sum(e, axis=0, keepdims=True)
        psum = psum + e * inv
        e_cmp.append(e.astype(bf16))
        inv_cmp.append(inv * any_valid)
    pv_cmp = _dot(vc, jnp.concatenate(e_cmp, axis=1))

    mt = mt_ref[...]
    p_hi = psum.astype(bf16)
    rem = psum - p_hi.astype(f32)
    p_mid = rem.astype(bf16)
    p_lo = (rem - p_mid.astype(f32)).astype(bf16)
    imp = _dot(mt, p_hi) + _dot(mt, p_mid) + _dot(mt, p_lo)
    blk = lax.broadcasted_iota(jnp.int32, (n_sel, TQ), 0)
    cur = tok >> SEL_SHIFT
    imp = jnp.where(blk > cur, -FORCE_SCORE, imp)
    imp = jnp.where((blk == 0) | (blk == cur) | (blk == cur - 1), FORCE_SCORE, imp)

    p_win = []
    for r in range(GROUP):
        s = head_cols(s_win, r) + wbias
        p_win.append(jnp.exp2(s - jnp.max(s, axis=0, keepdims=True)))
    pv_win = _dot(vwin, jnp.concatenate(p_win, axis=1))

    imp_sc[...] = imp
    sel_sc[...] = jnp.full((n_sel, TQ), -1.0, f32)
    sub = lax.broadcasted_iota(jnp.int32, (SUBLANE, TQ), 0)
    row_keys = SUBLANE * SEL_BLOCK
    rows_in_play = (t0 + TQ + row_keys - 1) // row_keys
    for n_rows in range(1, n_sel // SUBLANE + 1):
        @pl.when(rows_in_play == n_rows)
        def _(n_rows=n_rows):
            vrows = [imp_sc[v * SUBLANE:(v + 1) * SUBLANE, :] for v in range(n_rows)]
            ahead = [jnp.zeros((SUBLANE, TQ), f32) for _ in range(n_rows)]
            for vj in range(n_rows):
                for sj in range(SUBLANE):
                    rival = jnp.broadcast_to(vrows[vj][sj:sj + 1, :], (SUBLANE, TQ))
                    for v in range(n_rows):
                        if v > vj:
                            ahead[v] = ahead[v] + jnp.where(rival >= vrows[v], 1.0, 0.0)
                        elif v < vj:
                            ahead[v] = ahead[v] + jnp.where(rival > vrows[v], 1.0, 0.0)
                        else:
                            tie = jnp.where(rival == vrows[v], (sub > sj).astype(f32), 0.0)
                            ahead[v] = ahead[v] + jnp.where(rival > vrows[v], 1.0, tie)
            for v in range(n_rows):
                sel_sc[v * SUBLANE:(v + 1) * SUBLANE, :] = jnp.where(
                    ahead[v] < float(min(SEL_TOPK, n_sel)), 0.0, -1.0)

    helper = jnp.concatenate([jnp.zeros((LANE - n_sel, TQ), f32), sel_sc[...]], axis=0).astype(bf16)
    qa_sc[...] = q_all + jnp.concatenate([helper] * GROUP, axis=1)


    n_tiles = (t0 + TQ + TK - 1) // TK
    mx_sc[...] = jnp.full((SUBLANE, cols), M_INIT, f32)

    def score_tiles(kts, ends_sweep):
        slabs = []
        for idx, kt in enumerate(kts):
            k0 = pl.multiple_of(kt * TK, TK)
            s = _dot(ks_ref[0, pl.ds(k0, TK), :], qa_sc[...])
            if ends_sweep and idx == len(kts) - 1:
                kpos = k0 + lax.broadcasted_iota(jnp.int32, (TK, 1), 0)
                s = jnp.where(kpos <= tok4, s, -BIG)
            s_sc[kt] = s.astype(bf16)
            slabs += [s[j * SUBLANE:(j + 1) * SUBLANE] for j in range(TK // SUBLANE)]
        mx_sc[...] = jnp.maximum(mx_sc[...], _tree(jnp.maximum, slabs))

    def weight_tiles(kts, ends_sweep):
        m = mx_sc[0:1, :].astype(bf16)
        total = None
        for kt in kts:
            p = jnp.exp2(s_sc[kt] - m)
            d = _dot(vs_ref[0, 0, kt], p)
            total = d if total is None else total + d
        acc_sc[...] += total

    def sweep(n, tiles_fn):
        def trip(j, carry):
            tiles_fn([SWEEP_UNROLL * j + u for u in range(SWEEP_UNROLL)], False)
            return carry

        trips = (n - 1) // SWEEP_UNROLL
        lax.fori_loop(0, trips, trip, 0)
        done = trips * SWEEP_UNROLL
        for left in range(1, SWEEP_UNROLL + 1):
            @pl.when(n - done == left)
            def _(left=left):
                tiles_fn([done + u for u in range(left)], True)

    sweep(n_tiles, score_tiles)
    mx_sc[...] = jnp.broadcast_to(jnp.max(mx_sc[...], axis=0, keepdims=True), (SUBLANE, cols))
    acc_sc[...] = jnp.zeros((LANE, cols), f32)
    sweep(n_tiles, weight_tiles)
    pv_sel = acc_sc[...]

    outs = []
    for r in range(GROUP):
        def branch(pv):
            x = head_cols(pv, r)
            return x[:HEAD_DIM], x[HEAD_DIM:HEAD_DIM + 1]

        num_c, _ = branch(pv_cmp)
        num_s, den_s = branch(pv_sel)
        num_w, den_w = branch(pv_win)
        gate = lambda br: ga_ref[0, pl.ds(br * N_HEADS + g * GROUP + r, 1), :]
        outs.append((gate(0) * inv_cmp[r]) * num_c + (gate(1) / den_s) * num_s + (gate(2) / den_w) * num_w)
    o_ref[0] = jnp.concatenate(outs, axis=0).T.astype(bf16)


def _attention(q, kcmp, vcmp, ks, vs, kw, vw, ga, mt, wb, n_cmp):
    B, _, _, S = q.shape
    NC = kcmp.shape[2]
    n_sel = S // SEL_BLOCK
    cols = GROUP * TQ
    grid = (B, N_KV, S // TQ)
    keys_g = pl.BlockSpec((1, S, LANE), lambda b, g, i: (b, 0, g))
    const = lambda a: pl.BlockSpec(a.shape, lambda b, g, i: (0,) * a.ndim)
    return pl.pallas_call(
        functools.partial(_attn_kernel, n_cmp=n_cmp, n_sel=n_sel),
        grid=grid,
        in_specs=[
            pl.BlockSpec((1, GROUP, LANE, TQ), lambda b, g, i: (b, g, 0, i)),
            pl.BlockSpec((1, 1, NC, LANE), lambda b, g, i: (b, g, 0, 0)),
            pl.BlockSpec((1, 1, LANE, NC), lambda b, g, i: (b, g, 0, 0)),
            keys_g,
            pl.BlockSpec((1, 1, S // TK, LANE, TK), lambda b, g, i: (b, g, 0, 0, 0)),
            keys_g,
            pl.BlockSpec((1, 1, S // TQ, LANE, TQ), lambda b, g, i: (b, g, 0, 0, 0)),
            pl.BlockSpec((1, LANE, TQ), lambda b, g, i: (b, 0, i)),
            const(mt), const(wb),
        ],
        out_specs=pl.BlockSpec((1, TQ, GROUP * HEAD_DIM), lambda b, g, i: (b, i, g)),
        out_shape=jax.ShapeDtypeStruct((B, S, ATTN_WIDTH), bf16),
        scratch_shapes=[pltpu.VMEM((LANE, cols), bf16),
                        pltpu.VMEM((S // TK, TK, cols), bf16),
                        pltpu.VMEM((SUBLANE, cols), f32),
                        pltpu.VMEM((LANE, cols), f32),
                        pltpu.VMEM((n_sel, TQ), f32),
                        pltpu.VMEM((n_sel, TQ), f32)],
        compiler_params=pltpu.CompilerParams(
            dimension_semantics=("arbitrary", "arbitrary", "arbitrary"),
            vmem_limit_bytes=VMEM_LIMIT),
        name="attention",
    )(q, kcmp, vcmp, ks, vs, kw, vw, ga, mt, wb)


def _out_mlp_kernel(x_ref, ao_ref, pm_ref, gm_ref, wba_ref, wbp_ref, wo_ref, gn_ref,
                    w1_ref, w2_ref, gf_ref, o_ref):
    a = _dot(ao_ref[...], wba_ref[...])
    b = _dot(pm_ref[...], wbp_ref[...])
    gm = gm_ref[...].astype(f32)
    merged = (gm[:, :D_MODEL] * a + gm[:, D_MODEL:] * b).astype(bf16)
    x1 = x_ref[...] + _dot(merged, wo_ref[...])
    h = _rms(x1, gn_ref[...]).astype(bf16)
    f = jnp.square(jnp.maximum(_dot(h, w1_ref[...]), 0.0)).astype(bf16)
    x2 = x1 + _dot(f, w2_ref[...])
    o_ref[...] = _rms(x2, gf_ref[...])


def _out_mlp(x2d, ao, pm, gm, wba, wbp, wo, gn, w1, w2, gf):
    T, D = x2d.shape
    tok = lambda width: pl.BlockSpec((TS_OUT, width), lambda t: (t, 0))
    const = lambda arr: pl.BlockSpec(arr.shape, lambda t: (0, 0), pipeline_mode=pl.Buffered(1))
    return pl.pallas_call(
        _out_mlp_kernel,
        grid=(T // TS_OUT,),
        in_specs=[tok(D), tok(ATTN_WIDTH), tok(POOL_WIDTH), tok(2 * D_MODEL),
                  const(wba), const(wbp), const(wo), const(gn), const(w1), const(w2), const(gf)],
        out_specs=tok(D),
        out_shape=jax.ShapeDtypeStruct((T, D), f32),
        compiler_params=pltpu.CompilerParams(
            dimension_semantics=("arbitrary",), vmem_limit_bytes=VMEM_LIMIT),
        name="out_mlp",
    )(x2d, ao, pm, gm, wba, wbp, wo, gn, w1, w2, gf)


def _rope_tables(pos):
    inv = ROPE_THETA ** (-jnp.arange(0, ROT_DIM, 2, dtype=f32) / ROT_DIM)
    ang = pos.astype(f32)[:, None] * inv
    cos, sin = jnp.cos(ang), jnp.sin(ang)
    P = pos.shape[0]
    ones = jnp.ones((P, LANE - ROT_DIM), f32)
    zeros = jnp.zeros((P, LANE - ROT_HALF), f32)
    c = jnp.concatenate([cos, cos, ones], axis=1)
    sa = jnp.concatenate([-sin, zeros], axis=1)
    sb = jnp.concatenate([jnp.zeros((P, ROT_HALF), f32), sin,
                          jnp.zeros((P, LANE - ROT_DIM), f32)], axis=1)
    return c, sa, sb


def _cmp_to_sel_t(S):
    n_cmp = (S - CMP_BLOCK) // CMP_STRIDE + 1
    n_sel = S // SEL_BLOCK
    cs = np.arange(n_cmp)[:, None] * CMP_STRIDE
    js = np.arange(n_sel)[None, :] * SEL_BLOCK
    ov = np.clip(np.minimum(cs + CMP_BLOCK, js + SEL_BLOCK) - np.maximum(cs, js), 0, None)
    m = np.zeros((S // CMP_STRIDE, n_sel), np.float32)
    m[:n_cmp] = ov / CMP_STRIDE
    return jnp.asarray(m.T, dtype=bf16), n_cmp


def _window_bias():
    band = WINDOW + TQ
    k = np.arange(band)[:, None]
    r = np.arange(TQ)[None, :]
    cases = [k <= r + case * TQ for case in range(WINDOW // TQ)]
    cases.append((k > r) & (k <= r + WINDOW))
    return jnp.asarray(np.where(np.stack(cases), 0.0, -BIG), dtype=bf16)


def _layout_w_in(w):
    D = w.shape[0]
    w = w.astype(bf16)
    o = 0
    wq = w[:, o:o + ATTN_WIDTH]; o += ATTN_WIDTH
    kcw, vcw, ksw, vsw, kww, vww = [w[:, o + j * KV_WIDTH:o + (j + 1) * KV_WIDTH] for j in range(6)]
    o += 6 * KV_WIDTH
    gaw = w[:, o:o + N_BRANCH * N_HEADS]; o += N_BRANCH * N_HEADS
    uw = w[:, o:o + POOL_WIDTH]; o += POOL_WIDTH
    gmw = w[:, o:]
    gaw = jnp.concatenate([gaw, jnp.zeros((D, LANE - N_BRANCH * N_HEADS), w.dtype)], axis=1)
    out = jnp.concatenate([wq, ksw, kww, vsw, vww, kcw, vcw, uw, gmw, gaw], axis=1)
    assert out.shape[1] == W_COLS
    return out


def _block_diag_kv(m):
    z = jnp.zeros_like(m)
    return jnp.concatenate([jnp.concatenate([m, z], axis=-1), jnp.concatenate([z, m], axis=-1)], axis=-2)


def kernel(x, norm_mix, w_in, cmp_pe_k, cmp_pe_v, cmp_k_w1, cmp_k_w2, cmp_v_w1, cmp_v_w2,
           w_branch_attn, pool_w, pool_scale, w_branch_pool, w_out, norm_mlp, w_ff1, w_ff2,
           norm_final):
    B, S, D = x.shape
    assert norm_mix.shape[0] == 1, "single-layer block: the final norm is fused into out_mlp"
    assert S % TS_IN == 0 and S % TQ == 0 and (B * S) % TS_OUT == 0 and TS_IN == TK
    assert TK % TQ == 0 and S % TK == 0 and WINDOW % TQ == 0 and N_KV == 2
    assert SWEEP_UNROLL & (SWEEP_UNROLL - 1) == 0
    n_chunk = S // CMP_STRIDE
    rc, rsa, rsb = _rope_tables(jnp.arange(S))
    crc, crsa, crsb = _rope_tables(jnp.arange(n_chunk) * CMP_STRIDE + CMP_BLOCK - 1)
    mt, n_cmp = _cmp_to_sel_t(S)

    (q, ks, kw, vs, vw, kc, vc, ga, pm, gm) = _in_proj(
        x, norm_mix[0][None, :], _layout_w_in(w_in[0]), rc, rsa, rsb,
        pool_w[0].astype(bf16), pool_scale[0][None, :])

    def first_layer(w1):
        per_token = _block_diag_kv(w1.reshape(CMP_BLOCK, HEAD_DIM, CMP_HIDDEN)).astype(bf16)
        return per_token.reshape(CMP_BLOCK // 2, 2 * N_KV * HEAD_DIM, N_KV * CMP_HIDDEN)

    def second_layer(w2):
        return _block_diag_kv(jnp.concatenate([w2, jnp.zeros_like(w2)], axis=1)).astype(bf16)

    kcmp, vcmp = _compress(
        kc, vc, jnp.tile(cmp_pe_k[0], (1, N_KV)), jnp.tile(cmp_pe_v[0], (1, N_KV)),
        first_layer(cmp_k_w1[0]), second_layer(cmp_k_w2[0]),
        first_layer(cmp_v_w1[0]), second_layer(cmp_v_w2[0]), crc, crsa, crsb)

    ao = _attention(q, kcmp, vcmp, ks, vs, kw, vw, ga, mt, _window_bias(), n_cmp)

    y = _out_mlp(x.reshape(B * S, D), ao.reshape(B * S, ATTN_WIDTH), pm.reshape(B * S, POOL_WIDTH),
                 gm.reshape(B * S, 2 * D_MODEL), w_branch_attn[0].astype(bf16),
                 w_branch_pool[0].astype(bf16), w_out[0].astype(bf16), norm_mlp[0][None, :],
                 w_ff1[0].astype(bf16), w_ff2[0].astype(bf16), norm_final[None, :])
    return y.reshape(B, S, D)
```

```python
import functools
import math

import jax
import jax.numpy as jnp
import numpy as np
from jax import lax
from jax.experimental import pallas as pl
from jax.experimental.pallas import tpu as pltpu

f32 = jnp.float32
bf16 = jnp.bfloat16

D_MODEL = 1024
N_HEADS = 8
HEAD_DIM = 64
N_KV = 2
GROUP = N_HEADS // N_KV
ROT_DIM = HEAD_DIM // 4
ROT_HALF = ROT_DIM // 2
ROPE_THETA = 500000.0
CMP_BLOCK = 32
CMP_STRIDE = 16
CMP_RATIO = CMP_BLOCK // CMP_STRIDE
CMP_HIDDEN = 4 * HEAD_DIM
SEL_BLOCK = 64
SEL_SHIFT = 6
SEL_TOPK = 16
WINDOW = 512
N_BRANCH = 3
ATTN_WIDTH = N_HEADS * HEAD_DIM
KV_WIDTH = N_KV * HEAD_DIM
POOL_WIDTH = 512
POOL_WINDOWS = (2, 4, 8, 16)
POOL_GROUP = POOL_WIDTH // len(POOL_WINDOWS)
POOL_HALO = 16
D_FF = 4 * D_MODEL
EPS = 1e-6
NEG_INF = -1e30
FORCE_SCORE = 1e4
BIG = 2.0 ** 100
M_INIT = -3.0e38
Q_SCALE = HEAD_DIM ** -0.5 * math.log2(math.e)

LANE = 128
SUBLANE = 8
VMEM_LIMIT = 56 * 1024 * 1024

SEG_Q = (0, ATTN_WIDTH)
SEG_KS = (SEG_Q[1], SEG_Q[1] + KV_WIDTH)
SEG_KW = (SEG_KS[1], SEG_KS[1] + KV_WIDTH)
SEG_VS = (SEG_KW[1], SEG_KW[1] + KV_WIDTH)
SEG_VW = (SEG_VS[1], SEG_VS[1] + KV_WIDTH)
SEG_KC = (SEG_VW[1], SEG_VW[1] + KV_WIDTH)
SEG_VC = (SEG_KC[1], SEG_KC[1] + KV_WIDTH)
SEG_U = (SEG_VC[1], SEG_VC[1] + POOL_WIDTH)
SEG_GM = (SEG_U[1], SEG_U[1] + 2 * D_MODEL)
SEG_GA = (SEG_GM[1], SEG_GM[1] + LANE)
W_COLS = SEG_GA[1]

TS_IN = 512
TQ = 256
TK = 256
SWEEP_UNROLL = 8
TS_OUT = 512


def _dot(a, b):
    return jnp.dot(a, b, preferred_element_type=f32)


def _swap_halves(x):
    return pltpu.roll(x, HEAD_DIM, axis=1)


def _rope_lane_tile(x, c, sa, sb):
    return (x * c + pltpu.roll(x, LANE - ROT_HALF, axis=1) * sa
            + pltpu.roll(x, ROT_HALF, axis=1) * sb)


def _rms(x, g):
    ms = jnp.mean(x * x, axis=-1, keepdims=True)
    return x * lax.rsqrt(ms + EPS) * g


def _lane_chunks(s):
    return [s[:, c * LANE:(c + 1) * LANE] for c in range(s.shape[1] // LANE)]


def _tree(fn, items):
    items = list(items)
    while len(items) > 1:
        items = [fn(items[j], items[j + 1]) if j + 1 < len(items) else items[j]
                 for j in range(0, len(items), 2)]
    return items[0]


def _in_proj_kernel(x_ref, g_ref, w_ref, rc_ref, rsa_ref, rsb_ref, pw_ref, ps_ref,
                    q_ref, ks_ref, kw_ref, vs_ref, vw_ref, kc_ref, vc_ref, ga_ref, pm_ref, gm_ref,
                    ubuf):
    si = pl.program_id(1)
    t0 = si * TS_IN

    @pl.when(si == 0)
    def _():
        ubuf[0:POOL_HALO, :] = jnp.zeros((POOL_HALO, POOL_WIDTH), f32)

    h = _rms(x_ref[0], g_ref[...]).astype(bf16)

    def seg(s):
        return _dot(h, w_ref[:, s[0]:s[1]])

    u = seg(SEG_U)
    ubuf[POOL_HALO:, :] = u
    tpos1 = (t0 + 1 + lax.broadcasted_iota(jnp.int32, (TS_IN, 1), 0)).astype(f32)

    def pool_steps():
        for gi, w in enumerate(POOL_WINDOWS):
            sl = slice(gi * POOL_GROUP, (gi + 1) * POOL_GROUP)
            acc = u[:, sl]
            for k in range(1, w):
                acc = acc + ubuf[POOL_HALO - k:POOL_HALO - k + TS_IN, sl]
                yield
            cnt = jnp.minimum(tpos1, float(w))
            pooled = (acc / cnt - u[:, sl]).astype(bf16)
            mixed = _dot(pooled, pw_ref[gi]) * ps_ref[:, sl]
            pm_ref[0, :, sl] = mixed.astype(bf16)
            yield
        ubuf[0:POOL_HALO, :] = ubuf[TS_IN:TS_IN + POOL_HALO, :]

    pool = pool_steps()

    def pool_advance(n):
        for _ in range(n):
            next(pool, None)

    rc, rsa, rsb = rc_ref[...], rsa_ref[...], rsb_ref[...]
    row_t = t0 + lax.broadcasted_iota(jnp.int32, (TS_IN, LANE), 0)
    lane = lax.broadcasted_iota(jnp.int32, (TS_IN, LANE), 1)
    lo = lane < HEAD_DIM

    def head_tiles(x):
        return (x, _swap_halves(x))

    q = seg(SEG_Q)
    pool_advance(4)
    for m, chunk in enumerate(_lane_chunks(q)):
        for half, xh in enumerate(head_tiles(chunk)):
            rot = jnp.where(lo, _rope_lane_tile(xh, rc, rsa, rsb) * Q_SCALE, 0.0)
            q_ref[0, 2 * m + half] = rot.T.astype(bf16)

    onehot = jnp.where(lane - HEAD_DIM == (row_t >> SEL_SHIFT), BIG, 0.0).astype(f32)
    k_both = _lane_chunks(seg((SEG_KS[0], SEG_KW[1])))
    pool_advance(2)
    v_both = _lane_chunks(seg((SEG_VS[0], SEG_VW[1])))
    pool_advance(2)
    c_both = _lane_chunks(seg((SEG_KC[0], SEG_VC[1])))
    pool_advance(2)
    for gg, (ksh, kwh, vsh, vwh) in enumerate(zip(head_tiles(k_both[0]), head_tiles(k_both[1]),
                                                   head_tiles(v_both[0]), head_tiles(v_both[1]))):
        sl = slice(gg * LANE, (gg + 1) * LANE)
        ks_ref[0, :, sl] = jnp.where(lo, _rope_lane_tile(ksh, rc, rsa, rsb), onehot).astype(bf16)
        kw_ref[0, :, sl] = jnp.where(lo, _rope_lane_tile(kwh, rc, rsa, rsb), 0.0).astype(bf16)
        vs_t = jnp.where(lo, vsh, 1.0).T.astype(bf16)
        for j in range(TS_IN // TK):
            vs_ref[0, gg, j] = vs_t[:, j * TK:(j + 1) * TK]
        vw_t = jnp.where(lo, vwh, 1.0).T.astype(bf16)
        for j in range(TS_IN // TQ):
            vw_ref[0, gg, j] = vw_t[:, j * TQ:(j + 1) * TQ]

    kc_ref[0] = c_both[0]
    vc_ref[0] = c_both[1]
    ga_ref[0] = jax.nn.sigmoid(seg(SEG_GA)).T
    gm_cols = 2 * LANE * 2
    for c0 in range(SEG_GM[0], SEG_GM[1], gm_cols):
        part = seg((c0, c0 + gm_cols))
        pool_advance(5)
        gm_ref[0, :, c0 - SEG_GM[0]:c0 - SEG_GM[0] + gm_cols] = jax.nn.sigmoid(part).astype(bf16)
    for _ in pool:
        pass


def _in_proj(x, g, w, rc, rsa, rsb, pw, ps):
    B, S, D = x.shape
    grid = (B, S // TS_IN)
    tok = lambda width: pl.BlockSpec((1, TS_IN, width), lambda b, s: (b, s, 0))
    const = lambda shape: pl.BlockSpec(shape, lambda b, s: (0,) * len(shape),
                                       pipeline_mode=pl.Buffered(1))
    tab = pl.BlockSpec((TS_IN, LANE), lambda b, s: (s, 0))
    out_shapes = [
        jax.ShapeDtypeStruct((B, N_HEADS, LANE, S), bf16),
        jax.ShapeDtypeStruct((B, S, N_KV * LANE), bf16),
        jax.ShapeDtypeStruct((B, S, N_KV * LANE), bf16),
        jax.ShapeDtypeStruct((B, N_KV, S // TK, LANE, TK), bf16),
        jax.ShapeDtypeStruct((B, N_KV, S // TQ, LANE, TQ), bf16),
        jax.ShapeDtypeStruct((B, S, KV_WIDTH), f32),
        jax.ShapeDtypeStruct((B, S, KV_WIDTH), f32),
        jax.ShapeDtypeStruct((B, LANE, S), f32),
        jax.ShapeDtypeStruct((B, S, POOL_WIDTH), bf16),
        jax.ShapeDtypeStruct((B, S, 2 * D_MODEL), bf16),
    ]
    out_specs = [
        pl.BlockSpec((1, N_HEADS, LANE, TS_IN), lambda b, s: (b, 0, 0, s)),
        tok(N_KV * LANE), tok(N_KV * LANE),
        pl.BlockSpec((1, N_KV, TS_IN // TK, LANE, TK), lambda b, s: (b, 0, s, 0, 0)),
        pl.BlockSpec((1, N_KV, TS_IN // TQ, LANE, TQ), lambda b, s: (b, 0, s, 0, 0)),
        tok(KV_WIDTH), tok(KV_WIDTH),
        pl.BlockSpec((1, LANE, TS_IN), lambda b, s: (b, 0, s)),
        tok(POOL_WIDTH), tok(2 * D_MODEL),
    ]
    return pl.pallas_call(
        _in_proj_kernel,
        grid=grid,
        in_specs=[tok(D), const((1, D)), const((D, W_COLS)), tab, tab, tab,
                  const((len(POOL_WINDOWS), POOL_GROUP, POOL_GROUP)), const((1, POOL_WIDTH))],
        out_specs=out_specs,
        out_shape=out_shapes,
        scratch_shapes=[pltpu.VMEM((POOL_HALO + TS_IN, POOL_WIDTH), f32)],
        compiler_params=pltpu.CompilerParams(
            dimension_semantics=("arbitrary", "arbitrary"), vmem_limit_bytes=VMEM_LIMIT),
        name="in_proj",
    )(x, g, w, rc, rsa, rsb, pw, ps)


def _compress_kernel(kc_ref, vc_ref, pek_ref, pev_ref, kw1_ref, kw2_ref, vw1_ref, vw2_ref,
                     rc_ref, rsa_ref, rsb_ref, kout_ref, vout_ref):
    n_chunk = kc_ref.shape[1] // CMP_STRIDE

    def hidden(src_ref, pe_ref, w1_ref):
        parts = []
        for part in range(CMP_RATIO):
            acc = None
            for tok in range(0, CMP_STRIDE, 2):
                t = part * CMP_STRIDE + tok
                a = jnp.concatenate(
                    [(src_ref[0, pl.ds(tok + u, n_chunk, stride=CMP_STRIDE), :]
                      + pe_ref[t + u:t + u + 1, :]).astype(bf16) for u in range(2)], axis=1)
                d = _dot(a, w1_ref[t // 2])
                acc = d if acc is None else acc + d
            parts.append(acc)
        pre = parts[0] + pltpu.roll(parts[1], n_chunk - 1, axis=0)
        return jax.nn.gelu(pre, approximate=True).astype(bf16)

    lane = lax.broadcasted_iota(jnp.int32, (n_chunk, LANE), 1)
    lo = lane < HEAD_DIM
    kcmp = _dot(hidden(kc_ref, pek_ref, kw1_ref), kw2_ref[...])
    vcmp = _dot(hidden(vc_ref, pev_ref, vw1_ref), vw2_ref[...])
    for gg in range(N_KV):
        sl = slice(gg * LANE, (gg + 1) * LANE)
        kout_ref[0, gg] = _rope_lane_tile(kcmp[:, sl], rc_ref[...], rsa_ref[...], rsb_ref[...]).astype(bf16)
        vout_ref[0, gg] = jnp.where(lo, vcmp[:, sl], 1.0).T.astype(bf16)


def _compress(kc, vc, pek, pev, kw1, kw2, vw1, vw2, rc, rsa, rsb):
    B, S, W = kc.shape
    NC = S // CMP_STRIDE
    const = lambda shape: pl.BlockSpec(shape, lambda b: (0,) * len(shape))
    src = pl.BlockSpec((1, S, W), lambda b: (b, 0, 0))
    return pl.pallas_call(
        _compress_kernel,
        grid=(B,),
        in_specs=[src, src, const(pek.shape), const(pev.shape), const(kw1.shape), const(kw2.shape),
                  const(vw1.shape), const(vw2.shape), const(rc.shape), const(rsa.shape), const(rsb.shape)],
        out_specs=[pl.BlockSpec((1, N_KV, NC, LANE), lambda b: (b, 0, 0, 0)),
                   pl.BlockSpec((1, N_KV, LANE, NC), lambda b: (b, 0, 0, 0))],
        out_shape=[jax.ShapeDtypeStruct((B, N_KV, NC, LANE), bf16),
                   jax.ShapeDtypeStruct((B, N_KV, LANE, NC), bf16)],
        compiler_params=pltpu.CompilerParams(
            dimension_semantics=("arbitrary",), vmem_limit_bytes=VMEM_LIMIT),
        name="compress",
    )(kc, vc, pek, pev, kw1, kw2, vw1, vw2, rc, rsa, rsb)


def _attn_kernel(q_ref, kc_ref, vc_ref, ks_ref, vs_ref, kw_ref, vw_ref, ga_ref, mt_ref, wb_ref, o_ref,
                 qa_sc, s_sc, mx_sc, acc_sc, imp_sc, sel_sc, *, n_cmp, n_sel):
    g = pl.program_id(1)
    i = pl.program_id(2)
    t0 = i * TQ
    cols = GROUP * TQ
    ncp = kc_ref.shape[2]

    tok = t0 + lax.broadcasted_iota(jnp.int32, (1, TQ), 1)
    tok4 = t0 + (lax.broadcasted_iota(jnp.int32, (1, cols), 1) & (TQ - 1))
    head_cols = lambda a, r: a[:, r * TQ:(r + 1) * TQ]

    q_all = jnp.concatenate([q_ref[0, r] for r in range(GROUP)], axis=1)

    kc = kc_ref[0, 0]
    vc = vc_ref[0, 0]
    cpos = lax.broadcasted_iota(jnp.int32, (ncp, 1), 0)
    cvalid = ((cpos * CMP_STRIDE + (CMP_BLOCK - 1)) <= tok) & (cpos < n_cmp)
    cbias = jnp.where(cvalid, 0.0, NEG_INF).astype(f32)
    any_valid = (tok >= CMP_BLOCK - 1).astype(f32)
    s_cmp = _dot(kc, q_all)
    band = WINDOW + TQ
    w0 = pl.multiple_of(jnp.maximum(t0 - WINDOW, 0), TQ)
    wbias = wb_ref[jnp.minimum(i, WINDOW // TQ)]
    kwin = kw_ref[0, pl.ds(w0, band), :]
    vwin = jnp.concatenate([vw_ref[0, 0, w0 // TQ + j] for j in range(band // TQ)], axis=1)
    s_win = _dot(kwin, q_all).astype(bf16)
    psum = jnp.zeros((ncp, TQ), f32)
    e_cmp = []
    inv_cmp = []
    for r in range(GROUP):
        s = head_cols(s_cmp, r) + cbias
        e = jnp.exp2(s - jnp.max(s, axis=0, keepdims=True))
        inv = 1.0 / jnp.sum(e, axis=0, keepdims=True)
        psum = psum + e * inv
        e_cmp.append(e.astype(bf16))
        inv_cmp.append(inv * any_valid)
    pv_cmp = _dot(vc, jnp.concatenate(e_cmp, axis=1))

    mt = mt_ref[...]
    p_hi = psum.astype(bf16)
    rem = psum - p_hi.astype(f32)
    p_mid = rem.astype(bf16)
    p_lo = (rem - p_mid.astype(f32)).astype(bf16)
    imp = _dot(mt, p_hi) + _dot(mt, p_mid) + _dot(mt, p_lo)
    blk = lax.broadcasted_iota(jnp.int32, (n_sel, TQ), 0)
    cur = tok >> SEL_SHIFT
    imp = jnp.where(blk > cur, -FORCE_SCORE, imp)
    imp = jnp.where((blk == 0) | (blk == cur) | (blk == cur - 1), FORCE_SCORE, imp)

    p_win = []
    for r in range(GROUP):
        s = head_cols(s_win, r) + wbias
        p_win.append(jnp.exp2(s - jnp.max(s, axis=0, keepdims=True)))
    pv_win = _dot(vwin, jnp.concatenate(p_win, axis=1))

    imp_sc[...] = imp
    sel_sc[...] = jnp.full((n_sel, TQ), -1.0, f32)
    sub = lax.broadcasted_iota(jnp.int32, (SUBLANE, TQ), 0)
    row_keys = SUBLANE * SEL_BLOCK
    rows_in_play = (t0 + TQ + row_keys - 1) // row_keys
    for n_rows in range(1, n_sel // SUBLANE + 1):
        @pl.when(rows_in_play == n_rows)
        def _(n_rows=n_rows):
            vrows = [imp_sc[v * SUBLANE:(v + 1) * SUBLANE, :] for v in range(n_rows)]
            ahead = [jnp.zeros((SUBLANE, TQ), f32) for _ in range(n_rows)]
            for vj in range(n_rows):
                for sj in range(SUBLANE):
                    rival = jnp.broadcast_to(vrows[vj][sj:sj + 1, :], (SUBLANE, TQ))
                    for v in range(n_rows):
                        if v > vj:
                            ahead[v] = ahead[v] + jnp.where(rival >= vrows[v], 1.0, 0.0)
                        elif v < vj:
                            ahead[v] = ahead[v] + jnp.where(rival > vrows[v], 1.0, 0.0)
                        else:
                            tie = jnp.where(rival == vrows[v], (sub > sj).astype(f32), 0.0)
                            ahead[v] = ahead[v] + jnp.where(rival > vrows[v], 1.0, tie)
            for v in range(n_rows):
                sel_sc[v * SUBLANE:(v + 1) * SUBLANE, :] = jnp.where(
                    ahead[v] < float(min(SEL_TOPK, n_sel)), 0.0, -1.0)

    helper = jnp.concatenate([jnp.zeros((LANE - n_sel, TQ), f32), sel_sc[...]], axis=0).astype(bf16)
    qa_sc[...] = q_all + jnp.concatenate([helper] * GROUP, axis=1)


    n_tiles = (t0 + TQ + TK - 1) // TK
    mx_sc[...] = jnp.full((SUBLANE, cols), M_INIT, f32)

    def score_tiles(kts, ends_sweep):
        slabs = []
        for idx, kt in enumerate(kts):
            k0 = pl.multiple_of(kt * TK, TK)
            s = _dot(ks_ref[0, pl.ds(k0, TK), :], qa_sc[...])
            if ends_sweep and idx == len(kts) - 1:
                kpos = k0 + lax.broadcasted_iota(jnp.int32, (TK, 1), 0)
                s = jnp.where(kpos <= tok4, s, -BIG)
            s_sc[kt] = s.astype(bf16)
            slabs += [s[j * SUBLANE:(j + 1) * SUBLANE] for j in range(TK // SUBLANE)]
        mx_sc[...] = jnp.maximum(mx_sc[...], _tree(jnp.maximum, slabs))

    def weight_tiles(kts, ends_sweep):
        m = mx_sc[0:1, :].astype(bf16)
        total = None
        for kt in kts:
            p = jnp.exp2(s_sc[kt] - m)
            d = _dot(vs_ref[0, 0, kt], p)
            total = d if total is None else total + d
        acc_sc[...] += total

    def sweep(n, tiles_fn):
        def trip(j, carry):
            tiles_fn([SWEEP_UNROLL * j + u for u in range(SWEEP_UNROLL)], False)
            return carry

        trips = (n - 1) // SWEEP_UNROLL
        lax.fori_loop(0, trips, trip, 0)
        done = trips * SWEEP_UNROLL
        for left in range(1, SWEEP_UNROLL + 1):
            @pl.when(n - done == left)
            def _(left=left):
                tiles_fn([done + u for u in range(left)], True)

    sweep(n_tiles, score_tiles)
    mx_sc[...] = jnp.broadcast_to(jnp.max(mx_sc[...], axis=0, keepdims=True), (SUBLANE, cols))
    acc_sc[...] = jnp.zeros((LANE, cols), f32)
    sweep(n_tiles, weight_tiles)
    pv_sel = acc_sc[...]

    outs = []
    for r in range(GROUP):
        def branch(pv):
            x = head_cols(pv, r)
            return x[:HEAD_DIM], x[HEAD_DIM:HEAD_DIM + 1]

        num_c, _ = branch(pv_cmp)
        num_s, den_s = branch(pv_sel)
        num_w, den_w = branch(pv_win)
        gate = lambda br: ga_ref[0, pl.ds(br * N_HEADS + g * GROUP + r, 1), :]
        outs.append((gate(0) * inv_cmp[r]) * num_c + (gate(1) / den_s) * num_s + (gate(2) / den_w) * num_w)
    o_ref[0] = jnp.concatenate(outs, axis=0).T.astype(bf16)


def _attention(q, kcmp, vcmp, ks, vs, kw, vw, ga, mt, wb, n_cmp):
    B, _, _, S = q.shape
    NC = kcmp.shape[2]
    n_sel = S // SEL_BLOCK
    cols = GROUP * TQ
    grid = (B, N_KV, S // TQ)
    keys_g = pl.BlockSpec((1, S, LANE), lambda b, g, i: (b, 0, g))
    const = lambda a: pl.BlockSpec(a.shape, lambda b, g, i: (0,) * a.ndim)
    return pl.pallas_call(
        functools.partial(_attn_kernel, n_cmp=n_cmp, n_sel=n_sel),
        grid=grid,
        in_specs=[
            pl.BlockSpec((1, GROUP, LANE, TQ), lambda b, g, i: (b, g, 0, i)),
            pl.BlockSpec((1, 1, NC, LANE), lambda b, g, i: (b, g, 0, 0)),
            pl.BlockSpec((1, 1, LANE, NC), lambda b, g, i: (b, g, 0, 0)),
            keys_g,
            pl.BlockSpec((1, 1, S // TK, LANE, TK), lambda b, g, i: (b, g, 0, 0, 0)),
            keys_g,
            pl.BlockSpec((1, 1, S // TQ, LANE, TQ), lambda b, g, i: (b, g, 0, 0, 0)),
            pl.BlockSpec((1, LANE, TQ), lambda b, g, i: (b, 0, i)),
            const(mt), const(wb),
        ],
        out_specs=pl.BlockSpec((1, TQ, GROUP * HEAD_DIM), lambda b, g, i: (b, i, g)),
        out_shape=jax.ShapeDtypeStruct((B, S, ATTN_WIDTH), bf16),
        scratch_shapes=[pltpu.VMEM((LANE, cols), bf16),
                        pltpu.VMEM((S // TK, TK, cols), bf16),
                        pltpu.VMEM((SUBLANE, cols), f32),
                        pltpu.VMEM((LANE, cols), f32),
                        pltpu.VMEM((n_sel, TQ), f32),
                        pltpu.VMEM((n_sel, TQ), f32)],
        compiler_params=pltpu.CompilerParams(
            dimension_semantics=("arbitrary", "arbitrary", "arbitrary"),
            vmem_limit_bytes=VMEM_LIMIT),
        name="attention",
    )(q, kcmp, vcmp, ks, vs, kw, vw, ga, mt, wb)


def _out_mlp_kernel(x_ref, ao_ref, pm_ref, gm_ref, wba_ref, wbp_ref, wo_ref, gn_ref,
                    w1_ref, w2_ref, gf_ref, o_ref):
    a = _dot(ao_ref[...], wba_ref[...])
    b = _dot(pm_ref[...], wbp_ref[...])
    gm = gm_ref[...].astype(f32)
    merged = (gm[:, :D_MODEL] * a + gm[:, D_MODEL:] * b).astype(bf16)
    x1 = x_ref[...] + _dot(merged, wo_ref[...])
    h = _rms(x1, gn_ref[...]).astype(bf16)
    f = jnp.square(jnp.maximum(_dot(h, w1_ref[...]), 0.0)).astype(bf16)
    x2 = x1 + _dot(f, w2_ref[...])
    o_ref[...] = _rms(x2, gf_ref[...])


def _out_mlp(x2d, ao, pm, gm, wba, wbp, wo, gn, w1, w2, gf):
    T, D = x2d.shape
    tok = lambda width: pl.BlockSpec((TS_OUT, width), lambda t: (t, 0))
    const = lambda arr: pl.BlockSpec(arr.shape, lambda t: (0, 0), pipeline_mode=pl.Buffered(1))
    return pl.pallas_call(
        _out_mlp_kernel,
        grid=(T // TS_OUT,),
        in_specs=[tok(D), tok(ATTN_WIDTH), tok(POOL_WIDTH), tok(2 * D_MODEL),
                  const(wba), const(wbp), const(wo), const(gn), const(w1), const(w2), const(gf)],
        out_specs=tok(D),
        out_shape=jax.ShapeDtypeStruct((T, D), f32),
        compiler_params=pltpu.CompilerParams(
            dimension_semantics=("arbitrary",), vmem_limit_bytes=VMEM_LIMIT),
        name="out_mlp",
    )(x2d, ao, pm, gm, wba, wbp, wo, gn, w1, w2, gf)


def _rope_tables(pos):
    inv = ROPE_THETA ** (-jnp.arange(0, ROT_DIM, 2, dtype=f32) / ROT_DIM)
    ang = pos.astype(f32)[:, None] * inv
    cos, sin = jnp.cos(ang), jnp.sin(ang)
    P = pos.shape[0]
    ones = jnp.ones((P, LANE - ROT_DIM), f32)
    zeros = jnp.zeros((P, LANE - ROT_HALF), f32)
    c = jnp.concatenate([cos, cos, ones], axis=1)
    sa = jnp.concatenate([-sin, zeros], axis=1)
    sb = jnp.concatenate([jnp.zeros((P, ROT_HALF), f32), sin,
                          jnp.zeros((P, LANE - ROT_DIM), f32)], axis=1)
    return c, sa, sb


def _cmp_to_sel_t(S):
    n_cmp = (S - CMP_BLOCK) // CMP_STRIDE + 1
    n_sel = S // SEL_BLOCK
    cs = np.arange(n_cmp)[:, None] * CMP_STRIDE
    js = np.arange(n_sel)[None, :] * SEL_BLOCK
    ov = np.clip(np.minimum(cs + CMP_BLOCK, js + SEL_BLOCK) - np.maximum(cs, js), 0, None)
    m = np.zeros((S // CMP_STRIDE, n_sel), np.float32)
    m[:n_cmp] = ov / CMP_STRIDE
    return jnp.asarray(m.T, dtype=bf16), n_cmp


def _window_bias():
    band = WINDOW + TQ
    k = np.arange(band)[:, None]
    r = np.arange(TQ)[None, :]
    cases = [k <= r + case * TQ for case in range(WINDOW // TQ)]
    cases.append((k > r) & (k <= r + WINDOW))
    return jnp.asarray(np.where(np.stack(cases), 0.0, -BIG), dtype=bf16)


def _layout_w_in(w):
    D = w.shape[0]
    w = w.astype(bf16)
    o = 0
    wq = w[:, o:o + ATTN_WIDTH]; o += ATTN_WIDTH
    kcw, vcw, ksw, vsw, kww, vww = [w[:, o + j * KV_WIDTH:o + (j + 1) * KV_WIDTH] for j in range(6)]
    o += 6 * KV_WIDTH
    gaw = w[:, o:o + N_BRANCH * N_HEADS]; o += N_BRANCH * N_HEADS
    uw = w[:, o:o + POOL_WIDTH]; o += POOL_WIDTH
    gmw = w[:, o:]
    gaw = jnp.concatenate([gaw, jnp.zeros((D, LANE - N_BRANCH * N_HEADS), w.dtype)], axis=1)
    out = jnp.concatenate([wq, ksw, kww, vsw, vww, kcw, vcw, uw, gmw, gaw], axis=1)
    assert out.shape[1] == W_COLS
    return out


def _block_diag_kv(m):
    z = jnp.zeros_like(m)
    return jnp.concatenate([jnp.concatenate([m, z], axis=-1), jnp.concatenate([z, m], axis=-1)], axis=-2)


def kernel(x, norm_mix, w_in, cmp_pe_k, cmp_pe_v, cmp_k_w1, cmp_k_w2, cmp_v_w1, cmp_v_w2,
           w_branch_attn, pool_w, pool_scale, w_branch_pool, w_out, norm_mlp, w_ff1, w_ff2,
           norm_final):
    B, S, D = x.shape
    assert norm_mix.shape[0] == 1, "single-layer block: the final norm is fused into out_mlp"
    assert S % TS_IN == 0 and S % TQ == 0 and (B * S) % TS_OUT == 0 and TS_IN % TK == 0
    assert TK % TQ == 0 and S % TK == 0 and WINDOW % TQ == 0 and N_KV == 2
    assert SWEEP_UNROLL & (SWEEP_UNROLL - 1) == 0
    n_chunk = S // CMP_STRIDE
    rc, rsa, rsb = _rope_tables(jnp.arange(S))
    crc, crsa, crsb = _rope_tables(jnp.arange(n_chunk) * CMP_STRIDE + CMP_BLOCK - 1)
    mt, n_cmp = _cmp_to_sel_t(S)

    (q, ks, kw, vs, vw, kc, vc, ga, pm, gm) = _in_proj(
        x, norm_mix[0][None, :], _layout_w_in(w_in[0]), rc, rsa, rsb,
        pool_w[0].astype(bf16), pool_scale[0][None, :])

    def first_layer(w1):
        per_token = _block_diag_kv(w1.reshape(CMP_BLOCK, HEAD_DIM, CMP_HIDDEN)).astype(bf16)
        return per_token.reshape(CMP_BLOCK // 2, 2 * N_KV * HEAD_DIM, N_KV * CMP_HIDDEN)

    def second_layer(w2):
        return _block_diag_kv(jnp.concatenate([w2, jnp.zeros_like(w2)], axis=1)).astype(bf16)

    kcmp, vcmp = _compress(
        kc, vc, jnp.tile(cmp_pe_k[0], (1, N_KV)), jnp.tile(cmp_pe_v[0], (1, N_KV)),
        first_layer(cmp_k_w1[0]), second_layer(cmp_k_w2[0]),
        first_layer(cmp_v_w1[0]), second_layer(cmp_v_w2[0]), crc, crsa, crsb)

    ao = _attention(q, kcmp, vcmp, ks, vs, kw, vw, ga, mt, _window_bias(), n_cmp)

    y = _out_mlp(x.reshape(B * S, D), ao.reshape(B * S, ATTN_WIDTH), pm.reshape(B * S, POOL_WIDTH),
                 gm.reshape(B * S, 2 * D_MODEL), w_branch_attn[0].astype(bf16),
                 w_branch_pool[0].astype(bf16), w_out[0].astype(bf16), norm_mlp[0][None, :],
                 w_ff1[0].astype(bf16), w_ff2[0].astype(bf16), norm_final[None, :])
    return y.reshape(B, S, D)
```

```python
import functools
import math

import jax
import jax.numpy as jnp
import numpy as np
from jax import lax
from jax.experimental import pallas as pl
from jax.experimental.pallas import tpu as pltpu

f32 = jnp.float32
bf16 = jnp.bfloat16

D_MODEL = 1024
N_HEADS = 8
HEAD_DIM = 64
N_KV = 2
GROUP = N_HEADS // N_KV
ROT_DIM = HEAD_DIM // 4
ROT_HALF = ROT_DIM // 2
ROPE_THETA = 500000.0
CMP_BLOCK = 32
CMP_STRIDE = 16
CMP_RATIO = CMP_BLOCK // CMP_STRIDE
CMP_HIDDEN = 4 * HEAD_DIM
SEL_BLOCK = 64
SEL_SHIFT = 6
SEL_TOPK = 16
WINDOW = 512
N_BRANCH = 3
ATTN_WIDTH = N_HEADS * HEAD_DIM
KV_WIDTH = N_KV * HEAD_DIM
POOL_WIDTH = 512
POOL_WINDOWS = (2, 4, 8, 16)
POOL_GROUP = POOL_WIDTH // len(POOL_WINDOWS)
POOL_HALO = 16
D_FF = 4 * D_MODEL
EPS = 1e-6
NEG_INF = -1e30
FORCE_SCORE = 1e4
BIG = 2.0 ** 100
M_INIT = -3.0e38
Q_SCALE = HEAD_DIM ** -0.5 * math.log2(math.e)

LANE = 128
SUBLANE = 8
VMEM_LIMIT = 56 * 1024 * 1024

SEG_Q = (0, ATTN_WIDTH)
SEG_KC = (SEG_Q[1], SEG_Q[1] + KV_WIDTH)
SEG_VC = (SEG_KC[1], SEG_KC[1] + KV_WIDTH)
SEG_KS = (SEG_VC[1], SEG_VC[1] + KV_WIDTH)
SEG_VS = (SEG_KS[1], SEG_KS[1] + KV_WIDTH)
SEG_KW = (SEG_VS[1], SEG_VS[1] + KV_WIDTH)
SEG_VW = (SEG_KW[1], SEG_KW[1] + KV_WIDTH)
SEG_U = (SEG_VW[1], SEG_VW[1] + POOL_WIDTH)
SEG_GM = (SEG_U[1], SEG_U[1] + 2 * D_MODEL)
SEG_GA = (SEG_GM[1], SEG_GM[1] + LANE)
W_COLS = SEG_GA[1]

TS_IN = 512
TQ = 256
TK = 256
SWEEP_UNROLL = 8
TS_OUT = 512


def _dot(a, b):
    return jnp.dot(a, b, preferred_element_type=f32)


def _swap_halves(x):
    return pltpu.roll(x, HEAD_DIM, axis=1)


def _rope_lane_tile(x, c, sa, sb):
    return (x * c + pltpu.roll(x, LANE - ROT_HALF, axis=1) * sa
            + pltpu.roll(x, ROT_HALF, axis=1) * sb)


def _rms(x, g):
    ms = jnp.mean(x * x, axis=-1, keepdims=True)
    return x * lax.rsqrt(ms + EPS) * g


def _lane_chunks(s):
    return [s[:, c * LANE:(c + 1) * LANE] for c in range(s.shape[1] // LANE)]


def _tree(fn, items):
    items = list(items)
    while len(items) > 1:
        items = [fn(items[j], items[j + 1]) if j + 1 < len(items) else items[j]
                 for j in range(0, len(items), 2)]
    return items[0]


def _in_proj_kernel(x_ref, g_ref, w_ref, rc_ref, rsa_ref, rsb_ref, pw_ref, ps_ref,
                    q_ref, ks_ref, kw_ref, vs_ref, vw_ref, kc_ref, vc_ref, ga_ref, pm_ref, gm_ref,
                    ubuf):
    si = pl.program_id(1)
    t0 = si * TS_IN

    @pl.when(si == 0)
    def _():
        ubuf[0:POOL_HALO, :] = jnp.zeros((POOL_HALO, POOL_WIDTH), f32)

    h = _rms(x_ref[0], g_ref[...]).astype(bf16)

    def seg(s):
        return _dot(h, w_ref[:, s[0]:s[1]])

    u = seg(SEG_U)
    ubuf[POOL_HALO:, :] = u
    tpos1 = (t0 + 1 + lax.broadcasted_iota(jnp.int32, (TS_IN, 1), 0)).astype(f32)

    def pool_steps():
        for gi, w in enumerate(POOL_WINDOWS):
            sl = slice(gi * POOL_GROUP, (gi + 1) * POOL_GROUP)
            acc = u[:, sl]
            for k in range(1, w):
                acc = acc + ubuf[POOL_HALO - k:POOL_HALO - k + TS_IN, sl]
                yield
            cnt = jnp.minimum(tpos1, float(w))
            pooled = (acc / cnt - u[:, sl]).astype(bf16)
            mixed = _dot(pooled, pw_ref[gi]) * ps_ref[:, sl]
            pm_ref[0, :, sl] = mixed.astype(bf16)
            yield
        ubuf[0:POOL_HALO, :] = ubuf[TS_IN:TS_IN + POOL_HALO, :]

    pool = pool_steps()

    def pool_advance(n):
        for _ in range(n):
            next(pool, None)

    rc, rsa, rsb = rc_ref[...], rsa_ref[...], rsb_ref[...]
    row_t = t0 + lax.broadcasted_iota(jnp.int32, (TS_IN, LANE), 0)
    lane = lax.broadcasted_iota(jnp.int32, (TS_IN, LANE), 1)
    lo = lane < HEAD_DIM

    def head_tiles(x):
        return (x, _swap_halves(x))

    q = seg(SEG_Q)
    pool_advance(4)
    for m, chunk in enumerate(_lane_chunks(q)):
        for half, xh in enumerate(head_tiles(chunk)):
            rot = jnp.where(lo, _rope_lane_tile(xh, rc, rsa, rsb) * Q_SCALE, 0.0)
            q_ref[0, 2 * m + half] = rot.T.astype(bf16)

    onehot = jnp.where(lane - HEAD_DIM == (row_t >> SEL_SHIFT), BIG, 0.0).astype(f32)
    sel_both = _lane_chunks(seg((SEG_KS[0], SEG_VS[1])))
    pool_advance(2)
    win_both = _lane_chunks(seg((SEG_KW[0], SEG_VW[1])))
    pool_advance(2)
    c_both = _lane_chunks(seg((SEG_KC[0], SEG_VC[1])))
    pool_advance(2)
    for gg, (ksh, kwh, vsh, vwh) in enumerate(zip(head_tiles(sel_both[0]), head_tiles(win_both[0]),
                                                   head_tiles(sel_both[1]), head_tiles(win_both[1]))):
        sl = slice(gg * LANE, (gg + 1) * LANE)
        ks_ref[0, :, sl] = jnp.where(lo, _rope_lane_tile(ksh, rc, rsa, rsb), onehot).astype(bf16)
        kw_ref[0, :, sl] = jnp.where(lo, _rope_lane_tile(kwh, rc, rsa, rsb), 0.0).astype(bf16)
        vs_t = jnp.where(lo, vsh, 1.0).T.astype(bf16)
        for j in range(TS_IN // TK):
            vs_ref[0, gg, j] = vs_t[:, j * TK:(j + 1) * TK]
        vw_t = jnp.where(lo, vwh, 1.0).T.astype(bf16)
        for j in range(TS_IN // TQ):
            vw_ref[0, gg, j] = vw_t[:, j * TQ:(j + 1) * TQ]

    kc_ref[0] = c_both[0]
    vc_ref[0] = c_both[1]
    ga_ref[0] = jax.nn.sigmoid(seg(SEG_GA)).T
    gm_cols = 2 * LANE * 2
    for c0 in range(SEG_GM[0], SEG_GM[1], gm_cols):
        part = seg((c0, c0 + gm_cols))
        pool_advance(5)
        gm_ref[0, :, c0 - SEG_GM[0]:c0 - SEG_GM[0] + gm_cols] = jax.nn.sigmoid(part).astype(bf16)
    for _ in pool:
        pass


def _in_proj(x, g, w, rc, rsa, rsb, pw, ps):
    B, S, D = x.shape
    grid = (B, S // TS_IN)
    tok = lambda width: pl.BlockSpec((1, TS_IN, width), lambda b, s: (b, s, 0))
    const = lambda shape: pl.BlockSpec(shape, lambda b, s: (0,) * len(shape),
                                       pipeline_mode=pl.Buffered(1))
    tab = pl.BlockSpec((TS_IN, LANE), lambda b, s: (s, 0))
    out_shapes = [
        jax.ShapeDtypeStruct((B, N_HEADS, LANE, S), bf16),
        jax.ShapeDtypeStruct((B, S, N_KV * LANE), bf16),
        jax.ShapeDtypeStruct((B, S, N_KV * LANE), bf16),
        jax.ShapeDtypeStruct((B, N_KV, S // TK, LANE, TK), bf16),
        jax.ShapeDtypeStruct((B, N_KV, S // TQ, LANE, TQ), bf16),
        jax.ShapeDtypeStruct((B, S, KV_WIDTH), f32),
        jax.ShapeDtypeStruct((B, S, KV_WIDTH), f32),
        jax.ShapeDtypeStruct((B, LANE, S), f32),
        jax.ShapeDtypeStruct((B, S, POOL_WIDTH), bf16),
        jax.ShapeDtypeStruct((B, S, 2 * D_MODEL), bf16),
    ]
    out_specs = [
        pl.BlockSpec((1, N_HEADS, LANE, TS_IN), lambda b, s: (b, 0, 0, s)),
        tok(N_KV * LANE), tok(N_KV * LANE),
        pl.BlockSpec((1, N_KV, TS_IN // TK, LANE, TK), lambda b, s: (b, 0, s, 0, 0)),
        pl.BlockSpec((1, N_KV, TS_IN // TQ, LANE, TQ), lambda b, s: (b, 0, s, 0, 0)),
        tok(KV_WIDTH), tok(KV_WIDTH),
        pl.BlockSpec((1, LANE, TS_IN), lambda b, s: (b, 0, s)),
        tok(POOL_WIDTH), tok(2 * D_MODEL),
    ]
    return pl.pallas_call(
        _in_proj_kernel,
        grid=grid,
        in_specs=[tok(D), const((1, D)), const((D, W_COLS)), tab, tab, tab,
                  const((len(POOL_WINDOWS), POOL_GROUP, POOL_GROUP)), const((1, POOL_WIDTH))],
        out_specs=out_specs,
        out_shape=out_shapes,
        scratch_shapes=[pltpu.VMEM((POOL_HALO + TS_IN, POOL_WIDTH), f32)],
        compiler_params=pltpu.CompilerParams(
            dimension_semantics=("arbitrary", "arbitrary"), vmem_limit_bytes=VMEM_LIMIT),
        name="in_proj",
    )(x, g, w, rc, rsa, rsb, pw, ps)


def _compress_kernel(kc_ref, vc_ref, pek_ref, pev_ref, kw1_ref, kw2_ref, vw1_ref, vw2_ref,
                     rc_ref, rsa_ref, rsb_ref, kout_ref, vout_ref):
    n_chunk = kc_ref.shape[1] // CMP_STRIDE

    def hidden(src_ref, pe_ref, w1_ref):
        parts = []
        for part in range(CMP_RATIO):
            acc = None
            for tok in range(0, CMP_STRIDE, 2):
                t = part * CMP_STRIDE + tok
                a = jnp.concatenate(
                    [(src_ref[0, pl.ds(tok + u, n_chunk, stride=CMP_STRIDE), :]
                      + pe_ref[t + u:t + u + 1, :]).astype(bf16) for u in range(2)], axis=1)
                d = _dot(a, w1_ref[t // 2])
                acc = d if acc is None else acc + d
            parts.append(acc)
        pre = parts[0] + pltpu.roll(parts[1], n_chunk - 1, axis=0)
        return jax.nn.gelu(pre, approximate=True).astype(bf16)

    lane = lax.broadcasted_iota(jnp.int32, (n_chunk, LANE), 1)
    lo = lane < HEAD_DIM
    kcmp = _dot(hidden(kc_ref, pek_ref, kw1_ref), kw2_ref[...])
    vcmp = _dot(hidden(vc_ref, pev_ref, vw1_ref), vw2_ref[...])
    for gg in range(N_KV):
        sl = slice(gg * LANE, (gg + 1) * LANE)
        kout_ref[0, gg] = _rope_lane_tile(kcmp[:, sl], rc_ref[...], rsa_ref[...], rsb_ref[...]).astype(bf16)
        vout_ref[0, gg] = jnp.where(lo, vcmp[:, sl], 1.0).T.astype(bf16)


def _compress(kc, vc, pek, pev, kw1, kw2, vw1, vw2, rc, rsa, rsb):
    B, S, W = kc.shape
    NC = S // CMP_STRIDE
    const = lambda shape: pl.BlockSpec(shape, lambda b: (0,) * len(shape))
    src = pl.BlockSpec((1, S, W), lambda b: (b, 0, 0))
    return pl.pallas_call(
        _compress_kernel,
        grid=(B,),
        in_specs=[src, src, const(pek.shape), const(pev.shape), const(kw1.shape), const(kw2.shape),
                  const(vw1.shape), const(vw2.shape), const(rc.shape), const(rsa.shape), const(rsb.shape)],
        out_specs=[pl.BlockSpec((1, N_KV, NC, LANE), lambda b: (b, 0, 0, 0)),
                   pl.BlockSpec((1, N_KV, LANE, NC), lambda b: (b, 0, 0, 0))],
        out_shape=[jax.ShapeDtypeStruct((B, N_KV, NC, LANE), bf16),
                   jax.ShapeDtypeStruct((B, N_KV, LANE, NC), bf16)],
        compiler_params=pltpu.CompilerParams(
            dimension_semantics=("arbitrary",), vmem_limit_bytes=VMEM_LIMIT),
        name="compress",
    )(kc, vc, pek, pev, kw1, kw2, vw1, vw2, rc, rsa, rsb)


def _attn_kernel(q_ref, kc_ref, vc_ref, ks_ref, vs_ref, kw_ref, vw_ref, ga_ref, mt_ref, wb_ref, o_ref,
                 qa_sc, s_sc, mx_sc, acc_sc, imp_sc, sel_sc, *, n_cmp, n_sel):
    g = pl.program_id(1)
    i = pl.program_id(2)
    t0 = i * TQ
    cols = GROUP * TQ
    ncp = kc_ref.shape[2]

    tok = t0 + lax.broadcasted_iota(jnp.int32, (1, TQ), 1)
    tok4 = t0 + (lax.broadcasted_iota(jnp.int32, (1, cols), 1) & (TQ - 1))
    head_cols = lambda a, r: a[:, r * TQ:(r + 1) * TQ]

    q_all = jnp.concatenate([q_ref[0, r] for r in range(GROUP)], axis=1)

    kc = kc_ref[0, 0]
    vc = vc_ref[0, 0]
    cpos = lax.broadcasted_iota(jnp.int32, (ncp, 1), 0)
    cvalid = ((cpos * CMP_STRIDE + (CMP_BLOCK - 1)) <= tok) & (cpos < n_cmp)
    cbias = jnp.where(cvalid, 0.0, NEG_INF).astype(f32)
    any_valid = (tok >= CMP_BLOCK - 1).astype(f32)
    s_cmp = _dot(kc, q_all)
    band = WINDOW + TQ
    w0 = pl.multiple_of(jnp.maximum(t0 - WINDOW, 0), TQ)
    wbias = wb_ref[jnp.minimum(i, WINDOW // TQ)]
    kwin = kw_ref[0, pl.ds(w0, band), :]
    vwin = jnp.concatenate([vw_ref[0, 0, w0 // TQ + j] for j in range(band // TQ)], axis=1)
    s_win = _dot(kwin, q_all).astype(bf16)
    psum = jnp.zeros((ncp, TQ), f32)
    e_cmp = []
    inv_cmp = []
    for r in range(GROUP):
        s = head_cols(s_cmp, r) + cbias
        e = jnp.exp2(s - jnp.max(s, axis=0, keepdims=True))
        inv = 1.0 / jnp.sum(e, axis=0, keepdims=True)
        psum = psum + e * inv
        e_cmp.append(e.astype(bf16))
        inv_cmp.append(inv * any_valid)
    pv_cmp = _dot(vc, jnp.concatenate(e_cmp, axis=1))

    mt = mt_ref[...]
    p_hi = psum.astype(bf16)
    rem = psum - p_hi.astype(f32)
    p_mid = rem.astype(bf16)
    p_lo = (rem - p_mid.astype(f32)).astype(bf16)
    imp = _dot(mt, p_hi) + _dot(mt, p_mid) + _dot(mt, p_lo)
    blk = lax.broadcasted_iota(jnp.int32, (n_sel, TQ), 0)
    cur = tok >> SEL_SHIFT
    imp = jnp.where(blk > cur, -FORCE_SCORE, imp)
    imp = jnp.where((blk == 0) | (blk == cur) | (blk == cur - 1), FORCE_SCORE, imp)

    p_win = []
    for r in range(GROUP):
        s = head_cols(s_win, r) + wbias
        p_win.append(jnp.exp2(s - jnp.max(s, axis=0, keepdims=True)))
    pv_win = _dot(vwin, jnp.concatenate(p_win, axis=1))

    imp_sc[...] = imp
    sel_sc[...] = jnp.full((n_sel, TQ), -1.0, f32)
    sub = lax.broadcasted_iota(jnp.int32, (SUBLANE, TQ), 0)
    row_keys = SUBLANE * SEL_BLOCK
    rows_in_play = (t0 + TQ + row_keys - 1) // row_keys
    for n_rows in range(1, n_sel // SUBLANE + 1):
        @pl.when(rows_in_play == n_rows)
        def _(n_rows=n_rows):
            vrows = [imp_sc[v * SUBLANE:(v + 1) * SUBLANE, :] for v in range(n_rows)]
            ahead = [jnp.zeros((SUBLANE, TQ), f32) for _ in range(n_rows)]
            for vj in range(n_rows):
                for sj in range(SUBLANE):
                    rival = jnp.broadcast_to(vrows[vj][sj:sj + 1, :], (SUBLANE, TQ))
                    for v in range(n_rows):
                        if v > vj:
                            ahead[v] = ahead[v] + jnp.where(rival >= vrows[v], 1.0, 0.0)
                        elif v < vj:
                            ahead[v] = ahead[v] + jnp.where(rival > vrows[v], 1.0, 0.0)
                        else:
                            tie = jnp.where(rival == vrows[v], (sub > sj).astype(f32), 0.0)
                            ahead[v] = ahead[v] + jnp.where(rival > vrows[v], 1.0, tie)
            for v in range(n_rows):
                sel_sc[v * SUBLANE:(v + 1) * SUBLANE, :] = jnp.where(
                    ahead[v] < float(min(SEL_TOPK, n_sel)), 0.0, -1.0)

    helper = jnp.concatenate([jnp.zeros((LANE - n_sel, TQ), f32), sel_sc[...]], axis=0).astype(bf16)
    qa_sc[...] = q_all + jnp.concatenate([helper] * GROUP, axis=1)


    n_tiles = (t0 + TQ + TK - 1) // TK
    mx_sc[...] = jnp.full((SUBLANE, cols), M_INIT, f32)

    def score_tiles(kts, ends_sweep):
        slabs = []
        for idx, kt in enumerate(kts):
            k0 = pl.multiple_of(kt * TK, TK)
            s = _dot(ks_ref[0, pl.ds(k0, TK), :], qa_sc[...])
            if ends_sweep and idx == len(kts) - 1:
                kpos = k0 + lax.broadcasted_iota(jnp.int32, (TK, 1), 0)
                s = jnp.where(kpos <= tok4, s, -BIG)
            s_sc[kt] = s.astype(bf16)
            slabs += [s[j * SUBLANE:(j + 1) * SUBLANE] for j in range(TK // SUBLANE)]
        mx_sc[...] = jnp.maximum(mx_sc[...], _tree(jnp.maximum, slabs))

    def weight_tiles(kts, ends_sweep):
        m = mx_sc[0:1, :].astype(bf16)
        total = None
        for kt in kts:
            p = jnp.exp2(s_sc[kt] - m)
            d = _dot(vs_ref[0, 0, kt], p)
            total = d if total is None else total + d
        acc_sc[...] += total

    def sweep(n, tiles_fn):
        def trip(j, carry):
            tiles_fn([SWEEP_UNROLL * j + u for u in range(SWEEP_UNROLL)], False)
            return carry

        trips = (n - 1) // SWEEP_UNROLL
        lax.fori_loop(0, trips, trip, 0)
        done = trips * SWEEP_UNROLL
        for left in range(1, SWEEP_UNROLL + 1):
            @pl.when(n - done == left)
            def _(left=left):
                tiles_fn([done + u for u in range(left)], True)

    sweep(n_tiles, score_tiles)
    mx_sc[...] = jnp.broadcast_to(jnp.max(mx_sc[...], axis=0, keepdims=True), (SUBLANE, cols))
    acc_sc[...] = jnp.zeros((LANE, cols), f32)
    sweep(n_tiles, weight_tiles)
    pv_sel = acc_sc[...]

    outs = []
    for r in range(GROUP):
        def branch(pv):
            x = head_cols(pv, r)
            return x[:HEAD_DIM], x[HEAD_DIM:HEAD_DIM + 1]

        num_c, _ = branch(pv_cmp)
        num_s, den_s = branch(pv_sel)
        num_w, den_w = branch(pv_win)
        gate = lambda br: ga_ref[0, pl.ds(br * N_HEADS + g * GROUP + r, 1), :]
        outs.append((gate(0) * inv_cmp[r]) * num_c + (gate(1) / den_s) * num_s + (gate(2) / den_w) * num_w)
    o_ref[0] = jnp.concatenate(outs, axis=0).T.astype(bf16)


def _attention(q, kcmp, vcmp, ks, vs, kw, vw, ga, mt, wb, n_cmp):
    B, _, _, S = q.shape
    NC = kcmp.shape[2]
    n_sel = S // SEL_BLOCK
    cols = GROUP * TQ
    grid = (B, N_KV, S // TQ)
    keys_g = pl.BlockSpec((1, S, LANE), lambda b, g, i: (b, 0, g))
    const = lambda a: pl.BlockSpec(a.shape, lambda b, g, i: (0,) * a.ndim)
    return pl.pallas_call(
        functools.partial(_attn_kernel, n_cmp=n_cmp, n_sel=n_sel),
        grid=grid,
        in_specs=[
            pl.BlockSpec((1, GROUP, LANE, TQ), lambda b, g, i: (b, g, 0, i)),
            pl.BlockSpec((1, 1, NC, LANE), lambda b, g, i: (b, g, 0, 0)),
            pl.BlockSpec((1, 1, LANE, NC), lambda b, g, i: (b, g, 0, 0)),
            keys_g,
            pl.BlockSpec((1, 1, S // TK, LANE, TK), lambda b, g, i: (b, g, 0, 0, 0)),
            keys_g,
            pl.BlockSpec((1, 1, S // TQ, LANE, TQ), lambda b, g, i: (b, g, 0, 0, 0)),
            pl.BlockSpec((1, LANE, TQ), lambda b, g, i: (b, 0, i)),
            const(mt), const(wb),
        ],
        out_specs=pl.BlockSpec((1, TQ, GROUP * HEAD_DIM), lambda b, g, i: (b, i, g)),
        out_shape=jax.ShapeDtypeStruct((B, S, ATTN_WIDTH), bf16),
        scratch_shapes=[pltpu.VMEM((LANE, cols), bf16),
                        pltpu.VMEM((S // TK, TK, cols), bf16),
                        pltpu.VMEM((SUBLANE, cols), f32),
                        pltpu.VMEM((LANE, cols), f32),
                        pltpu.VMEM((n_sel, TQ), f32),
                        pltpu.VMEM((n_sel, TQ), f32)],
        compiler_params=pltpu.CompilerParams(
            dimension_semantics=("arbitrary", "arbitrary", "arbitrary"),
            vmem_limit_bytes=VMEM_LIMIT),
        name="attention",
    )(q, kcmp, vcmp, ks, vs, kw, vw, ga, mt, wb)


def _out_mlp_kernel(x_ref, ao_ref, pm_ref, gm_ref, wba_ref, wbp_ref, wo_ref, gn_ref,
                    w1_ref, w2_ref, gf_ref, o_ref):
    a = _dot(ao_ref[...], wba_ref[...])
    b = _dot(pm_ref[...], wbp_ref[...])
    gm = gm_ref[...].astype(f32)
    merged = (gm[:, :D_MODEL] * a + gm[:, D_MODEL:] * b).astype(bf16)
    x1 = x_ref[...] + _dot(merged, wo_ref[...])
    h = _rms(x1, gn_ref[...]).astype(bf16)
    f = jnp.square(jnp.maximum(_dot(h, w1_ref[...]), 0.0)).astype(bf16)
    x2 = x1 + _dot(f, w2_ref[...])
    o_ref[...] = _rms(x2, gf_ref[...])


def _out_mlp(x2d, ao, pm, gm, wba, wbp, wo, gn, w1, w2, gf):
    T, D = x2d.shape
    tok = lambda width: pl.BlockSpec((TS_OUT, width), lambda t: (t, 0))
    const = lambda arr: pl.BlockSpec(arr.shape, lambda t: (0, 0), pipeline_mode=pl.Buffered(1))
    return pl.pallas_call(
        _out_mlp_kernel,
        grid=(T // TS_OUT,),
        in_specs=[tok(D), tok(ATTN_WIDTH), tok(POOL_WIDTH), tok(2 * D_MODEL),
                  const(wba), const(wbp), const(wo), const(gn), const(w1), const(w2), const(gf)],
        out_specs=tok(D),
        out_shape=jax.ShapeDtypeStruct((T, D), f32),
        compiler_params=pltpu.CompilerParams(
            dimension_semantics=("arbitrary",), vmem_limit_bytes=VMEM_LIMIT),
        name="out_mlp",
    )(x2d, ao, pm, gm, wba, wbp, wo, gn, w1, w2, gf)


def _rope_tables(pos):
    inv = ROPE_THETA ** (-np.arange(0, ROT_DIM, 2, dtype=np.float64) / ROT_DIM)
    ang = pos.astype(np.float64)[:, None] * inv
    cos, sin = np.cos(ang), np.sin(ang)
    P = pos.shape[0]
    c = np.concatenate([cos, cos, np.ones((P, LANE - ROT_DIM))], axis=1)
    sa = np.concatenate([-sin, np.zeros((P, LANE - ROT_HALF))], axis=1)
    sb = np.concatenate([np.zeros((P, ROT_HALF)), sin, np.zeros((P, LANE - ROT_DIM))], axis=1)
    return tuple(jnp.asarray(t, dtype=f32) for t in (c, sa, sb))


def _cmp_to_sel_t(S):
    n_cmp = (S - CMP_BLOCK) // CMP_STRIDE + 1
    n_sel = S // SEL_BLOCK
    cs = np.arange(n_cmp)[:, None] * CMP_STRIDE
    js = np.arange(n_sel)[None, :] * SEL_BLOCK
    ov = np.clip(np.minimum(cs + CMP_BLOCK, js + SEL_BLOCK) - np.maximum(cs, js), 0, None)
    m = np.zeros((S // CMP_STRIDE, n_sel), np.float32)
    m[:n_cmp] = ov / CMP_STRIDE
    return jnp.asarray(m.T, dtype=bf16), n_cmp


def _window_bias():
    band = WINDOW + TQ
    k = np.arange(band)[:, None]
    r = np.arange(TQ)[None, :]
    cases = [k <= r + case * TQ for case in range(WINDOW // TQ)]
    cases.append((k > r) & (k <= r + WINDOW))
    return jnp.asarray(np.where(np.stack(cases), 0.0, -BIG), dtype=bf16)


def _layout_w_in(w):
    D = w.shape[0]
    w = w.astype(bf16)
    n_gate = N_BRANCH * N_HEADS
    head = w[:, :SEG_U[0]]
    gates = w[:, SEG_U[0]:SEG_U[0] + n_gate]
    rest = w[:, SEG_U[0] + n_gate:]
    out = jnp.concatenate([head, rest, gates, jnp.zeros((D, LANE - n_gate), w.dtype)], axis=1)
    assert out.shape[1] == W_COLS and rest.shape[1] == POOL_WIDTH + 2 * D_MODEL
    return out


def _block_diag_kv(m):
    z = jnp.zeros_like(m)
    return jnp.concatenate([jnp.concatenate([m, z], axis=-1), jnp.concatenate([z, m], axis=-1)], axis=-2)


def kernel(x, norm_mix, w_in, cmp_pe_k, cmp_pe_v, cmp_k_w1, cmp_k_w2, cmp_v_w1, cmp_v_w2,
           w_branch_attn, pool_w, pool_scale, w_branch_pool, w_out, norm_mlp, w_ff1, w_ff2,
           norm_final):
    B, S, D = x.shape
    assert norm_mix.shape[0] == 1, "single-layer block: the final norm is fused into out_mlp"
    assert S % TS_IN == 0 and S % TQ == 0 and (B * S) % TS_OUT == 0 and TS_IN % TK == 0
    assert TK % TQ == 0 and S % TK == 0 and WINDOW % TQ == 0 and N_KV == 2
    assert SWEEP_UNROLL & (SWEEP_UNROLL - 1) == 0
    n_chunk = S // CMP_STRIDE
    rc, rsa, rsb = _rope_tables(np.arange(S))
    crc, crsa, crsb = _rope_tables(np.arange(n_chunk) * CMP_STRIDE + CMP_BLOCK - 1)
    mt, n_cmp = _cmp_to_sel_t(S)
    layer = lambda a: a.reshape(a.shape[1:])
    row = lambda a: a.reshape(1, -1)

    (q, ks, kw, vs, vw, kc, vc, ga, pm, gm) = _in_proj(
        x, row(norm_mix), _layout_w_in(layer(w_in)), rc, rsa, rsb,
        layer(pool_w).astype(bf16), row(pool_scale))

    def first_layer(w1):
        per_token = _block_diag_kv(w1.reshape(CMP_BLOCK, HEAD_DIM, CMP_HIDDEN)).astype(bf16)
        return per_token.reshape(CMP_BLOCK // 2, 2 * N_KV * HEAD_DIM, N_KV * CMP_HIDDEN)

    def second_layer(w2):
        return _block_diag_kv(jnp.concatenate([w2, jnp.zeros_like(w2)], axis=1)).astype(bf16)

    kcmp, vcmp = _compress(
        kc, vc, jnp.tile(layer(cmp_pe_k), (1, N_KV)), jnp.tile(layer(cmp_pe_v), (1, N_KV)),
        first_layer(layer(cmp_k_w1)), second_layer(layer(cmp_k_w2)),
        first_layer(layer(cmp_v_w1)), second_layer(layer(cmp_v_w2)), crc, crsa, crsb)

    ao = _attention(q, kcmp, vcmp, ks, vs, kw, vw, ga, mt, _window_bias(), n_cmp)

    y = _out_mlp(x.reshape(B * S, D), ao.reshape(B * S, ATTN_WIDTH), pm.reshape(B * S, POOL_WIDTH),
                 gm.reshape(B * S, 2 * D_MODEL), layer(w_branch_attn).astype(bf16),
                 layer(w_branch_pool).astype(bf16), layer(w_out).astype(bf16), row(norm_mlp),
                 layer(w_ff1).astype(bf16), layer(w_ff2).astype(bf16), row(norm_final))
    return y.reshape(B, S, D)
```

```python
import functools
import math

import jax
import jax.numpy as jnp
import numpy as np
from jax import lax
from jax.experimental import pallas as pl
from jax.experimental.pallas import tpu as pltpu

f32 = jnp.float32
bf16 = jnp.bfloat16

D_MODEL = 1024
N_HEADS = 8
HEAD_DIM = 64
N_KV = 2
GROUP = N_HEADS // N_KV
ROT_DIM = HEAD_DIM // 4
ROT_HALF = ROT_DIM // 2
ROPE_THETA = 500000.0
CMP_BLOCK = 32
CMP_STRIDE = 16
CMP_RATIO = CMP_BLOCK // CMP_STRIDE
CMP_HIDDEN = 4 * HEAD_DIM
SEL_BLOCK = 64
SEL_SHIFT = 6
SEL_TOPK = 16
WINDOW = 512
N_BRANCH = 3
ATTN_WIDTH = N_HEADS * HEAD_DIM
KV_WIDTH = N_KV * HEAD_DIM
POOL_WIDTH = 512
POOL_WINDOWS = (2, 4, 8, 16)
POOL_GROUP = POOL_WIDTH // len(POOL_WINDOWS)
POOL_HALO = 16
D_FF = 4 * D_MODEL
EPS = 1e-6
NEG_INF = -1e30
FORCE_SCORE = 1e4
BIG = 2.0 ** 100
M_INIT = -3.0e38
Q_SCALE = HEAD_DIM ** -0.5 * math.log2(math.e)

LANE = 128
SUBLANE = 8
VMEM_LIMIT = 56 * 1024 * 1024

SEG_Q = (0, ATTN_WIDTH)
SEG_KC = (SEG_Q[1], SEG_Q[1] + KV_WIDTH)
SEG_VC = (SEG_KC[1], SEG_KC[1] + KV_WIDTH)
SEG_KS = (SEG_VC[1], SEG_VC[1] + KV_WIDTH)
SEG_VS = (SEG_KS[1], SEG_KS[1] + KV_WIDTH)
SEG_KW = (SEG_VS[1], SEG_VS[1] + KV_WIDTH)
SEG_VW = (SEG_KW[1], SEG_KW[1] + KV_WIDTH)
SEG_U = (SEG_VW[1], SEG_VW[1] + POOL_WIDTH)
SEG_GM = (SEG_U[1], SEG_U[1] + 2 * D_MODEL)
SEG_GA = (SEG_GM[1], SEG_GM[1] + LANE)
W_COLS = SEG_GA[1]

TS_IN = 512
TQ = 256
TK = 256
SWEEP_UNROLL = 8
TS_OUT = 512


def _dot(a, b):
    return jnp.dot(a, b, preferred_element_type=f32)


def _swap_halves(x):
    return pltpu.roll(x, HEAD_DIM, axis=1)


def _rope_lane_tile(x, c, sa, sb):
    return (x * c + pltpu.roll(x, LANE - ROT_HALF, axis=1) * sa
            + pltpu.roll(x, ROT_HALF, axis=1) * sb)


def _rms(x, g):
    ms = jnp.mean(x * x, axis=-1, keepdims=True)
    return x * lax.rsqrt(ms + EPS) * g


def _lane_chunks(s):
    return [s[:, c * LANE:(c + 1) * LANE] for c in range(s.shape[1] // LANE)]


def _tree(fn, items):
    items = list(items)
    while len(items) > 1:
        items = [fn(items[j], items[j + 1]) if j + 1 < len(items) else items[j]
                 for j in range(0, len(items), 2)]
    return items[0]


def _in_proj_kernel(x_ref, g_ref, w_ref, rc_ref, rsa_ref, rsb_ref, pw_ref, ps_ref,
                    q_ref, ks_ref, kw_ref, vs_ref, vw_ref, kc_ref, vc_ref, ga_ref, pm_ref, gm_ref,
                    ubuf):
    si = pl.program_id(1)
    t0 = si * TS_IN

    @pl.when(si == 0)
    def _():
        ubuf[0:POOL_HALO, :] = jnp.zeros((POOL_HALO, POOL_WIDTH), f32)

    h = _rms(x_ref[0], g_ref[...]).astype(bf16)

    def seg(s):
        return _dot(h, w_ref[:, s[0]:s[1]])

    u = seg(SEG_U)
    ubuf[POOL_HALO:, :] = u
    tpos1 = (t0 + 1 + lax.broadcasted_iota(jnp.int32, (TS_IN, 1), 0)).astype(f32)

    def pool_steps():
        for gi, w in enumerate(POOL_WINDOWS):
            sl = slice(gi * POOL_GROUP, (gi + 1) * POOL_GROUP)
            acc = u[:, sl]
            for k in range(1, w):
                acc = acc + ubuf[POOL_HALO - k:POOL_HALO - k + TS_IN, sl]
                yield
            cnt = jnp.minimum(tpos1, float(w))
            pooled = (acc / cnt - u[:, sl]).astype(bf16)
            mixed = _dot(pooled, pw_ref[gi]) * ps_ref[:, sl]
            pm_ref[0, :, sl] = mixed.astype(bf16)
            yield
        ubuf[0:POOL_HALO, :] = ubuf[TS_IN:TS_IN + POOL_HALO, :]

    pool = pool_steps()

    def pool_advance(n):
        for _ in range(n):
            next(pool, None)

    rc, rsa, rsb = rc_ref[...], rsa_ref[...], rsb_ref[...]
    row_t = t0 + lax.broadcasted_iota(jnp.int32, (TS_IN, LANE), 0)
    lane = lax.broadcasted_iota(jnp.int32, (TS_IN, LANE), 1)
    lo = lane < HEAD_DIM

    def head_tiles(x):
        return (x, _swap_halves(x))

    q = seg(SEG_Q)
    pool_advance(4)
    for m, chunk in enumerate(_lane_chunks(q)):
        for half, xh in enumerate(head_tiles(chunk)):
            rot = jnp.where(lo, _rope_lane_tile(xh, rc, rsa, rsb) * Q_SCALE, 0.0)
            q_ref[0, 2 * m + half] = rot.T.astype(bf16)

    onehot = jnp.where(lane - HEAD_DIM == (row_t >> SEL_SHIFT), BIG, 0.0).astype(f32)
    sel_both = _lane_chunks(seg((SEG_KS[0], SEG_VS[1])))
    pool_advance(2)
    win_both = _lane_chunks(seg((SEG_KW[0], SEG_VW[1])))
    pool_advance(4)
    for gg, (ksh, kwh, vsh, vwh) in enumerate(zip(head_tiles(sel_both[0]), head_tiles(win_both[0]),
                                                   head_tiles(sel_both[1]), head_tiles(win_both[1]))):
        sl = slice(gg * LANE, (gg + 1) * LANE)
        ks_ref[0, :, sl] = jnp.where(lo, _rope_lane_tile(ksh, rc, rsa, rsb), onehot).astype(bf16)
        kw_ref[0, :, sl] = jnp.where(lo, _rope_lane_tile(kwh, rc, rsa, rsb), 0.0).astype(bf16)
        vs_t = jnp.where(lo, vsh, 1.0).T.astype(bf16)
        for j in range(TS_IN // TK):
            vs_ref[0, gg, j] = vs_t[:, j * TK:(j + 1) * TK]
        vw_t = jnp.where(lo, vwh, 1.0).T.astype(bf16)
        for j in range(TS_IN // TQ):
            vw_ref[0, gg, j] = vw_t[:, j * TQ:(j + 1) * TQ]

    ga_ref[0] = jax.nn.sigmoid(seg(SEG_GA)).T
    gm_cols = 2 * LANE * 2
    for c0 in range(SEG_GM[0], SEG_GM[1], gm_cols):
        part = seg((c0, c0 + gm_cols))
        pool_advance(5)
        gm_ref[0, :, c0 - SEG_GM[0]:c0 - SEG_GM[0] + gm_cols] = jax.nn.sigmoid(part).astype(bf16)
    for _ in pool:
        pass
    c_both = _lane_chunks(seg((SEG_KC[0], SEG_VC[1])))
    kc_ref[0] = c_both[0]
    vc_ref[0] = c_both[1]


def _in_proj(x, g, w, rc, rsa, rsb, pw, ps):
    B, S, D = x.shape
    grid = (B, S // TS_IN)
    tok = lambda width: pl.BlockSpec((1, TS_IN, width), lambda b, s: (b, s, 0))
    const = lambda shape: pl.BlockSpec(shape, lambda b, s: (0,) * len(shape),
                                       pipeline_mode=pl.Buffered(1))
    tab = pl.BlockSpec((TS_IN, LANE), lambda b, s: (s, 0))
    out_shapes = [
        jax.ShapeDtypeStruct((B, N_HEADS, LANE, S), bf16),
        jax.ShapeDtypeStruct((B, S, N_KV * LANE), bf16),
        jax.ShapeDtypeStruct((B, S, N_KV * LANE), bf16),
        jax.ShapeDtypeStruct((B, N_KV, S // TK, LANE, TK), bf16),
        jax.ShapeDtypeStruct((B, N_KV, S // TQ, LANE, TQ), bf16),
        jax.ShapeDtypeStruct((B, S, KV_WIDTH), f32),
        jax.ShapeDtypeStruct((B, S, KV_WIDTH), f32),
        jax.ShapeDtypeStruct((B, LANE, S), f32),
        jax.ShapeDtypeStruct((B, S, POOL_WIDTH), bf16),
        jax.ShapeDtypeStruct((B, S, 2 * D_MODEL), bf16),
    ]
    out_specs = [
        pl.BlockSpec((1, N_HEADS, LANE, TS_IN), lambda b, s: (b, 0, 0, s)),
        tok(N_KV * LANE), tok(N_KV * LANE),
        pl.BlockSpec((1, N_KV, TS_IN // TK, LANE, TK), lambda b, s: (b, 0, s, 0, 0)),
        pl.BlockSpec((1, N_KV, TS_IN // TQ, LANE, TQ), lambda b, s: (b, 0, s, 0, 0)),
        tok(KV_WIDTH), tok(KV_WIDTH),
        pl.BlockSpec((1, LANE, TS_IN), lambda b, s: (b, 0, s)),
        tok(POOL_WIDTH), tok(2 * D_MODEL),
    ]
    return pl.pallas_call(
        _in_proj_kernel,
        grid=grid,
        in_specs=[tok(D), const((1, D)), const((D, W_COLS)), tab, tab, tab,
                  const((len(POOL_WINDOWS), POOL_GROUP, POOL_GROUP)), const((1, POOL_WIDTH))],
        out_specs=out_specs,
        out_shape=out_shapes,
        scratch_shapes=[pltpu.VMEM((POOL_HALO + TS_IN, POOL_WIDTH), f32)],
        compiler_params=pltpu.CompilerParams(
            dimension_semantics=("arbitrary", "arbitrary"), vmem_limit_bytes=VMEM_LIMIT),
        name="in_proj",
    )(x, g, w, rc, rsa, rsb, pw, ps)


def _compress_kernel(kc_ref, vc_ref, pek_ref, pev_ref, kw1_ref, kw2_ref, vw1_ref, vw2_ref,
                     rc_ref, rsa_ref, rsb_ref, kout_ref, vout_ref):
    n_chunk = kc_ref.shape[1] // CMP_STRIDE

    def hidden(src_ref, pe_ref, w1_ref):
        parts = []
        for part in range(CMP_RATIO):
            acc = None
            for tok in range(0, CMP_STRIDE, 2):
                t = part * CMP_STRIDE + tok
                a = jnp.concatenate(
                    [(src_ref[0, pl.ds(tok + u, n_chunk, stride=CMP_STRIDE), :]
                      + pe_ref[t + u:t + u + 1, :]).astype(bf16) for u in range(2)], axis=1)
                d = _dot(a, w1_ref[t // 2])
                acc = d if acc is None else acc + d
            parts.append(acc)
        pre = parts[0] + pltpu.roll(parts[1], n_chunk - 1, axis=0)
        return jax.nn.gelu(pre, approximate=True).astype(bf16)

    lane = lax.broadcasted_iota(jnp.int32, (n_chunk, LANE), 1)
    lo = lane < HEAD_DIM
    kcmp = _dot(hidden(kc_ref, pek_ref, kw1_ref), kw2_ref[...])
    vcmp = _dot(hidden(vc_ref, pev_ref, vw1_ref), vw2_ref[...])
    for gg in range(N_KV):
        sl = slice(gg * LANE, (gg + 1) * LANE)
        kout_ref[0, gg] = _rope_lane_tile(kcmp[:, sl], rc_ref[...], rsa_ref[...], rsb_ref[...]).astype(bf16)
        vout_ref[0, gg] = jnp.where(lo, vcmp[:, sl], 1.0).T.astype(bf16)


def _compress(kc, vc, pek, pev, kw1, kw2, vw1, vw2, rc, rsa, rsb):
    B, S, W = kc.shape
    NC = S // CMP_STRIDE
    const = lambda shape: pl.BlockSpec(shape, lambda b: (0,) * len(shape))
    src = pl.BlockSpec((1, S, W), lambda b: (b, 0, 0))
    return pl.pallas_call(
        _compress_kernel,
        grid=(B,),
        in_specs=[src, src, const(pek.shape), const(pev.shape), const(kw1.shape), const(kw2.shape),
                  const(vw1.shape), const(vw2.shape), const(rc.shape), const(rsa.shape), const(rsb.shape)],
        out_specs=[pl.BlockSpec((1, N_KV, NC, LANE), lambda b: (b, 0, 0, 0)),
                   pl.BlockSpec((1, N_KV, LANE, NC), lambda b: (b, 0, 0, 0))],
        out_shape=[jax.ShapeDtypeStruct((B, N_KV, NC, LANE), bf16),
                   jax.ShapeDtypeStruct((B, N_KV, LANE, NC), bf16)],
        compiler_params=pltpu.CompilerParams(
            dimension_semantics=("arbitrary",), vmem_limit_bytes=VMEM_LIMIT),
        name="compress",
    )(kc, vc, pek, pev, kw1, kw2, vw1, vw2, rc, rsa, rsb)


def _attn_kernel(q_ref, kc_ref, vc_ref, ks_ref, vs_ref, kw_ref, vw_ref, ga_ref, mt_ref, wb_ref, o_ref,
                 qa_sc, s_sc, mx_sc, acc_sc, imp_sc, sel_sc, *, n_cmp, n_sel):
    g = pl.program_id(1)
    i = pl.program_id(2)
    t0 = i * TQ
    cols = GROUP * TQ
    ncp = kc_ref.shape[2]

    tok = t0 + lax.broadcasted_iota(jnp.int32, (1, TQ), 1)
    tok4 = t0 + (lax.broadcasted_iota(jnp.int32, (1, cols), 1) & (TQ - 1))
    head_cols = lambda a, r: a[:, r * TQ:(r + 1) * TQ]

    q_all = jnp.concatenate([q_ref[0, r] for r in range(GROUP)], axis=1)

    kc = kc_ref[0, 0]
    vc = vc_ref[0, 0]
    cpos = lax.broadcasted_iota(jnp.int32, (ncp, 1), 0)
    cvalid = ((cpos * CMP_STRIDE + (CMP_BLOCK - 1)) <= tok) & (cpos < n_cmp)
    cbias = jnp.where(cvalid, 0.0, NEG_INF).astype(f32)
    any_valid = (tok >= CMP_BLOCK - 1).astype(f32)
    s_cmp = _dot(kc, q_all)
    band = WINDOW + TQ
    w0 = pl.multiple_of(jnp.maximum(t0 - WINDOW, 0), TQ)
    wbias = wb_ref[jnp.minimum(i, WINDOW // TQ)]
    kwin = kw_ref[0, pl.ds(w0, band), :]
    vwin = jnp.concatenate([vw_ref[0, 0, w0 // TQ + j] for j in range(band // TQ)], axis=1)
    s_win = _dot(kwin, q_all).astype(bf16)
    psum = jnp.zeros((ncp, TQ), f32)
    e_cmp = []
    inv_cmp = []
    for r in range(GROUP):
        s = head_cols(s_cmp, r) + cbias
        e = jnp.exp2(s - jnp.max(s, axis=0, keepdims=True))
        inv = 1.0 / jnp.sum(e, axis=0, keepdims=True)
        psum = psum + e * inv
        e_cmp.append(e.astype(bf16))
        inv_cmp.append(inv * any_valid)
    pv_cmp = _dot(vc, jnp.concatenate(e_cmp, axis=1))

    mt = mt_ref[...]
    p_hi = psum.astype(bf16)
    rem = psum - p_hi.astype(f32)
    p_mid = rem.astype(bf16)
    p_lo = (rem - p_mid.astype(f32)).astype(bf16)
    imp = _dot(mt, p_hi) + _dot(mt, p_mid) + _dot(mt, p_lo)
    blk = lax.broadcasted_iota(jnp.int32, (n_sel, TQ), 0)
    cur = tok >> SEL_SHIFT
    imp = jnp.where(blk > cur, -FORCE_SCORE, imp)
    imp = jnp.where((blk == 0) | (blk == cur) | (blk == cur - 1), FORCE_SCORE, imp)

    p_win = []
    for r in range(GROUP):
        s = head_cols(s_win, r) + wbias
        p_win.append(jnp.exp2(s - jnp.max(s, axis=0, keepdims=True)))
    pv_win = _dot(vwin, jnp.concatenate(p_win, axis=1))

    imp_sc[...] = imp
    sel_sc[...] = jnp.full((n_sel, TQ), -1.0, f32)
    sub = lax.broadcasted_iota(jnp.int32, (SUBLANE, TQ), 0)
    row_keys = SUBLANE * SEL_BLOCK
    rows_in_play = (t0 + TQ + row_keys - 1) // row_keys
    for n_rows in range(1, n_sel // SUBLANE + 1):
        @pl.when(rows_in_play == n_rows)
        def _(n_rows=n_rows):
            vrows = [imp_sc[v * SUBLANE:(v + 1) * SUBLANE, :] for v in range(n_rows)]
            ahead = [jnp.zeros((SUBLANE, TQ), f32) for _ in range(n_rows)]
            for vj in range(n_rows):
                for sj in range(SUBLANE):
                    rival = jnp.broadcast_to(vrows[vj][sj:sj + 1, :], (SUBLANE, TQ))
                    for v in range(n_rows):
                        if v > vj:
                            ahead[v] = ahead[v] + jnp.where(rival >= vrows[v], 1.0, 0.0)
                        elif v < vj:
                            ahead[v] = ahead[v] + jnp.where(rival > vrows[v], 1.0, 0.0)
                        else:
                            tie = jnp.where(rival == vrows[v], (sub > sj).astype(f32), 0.0)
                            ahead[v] = ahead[v] + jnp.where(rival > vrows[v], 1.0, tie)
            for v in range(n_rows):
                sel_sc[v * SUBLANE:(v + 1) * SUBLANE, :] = jnp.where(
                    ahead[v] < float(min(SEL_TOPK, n_sel)), 0.0, -1.0)

    helper = jnp.concatenate([jnp.zeros((LANE - n_sel, TQ), f32), sel_sc[...]], axis=0).astype(bf16)
    qa_sc[...] = q_all + jnp.concatenate([helper] * GROUP, axis=1)


    n_tiles = (t0 + TQ + TK - 1) // TK
    mx_sc[...] = jnp.full((SUBLANE, cols), M_INIT, f32)

    def score_tiles(kts, ends_sweep):
        slabs = []
        for idx, kt in enumerate(kts):
            k0 = pl.multiple_of(kt * TK, TK)
            s = _dot(ks_ref[0, pl.ds(k0, TK), :], qa_sc[...])
            if ends_sweep and idx == len(kts) - 1:
                kpos = k0 + lax.broadcasted_iota(jnp.int32, (TK, 1), 0)
                s = jnp.where(kpos <= tok4, s, -BIG)
            s_sc[kt] = s.astype(bf16)
            slabs += [s[j * SUBLANE:(j + 1) * SUBLANE] for j in range(TK // SUBLANE)]
        mx_sc[...] = jnp.maximum(mx_sc[...], _tree(jnp.maximum, slabs))

    def weight_tiles(kts, ends_sweep):
        m = mx_sc[0:1, :].astype(bf16)
        total = None
        for kt in kts:
            p = jnp.exp2(s_sc[kt] - m)
            d = _dot(vs_ref[0, 0, kt], p)
            total = d if total is None else total + d
        acc_sc[...] += total

    def sweep(n, tiles_fn):
        def trip(j, carry):
            tiles_fn([SWEEP_UNROLL * j + u for u in range(SWEEP_UNROLL)], False)
            return carry

        trips = (n - 1) // SWEEP_UNROLL
        lax.fori_loop(0, trips, trip, 0)
        done = trips * SWEEP_UNROLL
        for left in range(1, SWEEP_UNROLL + 1):
            @pl.when(n - done == left)
            def _(left=left):
                tiles_fn([done + u for u in range(left)], True)

    sweep(n_tiles, score_tiles)
    mx_sc[...] = jnp.broadcast_to(jnp.max(mx_sc[...], axis=0, keepdims=True), (SUBLANE, cols))
    acc_sc[...] = jnp.zeros((LANE, cols), f32)
    sweep(n_tiles, weight_tiles)
    pv_sel = acc_sc[...]

    outs = []
    for r in range(GROUP):
        def branch(pv):
            x = head_cols(pv, r)
            return x[:HEAD_DIM], x[HEAD_DIM:HEAD_DIM + 1]

        num_c, _ = branch(pv_cmp)
        num_s, den_s = branch(pv_sel)
        num_w, den_w = branch(pv_win)
        gate = lambda br: ga_ref[0, pl.ds(br * N_HEADS + g * GROUP + r, 1), :]
        outs.append((gate(0) * inv_cmp[r]) * num_c + (gate(1) / den_s) * num_s + (gate(2) / den_w) * num_w)
    o_ref[0] = jnp.concatenate(outs, axis=0).T.astype(bf16)


def _attention(q, kcmp, vcmp, ks, vs, kw, vw, ga, mt, wb, n_cmp):
    B, _, _, S = q.shape
    NC = kcmp.shape[2]
    n_sel = S // SEL_BLOCK
    cols = GROUP * TQ
    grid = (B, N_KV, S // TQ)
    keys_g = pl.BlockSpec((1, S, LANE), lambda b, g, i: (b, 0, g))
    const = lambda a: pl.BlockSpec(a.shape, lambda b, g, i: (0,) * a.ndim)
    return pl.pallas_call(
        functools.partial(_attn_kernel, n_cmp=n_cmp, n_sel=n_sel),
        grid=grid,
        in_specs=[
            pl.BlockSpec((1, GROUP, LANE, TQ), lambda b, g, i: (b, g, 0, i)),
            pl.BlockSpec((1, 1, NC, LANE), lambda b, g, i: (b, g, 0, 0)),
            pl.BlockSpec((1, 1, LANE, NC), lambda b, g, i: (b, g, 0, 0)),
            keys_g,
            pl.BlockSpec((1, 1, S // TK, LANE, TK), lambda b, g, i: (b, g, 0, 0, 0)),
            keys_g,
            pl.BlockSpec((1, 1, S // TQ, LANE, TQ), lambda b, g, i: (b, g, 0, 0, 0)),
            pl.BlockSpec((1, LANE, TQ), lambda b, g, i: (b, 0, i)),
            const(mt), const(wb),
        ],
        out_specs=pl.BlockSpec((1, TQ, GROUP * HEAD_DIM), lambda b, g, i: (b, i, g)),
        out_shape=jax.ShapeDtypeStruct((B, S, ATTN_WIDTH), bf16),
        scratch_shapes=[pltpu.VMEM((LANE, cols), bf16),
                        pltpu.VMEM((S // TK, TK, cols), bf16),
                        pltpu.VMEM((SUBLANE, cols), f32),
                        pltpu.VMEM((LANE, cols), f32),
                        pltpu.VMEM((n_sel, TQ), f32),
                        pltpu.VMEM((n_sel, TQ), f32)],
        compiler_params=pltpu.CompilerParams(
            dimension_semantics=("arbitrary", "arbitrary", "arbitrary"),
            vmem_limit_bytes=VMEM_LIMIT),
        name="attention",
    )(q, kcmp, vcmp, ks, vs, kw, vw, ga, mt, wb)


def _out_mlp_kernel(x_ref, ao_ref, pm_ref, gm_ref, wba_ref, wbp_ref, wo_ref, gn_ref,
                    w1_ref, w2_ref, gf_ref, o_ref):
    a = _dot(ao_ref[...], wba_ref[...])
    b = _dot(pm_ref[...], wbp_ref[...])
    gm = gm_ref[...].astype(f32)
    merged = (gm[:, :D_MODEL] * a + gm[:, D_MODEL:] * b).astype(bf16)
    x1 = x_ref[...] + _dot(merged, wo_ref[...])
    h = _rms(x1, gn_ref[...]).astype(bf16)
    f = jnp.square(jnp.maximum(_dot(h, w1_ref[...]), 0.0)).astype(bf16)
    x2 = x1 + _dot(f, w2_ref[...])
    o_ref[...] = _rms(x2, gf_ref[...])


def _out_mlp(x2d, ao, pm, gm, wba, wbp, wo, gn, w1, w2, gf):
    T, D = x2d.shape
    tok = lambda width: pl.BlockSpec((TS_OUT, width), lambda t: (t, 0))
    const = lambda arr: pl.BlockSpec(arr.shape, lambda t: (0, 0), pipeline_mode=pl.Buffered(1))
    return pl.pallas_call(
        _out_mlp_kernel,
        grid=(T // TS_OUT,),
        in_specs=[tok(D), tok(ATTN_WIDTH), tok(POOL_WIDTH), tok(2 * D_MODEL),
                  const(wba), const(wbp), const(wo), const(gn), const(w1), const(w2), const(gf)],
        out_specs=tok(D),
        out_shape=jax.ShapeDtypeStruct((T, D), f32),
        compiler_params=pltpu.CompilerParams(
            dimension_semantics=("arbitrary",), vmem_limit_bytes=VMEM_LIMIT),
        name="out_mlp",
    )(x2d, ao, pm, gm, wba, wbp, wo, gn, w1, w2, gf)


def _rope_tables(pos):
    inv = ROPE_THETA ** (-np.arange(0, ROT_DIM, 2, dtype=np.float64) / ROT_DIM)
    ang = pos.astype(np.float64)[:, None] * inv
    cos, sin = np.cos(ang), np.sin(ang)
    P = pos.shape[0]
    c = np.concatenate([cos, cos, np.ones((P, LANE - ROT_DIM))], axis=1)
    sa = np.concatenate([-sin, np.zeros((P, LANE - ROT_HALF))], axis=1)
    sb = np.concatenate([np.zeros((P, ROT_HALF)), sin, np.zeros((P, LANE - ROT_DIM))], axis=1)
    return tuple(jnp.asarray(t, dtype=f32) for t in (c, sa, sb))


def _cmp_to_sel_t(S):
    n_cmp = (S - CMP_BLOCK) // CMP_STRIDE + 1
    n_sel = S // SEL_BLOCK
    cs = np.arange(n_cmp)[:, None] * CMP_STRIDE
    js = np.arange(n_sel)[None, :] * SEL_BLOCK
    ov = np.clip(np.minimum(cs + CMP_BLOCK, js + SEL_BLOCK) - np.maximum(cs, js), 0, None)
    m = np.zeros((S // CMP_STRIDE, n_sel), np.float32)
    m[:n_cmp] = ov / CMP_STRIDE
    return jnp.asarray(m.T, dtype=bf16), n_cmp


def _window_bias():
    band = WINDOW + TQ
    k = np.arange(band)[:, None]
    r = np.arange(TQ)[None, :]
    cases = [k <= r + case * TQ for case in range(WINDOW // TQ)]
    cases.append((k > r) & (k <= r + WINDOW))
    return jnp.asarray(np.where(np.stack(cases), 0.0, -BIG), dtype=bf16)


def _layout_w_in(w):
    D = w.shape[0]
    w = w.astype(bf16)
    n_gate = N_BRANCH * N_HEADS
    head = w[:, :SEG_U[0]]
    gates = w[:, SEG_U[0]:SEG_U[0] + n_gate]
    rest = w[:, SEG_U[0] + n_gate:]
    out = jnp.concatenate([head, rest, gates, jnp.zeros((D, LANE - n_gate), w.dtype)], axis=1)
    assert out.shape[1] == W_COLS and rest.shape[1] == POOL_WIDTH + 2 * D_MODEL
    return out


def _block_diag_kv(m):
    z = jnp.zeros_like(m)
    return jnp.concatenate([jnp.concatenate([m, z], axis=-1), jnp.concatenate([z, m], axis=-1)], axis=-2)


def kernel(x, norm_mix, w_in, cmp_pe_k, cmp_pe_v, cmp_k_w1, cmp_k_w2, cmp_v_w1, cmp_v_w2,
           w_branch_attn, pool_w, pool_scale, w_branch_pool, w_out, norm_mlp, w_ff1, w_ff2,
           norm_final):
    B, S, D = x.shape
    assert norm_mix.shape[0] == 1, "single-layer block: the final norm is fused into out_mlp"
    assert S % TS_IN == 0 and S % TQ == 0 and (B * S) % TS_OUT == 0 and TS_IN % TK == 0
    assert TK % TQ == 0 and S % TK == 0 and WINDOW % TQ == 0 and N_KV == 2
    assert SWEEP_UNROLL & (SWEEP_UNROLL - 1) == 0
    n_chunk = S // CMP_STRIDE
    rc, rsa, rsb = _rope_tables(np.arange(S))
    crc, crsa, crsb = _rope_tables(np.arange(n_chunk) * CMP_STRIDE + CMP_BLOCK - 1)
    mt, n_cmp = _cmp_to_sel_t(S)
    layer = lambda a: a.reshape(a.shape[1:])
    row = lambda a: a.reshape(1, -1)

    (q, ks, kw, vs, vw, kc, vc, ga, pm, gm) = _in_proj(
        x, row(norm_mix), _layout_w_in(layer(w_in)), rc, rsa, rsb,
        layer(pool_w).astype(bf16), row(pool_scale))

    def first_layer(w1):
        per_token = _block_diag_kv(w1.reshape(CMP_BLOCK, HEAD_DIM, CMP_HIDDEN)).astype(bf16)
        return per_token.reshape(CMP_BLOCK // 2, 2 * N_KV * HEAD_DIM, N_KV * CMP_HIDDEN)

    def second_layer(w2):
        return _block_diag_kv(jnp.concatenate([w2, jnp.zeros_like(w2)], axis=1)).astype(bf16)

    kcmp, vcmp = _compress(
        kc, vc, jnp.tile(layer(cmp_pe_k), (1, N_KV)), jnp.tile(layer(cmp_pe_v), (1, N_KV)),
        first_layer(layer(cmp_k_w1)), second_layer(layer(cmp_k_w2)),
        first_layer(layer(cmp_v_w1)), second_layer(layer(cmp_v_w2)), crc, crsa, crsb)

    ao = _attention(q, kcmp, vcmp, ks, vs, kw, vw, ga, mt, _window_bias(), n_cmp)

    y = _out_mlp(x.reshape(B * S, D), ao.reshape(B * S, ATTN_WIDTH), pm.reshape(B * S, POOL_WIDTH),
                 gm.reshape(B * S, 2 * D_MODEL), layer(w_branch_attn).astype(bf16),
                 layer(w_branch_pool).astype(bf16), layer(w_out).astype(bf16), row(norm_mlp),
                 layer(w_ff1).astype(bf16), layer(w_ff2).astype(bf16), row(norm_final))
    return y.reshape(B, S, D)
```

```python
import functools
import math

import jax
import jax.numpy as jnp
import numpy as np
from jax import lax
from jax.experimental import pallas as pl
from jax.experimental.pallas import tpu as pltpu

f32 = jnp.float32
bf16 = jnp.bfloat16

D_MODEL = 1024
N_HEADS = 8
HEAD_DIM = 64
N_KV = 2
GROUP = N_HEADS // N_KV
ROT_DIM = HEAD_DIM // 4
ROT_HALF = ROT_DIM // 2
ROPE_THETA = 500000.0
CMP_BLOCK = 32
CMP_STRIDE = 16
CMP_RATIO = CMP_BLOCK // CMP_STRIDE
CMP_HIDDEN = 4 * HEAD_DIM
SEL_BLOCK = 64
SEL_SHIFT = 6
SEL_TOPK = 16
WINDOW = 512
N_BRANCH = 3
ATTN_WIDTH = N_HEADS * HEAD_DIM
KV_WIDTH = N_KV * HEAD_DIM
POOL_WIDTH = 512
POOL_WINDOWS = (2, 4, 8, 16)
POOL_GROUP = POOL_WIDTH // len(POOL_WINDOWS)
POOL_HALO = 16
D_FF = 4 * D_MODEL
EPS = 1e-6
NEG_INF = -1e30
FORCE_SCORE = 1e4
BIG = 2.0 ** 100
M_INIT = -3.0e38
Q_SCALE = HEAD_DIM ** -0.5 * math.log2(math.e)

LANE = 128
SUBLANE = 8
VMEM_LIMIT = 56 * 1024 * 1024

SEG_Q = (0, ATTN_WIDTH)
SEG_KC = (SEG_Q[1], SEG_Q[1] + KV_WIDTH)
SEG_VC = (SEG_KC[1], SEG_KC[1] + KV_WIDTH)
SEG_KS = (SEG_VC[1], SEG_VC[1] + KV_WIDTH)
SEG_VS = (SEG_KS[1], SEG_KS[1] + KV_WIDTH)
SEG_KW = (SEG_VS[1], SEG_VS[1] + KV_WIDTH)
SEG_VW = (SEG_KW[1], SEG_KW[1] + KV_WIDTH)
SEG_U = (SEG_VW[1], SEG_VW[1] + POOL_WIDTH)
SEG_GM = (SEG_U[1], SEG_U[1] + 2 * D_MODEL)
SEG_GA = (SEG_GM[1], SEG_GM[1] + LANE)
W_COLS = SEG_GA[1]

TS_IN = 512
TQ = 256
TK = 256
SWEEP_UNROLL = 8
TS_OUT = 512


def _dot(a, b):
    return jnp.dot(a, b, preferred_element_type=f32)


def _swap_halves(x):
    return pltpu.roll(x, HEAD_DIM, axis=1)


def _rope_lane_tile(x, c, sa, sb):
    return (x * c + pltpu.roll(x, LANE - ROT_HALF, axis=1) * sa
            + pltpu.roll(x, ROT_HALF, axis=1) * sb)


def _rms(x, g):
    ms = jnp.mean(x * x, axis=-1, keepdims=True)
    return x * lax.rsqrt(ms + EPS) * g


def _lane_chunks(s):
    return [s[:, c * LANE:(c + 1) * LANE] for c in range(s.shape[1] // LANE)]


def _tree(fn, items):
    items = list(items)
    while len(items) > 1:
        items = [fn(items[j], items[j + 1]) if j + 1 < len(items) else items[j]
                 for j in range(0, len(items), 2)]
    return items[0]


def _in_proj_kernel(x_ref, g_ref, w_ref, rc_ref, rsa_ref, rsb_ref, pw_ref, ps_ref,
                    q_ref, ks_ref, kw_ref, vs_ref, vw_ref, kc_ref, vc_ref, ga_ref, pm_ref, gm_ref,
                    ubuf):
    si = pl.program_id(1)
    t0 = si * TS_IN

    @pl.when(si == 0)
    def _():
        ubuf[0:POOL_HALO, :] = jnp.zeros((POOL_HALO, POOL_WIDTH), f32)

    h = _rms(x_ref[0], g_ref[...]).astype(bf16)

    def seg(s):
        return _dot(h, w_ref[:, s[0]:s[1]])

    u = seg(SEG_U)
    ubuf[POOL_HALO:, :] = u
    tpos1 = (t0 + 1 + lax.broadcasted_iota(jnp.int32, (TS_IN, 1), 0)).astype(f32)

    def pool_steps():
        for gi, w in enumerate(POOL_WINDOWS):
            sl = slice(gi * POOL_GROUP, (gi + 1) * POOL_GROUP)
            acc = u[:, sl]
            for k in range(1, w):
                acc = acc + ubuf[POOL_HALO - k:POOL_HALO - k + TS_IN, sl]
                yield
            cnt = jnp.minimum(tpos1, float(w))
            pooled = (acc / cnt - u[:, sl]).astype(bf16)
            mixed = _dot(pooled, pw_ref[gi]) * ps_ref[:, sl]
            pm_ref[0, :, sl] = mixed.astype(bf16)
            yield
        ubuf[0:POOL_HALO, :] = ubuf[TS_IN:TS_IN + POOL_HALO, :]

    pool = pool_steps()

    def pool_advance(n):
        for _ in range(n):
            next(pool, None)

    rc, rsa, rsb = rc_ref[...], rsa_ref[...], rsb_ref[...]
    row_t = t0 + lax.broadcasted_iota(jnp.int32, (TS_IN, LANE), 0)
    lane = lax.broadcasted_iota(jnp.int32, (TS_IN, LANE), 1)
    lo = lane < HEAD_DIM

    def head_tiles(x):
        return (x, _swap_halves(x))

    q = seg(SEG_Q)
    pool_advance(4)
    for m, chunk in enumerate(_lane_chunks(q)):
        for half, xh in enumerate(head_tiles(chunk)):
            rot = jnp.where(lo, _rope_lane_tile(xh, rc, rsa, rsb) * Q_SCALE, 0.0)
            q_ref[0, 2 * m + half] = rot.T.astype(bf16)

    onehot = jnp.where(lane - HEAD_DIM == (row_t >> SEL_SHIFT), BIG, 0.0).astype(f32)
    sel_both = _lane_chunks(seg((SEG_KS[0], SEG_VS[1])))
    pool_advance(2)
    win_both = _lane_chunks(seg((SEG_KW[0], SEG_VW[1])))
    pool_advance(4)
    for gg, (ksh, kwh, vsh, vwh) in enumerate(zip(head_tiles(sel_both[0]), head_tiles(win_both[0]),
                                                   head_tiles(sel_both[1]), head_tiles(win_both[1]))):
        sl = slice(gg * LANE, (gg + 1) * LANE)
        ks_ref[0, :, sl] = jnp.where(lo, _rope_lane_tile(ksh, rc, rsa, rsb), onehot).astype(bf16)
        kw_ref[0, :, sl] = jnp.where(lo, _rope_lane_tile(kwh, rc, rsa, rsb), 0.0).astype(bf16)
        vs_t = jnp.where(lo, vsh, 1.0).T.astype(bf16)
        for j in range(TS_IN // TK):
            vs_ref[0, gg, j] = vs_t[:, j * TK:(j + 1) * TK]
        vw_t = jnp.where(lo, vwh, 1.0).T.astype(bf16)
        for j in range(TS_IN // TQ):
            vw_ref[0, gg, j] = vw_t[:, j * TQ:(j + 1) * TQ]

    ga_ref[0] = jax.nn.sigmoid(seg(SEG_GA)).T
    merge_logits = seg(SEG_GM)
    for _ in pool:
        pass
    gm_ref[0] = jax.nn.sigmoid(merge_logits).astype(bf16)
    c_both = _lane_chunks(seg((SEG_KC[0], SEG_VC[1])))
    kc_ref[0] = c_both[0]
    vc_ref[0] = c_both[1]


def _in_proj(x, g, w, rc, rsa, rsb, pw, ps):
    B, S, D = x.shape
    grid = (B, S // TS_IN)
    tok = lambda width: pl.BlockSpec((1, TS_IN, width), lambda b, s: (b, s, 0))
    const = lambda shape: pl.BlockSpec(shape, lambda b, s: (0,) * len(shape),
                                       pipeline_mode=pl.Buffered(1))
    tab = pl.BlockSpec((TS_IN, LANE), lambda b, s: (s, 0))
    out_shapes = [
        jax.ShapeDtypeStruct((B, N_HEADS, LANE, S), bf16),
        jax.ShapeDtypeStruct((B, S, N_KV * LANE), bf16),
        jax.ShapeDtypeStruct((B, S, N_KV * LANE), bf16),
        jax.ShapeDtypeStruct((B, N_KV, S // TK, LANE, TK), bf16),
        jax.ShapeDtypeStruct((B, N_KV, S // TQ, LANE, TQ), bf16),
        jax.ShapeDtypeStruct((B, S, KV_WIDTH), f32),
        jax.ShapeDtypeStruct((B, S, KV_WIDTH), f32),
        jax.ShapeDtypeStruct((B, LANE, S), f32),
        jax.ShapeDtypeStruct((B, S, POOL_WIDTH), bf16),
        jax.ShapeDtypeStruct((B, S, 2 * D_MODEL), bf16),
    ]
    out_specs = [
        pl.BlockSpec((1, N_HEADS, LANE, TS_IN), lambda b, s: (b, 0, 0, s)),
        tok(N_KV * LANE), tok(N_KV * LANE),
        pl.BlockSpec((1, N_KV, TS_IN // TK, LANE, TK), lambda b, s: (b, 0, s, 0, 0)),
        pl.BlockSpec((1, N_KV, TS_IN // TQ, LANE, TQ), lambda b, s: (b, 0, s, 0, 0)),
        tok(KV_WIDTH), tok(KV_WIDTH),
        pl.BlockSpec((1, LANE, TS_IN), lambda b, s: (b, 0, s)),
        tok(POOL_WIDTH), tok(2 * D_MODEL),
    ]
    return pl.pallas_call(
        _in_proj_kernel,
        grid=grid,
        in_specs=[tok(D), const((1, D)), const((D, W_COLS)), tab, tab, tab,
                  const((len(POOL_WINDOWS), POOL_GROUP, POOL_GROUP)), const((1, POOL_WIDTH))],
        out_specs=out_specs,
        out_shape=out_shapes,
        scratch_shapes=[pltpu.VMEM((POOL_HALO + TS_IN, POOL_WIDTH), f32)],
        compiler_params=pltpu.CompilerParams(
            dimension_semantics=("arbitrary", "arbitrary"), vmem_limit_bytes=VMEM_LIMIT),
        name="in_proj",
    )(x, g, w, rc, rsa, rsb, pw, ps)


def _compress_kernel(kc_ref, vc_ref, pek_ref, pev_ref, kw1_ref, kw2_ref, vw1_ref, vw2_ref,
                     rc_ref, rsa_ref, rsb_ref, kout_ref, vout_ref):
    n_chunk = kc_ref.shape[1] // CMP_STRIDE

    def hidden(src_ref, pe_ref, w1_ref):
        parts = []
        for part in range(CMP_RATIO):
            acc = None
            for tok in range(0, CMP_STRIDE, 2):
                t = part * CMP_STRIDE + tok
                a = jnp.concatenate(
                    [(src_ref[0, pl.ds(tok + u, n_chunk, stride=CMP_STRIDE), :]
                      + pe_ref[t + u:t + u + 1, :]).astype(bf16) for u in range(2)], axis=1)
                d = _dot(a, w1_ref[t // 2])
                acc = d if acc is None else acc + d
            parts.append(acc)
        pre = parts[0] + pltpu.roll(parts[1], n_chunk - 1, axis=0)
        return jax.nn.gelu(pre, approximate=True).astype(bf16)

    lane = lax.broadcasted_iota(jnp.int32, (n_chunk, LANE), 1)
    lo = lane < HEAD_DIM
    kcmp = _dot(hidden(kc_ref, pek_ref, kw1_ref), kw2_ref[...])
    vcmp = _dot(hidden(vc_ref, pev_ref, vw1_ref), vw2_ref[...])
    for gg in range(N_KV):
        sl = slice(gg * LANE, (gg + 1) * LANE)
        kout_ref[0, gg] = _rope_lane_tile(kcmp[:, sl], rc_ref[...], rsa_ref[...], rsb_ref[...]).astype(bf16)
        vout_ref[0, gg] = jnp.where(lo, vcmp[:, sl], 1.0).T.astype(bf16)


def _compress(kc, vc, pek, pev, kw1, kw2, vw1, vw2, rc, rsa, rsb):
    B, S, W = kc.shape
    NC = S // CMP_STRIDE
    const = lambda shape: pl.BlockSpec(shape, lambda b: (0,) * len(shape))
    src = pl.BlockSpec((1, S, W), lambda b: (b, 0, 0))
    return pl.pallas_call(
        _compress_kernel,
        grid=(B,),
        in_specs=[src, src, const(pek.shape), const(pev.shape), const(kw1.shape), const(kw2.shape),
                  const(vw1.shape), const(vw2.shape), const(rc.shape), const(rsa.shape), const(rsb.shape)],
        out_specs=[pl.BlockSpec((1, N_KV, NC, LANE), lambda b: (b, 0, 0, 0)),
                   pl.BlockSpec((1, N_KV, LANE, NC), lambda b: (b, 0, 0, 0))],
        out_shape=[jax.ShapeDtypeStruct((B, N_KV, NC, LANE), bf16),
                   jax.ShapeDtypeStruct((B, N_KV, LANE, NC), bf16)],
        compiler_params=pltpu.CompilerParams(
            dimension_semantics=("arbitrary",), vmem_limit_bytes=VMEM_LIMIT),
        name="compress",
    )(kc, vc, pek, pev, kw1, kw2, vw1, vw2, rc, rsa, rsb)


def _attn_kernel(q_ref, kc_ref, vc_ref, ks_ref, vs_ref, kw_ref, vw_ref, ga_ref, mt_ref, wb_ref, o_ref,
                 qa_sc, s_sc, mx_sc, acc_sc, imp_sc, sel_sc, *, n_cmp, n_sel):
    g = pl.program_id(1)
    i = pl.program_id(2)
    t0 = i * TQ
    cols = GROUP * TQ
    ncp = kc_ref.shape[2]

    tok = t0 + lax.broadcasted_iota(jnp.int32, (1, TQ), 1)
    tok4 = t0 + (lax.broadcasted_iota(jnp.int32, (1, cols), 1) & (TQ - 1))
    head_cols = lambda a, r: a[:, r * TQ:(r + 1) * TQ]

    q_all = jnp.concatenate([q_ref[0, r] for r in range(GROUP)], axis=1)

    kc = kc_ref[0, 0]
    vc = vc_ref[0, 0]
    cpos = lax.broadcasted_iota(jnp.int32, (ncp, 1), 0)
    cvalid = ((cpos * CMP_STRIDE + (CMP_BLOCK - 1)) <= tok) & (cpos < n_cmp)
    cbias = jnp.where(cvalid, 0.0, NEG_INF).astype(f32)
    any_valid = (tok >= CMP_BLOCK - 1).astype(f32)
    s_cmp = _dot(kc, q_all)
    band = WINDOW + TQ
    w0 = pl.multiple_of(jnp.maximum(t0 - WINDOW, 0), TQ)
    wbias = wb_ref[jnp.minimum(i, WINDOW // TQ)]
    kwin = kw_ref[0, pl.ds(w0, band), :]
    vwin = jnp.concatenate([vw_ref[0, 0, w0 // TQ + j] for j in range(band // TQ)], axis=1)
    s_win = _dot(kwin, q_all).astype(bf16)
    p_win = []
    psum = jnp.zeros((ncp, TQ), f32)
    e_cmp = []
    inv_cmp = []
    for r in range(GROUP):
        s = head_cols(s_win, r) + wbias
        p_win.append(jnp.exp2(s - jnp.max(s, axis=0, keepdims=True)))
        s = head_cols(s_cmp, r) + cbias
        e = jnp.exp2(s - jnp.max(s, axis=0, keepdims=True))
        inv = 1.0 / jnp.sum(e, axis=0, keepdims=True)
        psum = psum + e * inv
        e_cmp.append(e.astype(bf16))
        inv_cmp.append(inv * any_valid)
    pv_win = _dot(vwin, jnp.concatenate(p_win, axis=1))
    pv_cmp = _dot(vc, jnp.concatenate(e_cmp, axis=1))

    mt = mt_ref[...]
    p_hi = psum.astype(bf16)
    rem = psum - p_hi.astype(f32)
    p_mid = rem.astype(bf16)
    p_lo = (rem - p_mid.astype(f32)).astype(bf16)
    imp = _dot(mt, p_hi) + _dot(mt, p_mid) + _dot(mt, p_lo)
    blk = lax.broadcasted_iota(jnp.int32, (n_sel, TQ), 0)
    cur = tok >> SEL_SHIFT
    imp = jnp.where(blk > cur, -FORCE_SCORE, imp)
    imp = jnp.where((blk == 0) | (blk == cur) | (blk == cur - 1), FORCE_SCORE, imp)

    imp_sc[...] = imp
    sel_sc[...] = jnp.full((n_sel, TQ), -1.0, f32)
    sub = lax.broadcasted_iota(jnp.int32, (SUBLANE, TQ), 0)
    row_keys = SUBLANE * SEL_BLOCK
    rows_in_play = (t0 + TQ + row_keys - 1) // row_keys
    for n_rows in range(1, n_sel // SUBLANE + 1):
        @pl.when(rows_in_play == n_rows)
        def _(n_rows=n_rows):
            vrows = [imp_sc[v * SUBLANE:(v + 1) * SUBLANE, :] for v in range(n_rows)]
            ahead = [jnp.zeros((SUBLANE, TQ), f32) for _ in range(n_rows)]
            for vj in range(n_rows):
                for sj in range(SUBLANE):
                    rival = jnp.broadcast_to(vrows[vj][sj:sj + 1, :], (SUBLANE, TQ))
                    for v in range(n_rows):
                        if v > vj:
                            ahead[v] = ahead[v] + jnp.where(rival >= vrows[v], 1.0, 0.0)
                        elif v < vj:
                            ahead[v] = ahead[v] + jnp.where(rival > vrows[v], 1.0, 0.0)
                        else:
                            tie = jnp.where(rival == vrows[v], (sub > sj).astype(f32), 0.0)
                            ahead[v] = ahead[v] + jnp.where(rival > vrows[v], 1.0, tie)
            for v in range(n_rows):
                sel_sc[v * SUBLANE:(v + 1) * SUBLANE, :] = jnp.where(
                    ahead[v] < float(min(SEL_TOPK, n_sel)), 0.0, -1.0)

    helper = jnp.concatenate([jnp.zeros((LANE - n_sel, TQ), f32), sel_sc[...]], axis=0).astype(bf16)
    qa_sc[...] = q_all + jnp.concatenate([helper] * GROUP, axis=1)


    n_tiles = (t0 + TQ + TK - 1) // TK
    mx_sc[...] = jnp.full((SUBLANE, cols), M_INIT, f32)

    def score_tiles(kts, ends_sweep):
        slabs = []
        for idx, kt in enumerate(kts):
            k0 = pl.multiple_of(kt * TK, TK)
            s = _dot(ks_ref[0, pl.ds(k0, TK), :], qa_sc[...])
            if ends_sweep and idx == len(kts) - 1:
                kpos = k0 + lax.broadcasted_iota(jnp.int32, (TK, 1), 0)
                s = jnp.where(kpos <= tok4, s, -BIG)
            s_sc[kt] = s.astype(bf16)
            slabs += [s[j * SUBLANE:(j + 1) * SUBLANE] for j in range(TK // SUBLANE)]
        mx_sc[...] = jnp.maximum(mx_sc[...], _tree(jnp.maximum, slabs))

    def weight_tiles(kts, ends_sweep):
        m = mx_sc[0:1, :].astype(bf16)
        total = None
        for kt in kts:
            p = jnp.exp2(s_sc[kt] - m)
            d = _dot(vs_ref[0, 0, kt], p)
            total = d if total is None else total + d
        acc_sc[...] += total

    def sweep(n, tiles_fn):
        def trip(j, carry):
            tiles_fn([SWEEP_UNROLL * j + u for u in range(SWEEP_UNROLL)], False)
            return carry

        trips = (n - 1) // SWEEP_UNROLL
        lax.fori_loop(0, trips, trip, 0)
        done = trips * SWEEP_UNROLL
        for left in range(1, SWEEP_UNROLL + 1):
            @pl.when(n - done == left)
            def _(left=left):
                tiles_fn([done + u for u in range(left)], True)

    sweep(n_tiles, score_tiles)
    mx_sc[...] = jnp.broadcast_to(jnp.max(mx_sc[...], axis=0, keepdims=True), (SUBLANE, cols))
    acc_sc[...] = jnp.zeros((LANE, cols), f32)
    sweep(n_tiles, weight_tiles)
    pv_sel = acc_sc[...]

    outs = []
    for r in range(GROUP):
        def branch(pv):
            x = head_cols(pv, r)
            return x[:HEAD_DIM], x[HEAD_DIM:HEAD_DIM + 1]

        num_c, _ = branch(pv_cmp)
        num_s, den_s = branch(pv_sel)
        num_w, den_w = branch(pv_win)
        gate = lambda br: ga_ref[0, pl.ds(br * N_HEADS + g * GROUP + r, 1), :]
        outs.append((gate(0) * inv_cmp[r]) * num_c + (gate(1) / den_s) * num_s + (gate(2) / den_w) * num_w)
    o_ref[0] = jnp.concatenate(outs, axis=0).T.astype(bf16)


def _attention(q, kcmp, vcmp, ks, vs, kw, vw, ga, mt, wb, n_cmp):
    B, _, _, S = q.shape
    NC = kcmp.shape[2]
    n_sel = S // SEL_BLOCK
    cols = GROUP * TQ
    grid = (B, N_KV, S // TQ)
    keys_g = pl.BlockSpec((1, S, LANE), lambda b, g, i: (b, 0, g))
    const = lambda a: pl.BlockSpec(a.shape, lambda b, g, i: (0,) * a.ndim)
    return pl.pallas_call(
        functools.partial(_attn_kernel, n_cmp=n_cmp, n_sel=n_sel),
        grid=grid,
        in_specs=[
            pl.BlockSpec((1, GROUP, LANE, TQ), lambda b, g, i: (b, g, 0, i)),
            pl.BlockSpec((1, 1, NC, LANE), lambda b, g, i: (b, g, 0, 0)),
            pl.BlockSpec((1, 1, LANE, NC), lambda b, g, i: (b, g, 0, 0)),
            keys_g,
            pl.BlockSpec((1, 1, S // TK, LANE, TK), lambda b, g, i: (b, g, 0, 0, 0)),
            keys_g,
            pl.BlockSpec((1, 1, S // TQ, LANE, TQ), lambda b, g, i: (b, g, 0, 0, 0)),
            pl.BlockSpec((1, LANE, TQ), lambda b, g, i: (b, 0, i)),
            const(mt), const(wb),
        ],
        out_specs=pl.BlockSpec((1, TQ, GROUP * HEAD_DIM), lambda b, g, i: (b, i, g)),
        out_shape=jax.ShapeDtypeStruct((B, S, ATTN_WIDTH), bf16),
        scratch_shapes=[pltpu.VMEM((LANE, cols), bf16),
                        pltpu.VMEM((S // TK, TK, cols), bf16),
                        pltpu.VMEM((SUBLANE, cols), f32),
                        pltpu.VMEM((LANE, cols), f32),
                        pltpu.VMEM((n_sel, TQ), f32),
                        pltpu.VMEM((n_sel, TQ), f32)],
        compiler_params=pltpu.CompilerParams(
            dimension_semantics=("arbitrary", "arbitrary", "arbitrary"),
            vmem_limit_bytes=VMEM_LIMIT),
        name="attention",
    )(q, kcmp, vcmp, ks, vs, kw, vw, ga, mt, wb)


def _out_mlp_kernel(x_ref, ao_ref, pm_ref, gm_ref, wba_ref, wbp_ref, wo_ref, gn_ref,
                    w1_ref, w2_ref, gf_ref, o_ref):
    a = _dot(ao_ref[...], wba_ref[...])
    b = _dot(pm_ref[...], wbp_ref[...])
    gm = gm_ref[...].astype(f32)
    merged = (gm[:, :D_MODEL] * a + gm[:, D_MODEL:] * b).astype(bf16)
    x1 = x_ref[...] + _dot(merged, wo_ref[...])
    h = _rms(x1, gn_ref[...]).astype(bf16)
    f = jnp.square(jnp.maximum(_dot(h, w1_ref[...]), 0.0)).astype(bf16)
    x2 = x1 + _dot(f, w2_ref[...])
    o_ref[...] = _rms(x2, gf_ref[...])


def _out_mlp(x2d, ao, pm, gm, wba, wbp, wo, gn, w1, w2, gf):
    T, D = x2d.shape
    tok = lambda width: pl.BlockSpec((TS_OUT, width), lambda t: (t, 0))
    const = lambda arr: pl.BlockSpec(arr.shape, lambda t: (0, 0), pipeline_mode=pl.Buffered(1))
    return pl.pallas_call(
        _out_mlp_kernel,
        grid=(T // TS_OUT,),
        in_specs=[tok(D), tok(ATTN_WIDTH), tok(POOL_WIDTH), tok(2 * D_MODEL),
                  const(wba), const(wbp), const(wo), const(gn), const(w1), const(w2), const(gf)],
        out_specs=tok(D),
        out_shape=jax.ShapeDtypeStruct((T, D), f32),
        compiler_params=pltpu.CompilerParams(
            dimension_semantics=("arbitrary",), vmem_limit_bytes=VMEM_LIMIT),
        name="out_mlp",
    )(x2d, ao, pm, gm, wba, wbp, wo, gn, w1, w2, gf)


def _rope_tables(pos):
    inv = ROPE_THETA ** (-np.arange(0, ROT_DIM, 2, dtype=np.float64) / ROT_DIM)
    ang = pos.astype(np.float64)[:, None] * inv
    cos, sin = np.cos(ang), np.sin(ang)
    P = pos.shape[0]
    c = np.concatenate([cos, cos, np.ones((P, LANE - ROT_DIM))], axis=1)
    sa = np.concatenate([-sin, np.zeros((P, LANE - ROT_HALF))], axis=1)
    sb = np.concatenate([np.zeros((P, ROT_HALF)), sin, np.zeros((P, LANE - ROT_DIM))], axis=1)
    return tuple(jnp.asarray(t, dtype=f32) for t in (c, sa, sb))


def _cmp_to_sel_t(S):
    n_cmp = (S - CMP_BLOCK) // CMP_STRIDE + 1
    n_sel = S // SEL_BLOCK
    cs = np.arange(n_cmp)[:, None] * CMP_STRIDE
    js = np.arange(n_sel)[None, :] * SEL_BLOCK
    ov = np.clip(np.minimum(cs + CMP_BLOCK, js + SEL_BLOCK) - np.maximum(cs, js), 0, None)
    m = np.zeros((S // CMP_STRIDE, n_sel), np.float32)
    m[:n_cmp] = ov / CMP_STRIDE
    return jnp.asarray(m.T, dtype=bf16), n_cmp


def _window_bias():
    band = WINDOW + TQ
    k = np.arange(band)[:, None]
    r = np.arange(TQ)[None, :]
    cases = [k <= r + case * TQ for case in range(WINDOW // TQ)]
    cases.append((k > r) & (k <= r + WINDOW))
    return jnp.asarray(np.where(np.stack(cases), 0.0, -BIG), dtype=bf16)


def _layout_w_in(w):
    D = w.shape[0]
    w = w.astype(bf16)
    n_gate = N_BRANCH * N_HEADS
    head = w[:, :SEG_U[0]]
    gates = w[:, SEG_U[0]:SEG_U[0] + n_gate]
    rest = w[:, SEG_U[0] + n_gate:]
    out = jnp.concatenate([head, rest, gates, jnp.zeros((D, LANE - n_gate), w.dtype)], axis=1)
    assert out.shape[1] == W_COLS and rest.shape[1] == POOL_WIDTH + 2 * D_MODEL
    return out


def _block_diag_kv(m):
    z = jnp.zeros_like(m)
    return jnp.concatenate([jnp.concatenate([m, z], axis=-1), jnp.concatenate([z, m], axis=-1)], axis=-2)


def kernel(x, norm_mix, w_in, cmp_pe_k, cmp_pe_v, cmp_k_w1, cmp_k_w2, cmp_v_w1, cmp_v_w2,
           w_branch_attn, pool_w, pool_scale, w_branch_pool, w_out, norm_mlp, w_ff1, w_ff2,
           norm_final):
    B, S, D = x.shape
    assert norm_mix.shape[0] == 1, "single-layer block: the final norm is fused into out_mlp"
    assert S % TS_IN == 0 and S % TQ == 0 and (B * S) % TS_OUT == 0 and TS_IN % TK == 0
    assert TK % TQ == 0 and S % TK == 0 and WINDOW % TQ == 0 and N_KV == 2
    assert SWEEP_UNROLL & (SWEEP_UNROLL - 1) == 0
    n_chunk = S // CMP_STRIDE
    rc, rsa, rsb = _rope_tables(np.arange(S))
    crc, crsa, crsb = _rope_tables(np.arange(n_chunk) * CMP_STRIDE + CMP_BLOCK - 1)
    mt, n_cmp = _cmp_to_sel_t(S)
    layer = lambda a: a.reshape(a.shape[1:])
    row = lambda a: a.reshape(1, -1)

    (q, ks, kw, vs, vw, kc, vc, ga, pm, gm) = _in_proj(
        x, row(norm_mix), _layout_w_in(layer(w_in)), rc, rsa, rsb,
        layer(pool_w).astype(bf16), row(pool_scale))

    def first_layer(w1):
        per_token = _block_diag_kv(w1.reshape(CMP_BLOCK, HEAD_DIM, CMP_HIDDEN)).astype(bf16)
        return per_token.reshape(CMP_BLOCK // 2, 2 * N_KV * HEAD_DIM, N_KV * CMP_HIDDEN)

    def second_layer(w2):
        return _block_diag_kv(jnp.concatenate([w2, jnp.zeros_like(w2)], axis=1)).astype(bf16)

    kcmp, vcmp = _compress(
        kc, vc, jnp.tile(layer(cmp_pe_k), (1, N_KV)), jnp.tile(layer(cmp_pe_v), (1, N_KV)),
        first_layer(layer(cmp_k_w1)), second_layer(layer(cmp_k_w2)),
        first_layer(layer(cmp_v_w1)), second_layer(layer(cmp_v_w2)), crc, crsa, crsb)

    ao = _attention(q, kcmp, vcmp, ks, vs, kw, vw, ga, mt, _window_bias(), n_cmp)

    y = _out_mlp(x.reshape(B * S, D), ao.reshape(B * S, ATTN_WIDTH), pm.reshape(B * S, POOL_WIDTH),
                 gm.reshape(B * S, 2 * D_MODEL), layer(w_branch_attn).astype(bf16),
                 layer(w_branch_pool).astype(bf16), layer(w_out).astype(bf16), row(norm_mlp),
                 layer(w_ff1).astype(bf16), layer(w_ff2).astype(bf16), row(norm_final))
    return y.reshape(B, S, D)
```

```python
import functools
import math

import jax
import jax.numpy as jnp
import numpy as np
from jax import lax
from jax.experimental import pallas as pl
from jax.experimental.pallas import tpu as pltpu

f32 = jnp.float32
bf16 = jnp.bfloat16

D_MODEL = 1024
N_HEADS = 8
HEAD_DIM = 64
N_KV = 2
GROUP = N_HEADS // N_KV
ROT_DIM = HEAD_DIM // 4
ROT_HALF = ROT_DIM // 2
ROPE_THETA = 500000.0
CMP_BLOCK = 32
CMP_STRIDE = 16
CMP_RATIO = CMP_BLOCK // CMP_STRIDE
CMP_HIDDEN = 4 * HEAD_DIM
SEL_BLOCK = 64
SEL_SHIFT = 6
SEL_TOPK = 16
WINDOW = 512
N_BRANCH = 3
ATTN_WIDTH = N_HEADS * HEAD_DIM
KV_WIDTH = N_KV * HEAD_DIM
POOL_WIDTH = 512
POOL_WINDOWS = (2, 4, 8, 16)
POOL_GROUP = POOL_WIDTH // len(POOL_WINDOWS)
POOL_HALO = 16
D_FF = 4 * D_MODEL
EPS = 1e-6
NEG_INF = -1e30
FORCE_SCORE = 1e4
BIG = 2.0 ** 100
M_INIT = -3.0e38
Q_SCALE = HEAD_DIM ** -0.5 * math.log2(math.e)

LANE = 128
SUBLANE = 8
VMEM_LIMIT = 56 * 1024 * 1024

SEG_Q = (0, ATTN_WIDTH)
SEG_KC = (SEG_Q[1], SEG_Q[1] + KV_WIDTH)
SEG_VC = (SEG_KC[1], SEG_KC[1] + KV_WIDTH)
SEG_KS = (SEG_VC[1], SEG_VC[1] + KV_WIDTH)
SEG_VS = (SEG_KS[1], SEG_KS[1] + KV_WIDTH)
SEG_KW = (SEG_VS[1], SEG_VS[1] + KV_WIDTH)
SEG_VW = (SEG_KW[1], SEG_KW[1] + KV_WIDTH)
SEG_U = (SEG_VW[1], SEG_VW[1] + POOL_WIDTH)
SEG_GM = (SEG_U[1], SEG_U[1] + 2 * D_MODEL)
SEG_GA = (SEG_GM[1], SEG_GM[1] + LANE)
W_COLS = SEG_GA[1]

TS_IN = 512
TQ = 256
TK = 256
SWEEP_UNROLL = 8
TS_OUT = 512


def _dot(a, b):
    return jnp.dot(a, b, preferred_element_type=f32)


def _swap_halves(x):
    return pltpu.roll(x, HEAD_DIM, axis=1)


def _rope_lane_tile(x, c, sa, sb):
    return (x * c + pltpu.roll(x, LANE - ROT_HALF, axis=1) * sa
            + pltpu.roll(x, ROT_HALF, axis=1) * sb)


def _rms(x, g):
    ms = jnp.mean(x * x, axis=-1, keepdims=True)
    return x * lax.rsqrt(ms + EPS) * g


def _lane_chunks(s):
    return [s[:, c * LANE:(c + 1) * LANE] for c in range(s.shape[1] // LANE)]


def _tree(fn, items):
    items = list(items)
    while len(items) > 1:
        items = [fn(items[j], items[j + 1]) if j + 1 < len(items) else items[j]
                 for j in range(0, len(items), 2)]
    return items[0]


def _in_proj_kernel(x_ref, g_ref, w_ref, rc_ref, rsa_ref, rsb_ref, pw_ref, ps_ref,
                    q_ref, ks_ref, kw_ref, vs_ref, vw_ref, kc_ref, vc_ref, ga_ref, pm_ref, gm_ref,
                    ubuf):
    si = pl.program_id(1)
    t0 = si * TS_IN

    @pl.when(si == 0)
    def _():
        ubuf[0:POOL_HALO, :] = jnp.zeros((POOL_HALO, POOL_WIDTH), f32)

    h = _rms(x_ref[0], g_ref[...]).astype(bf16)

    def seg(s):
        return _dot(h, w_ref[:, s[0]:s[1]])

    u = seg(SEG_U)
    ubuf[POOL_HALO:, :] = u
    tpos1 = (t0 + 1 + lax.broadcasted_iota(jnp.int32, (TS_IN, 1), 0)).astype(f32)

    def pool_steps():
        for gi, w in enumerate(POOL_WINDOWS):
            sl = slice(gi * POOL_GROUP, (gi + 1) * POOL_GROUP)
            acc = u[:, sl]
            for k in range(1, w):
                acc = acc + ubuf[POOL_HALO - k:POOL_HALO - k + TS_IN, sl]
                yield
            cnt = jnp.minimum(tpos1, float(w))
            pooled = (acc / cnt - u[:, sl]).astype(bf16)
            mixed = _dot(pooled, pw_ref[gi]) * ps_ref[:, sl]
            pm_ref[0, :, sl] = mixed.astype(bf16)
            yield
        ubuf[0:POOL_HALO, :] = ubuf[TS_IN:TS_IN + POOL_HALO, :]

    pool = pool_steps()

    def pool_advance(n):
        for _ in range(n):
            next(pool, None)

    rc, rsa, rsb = rc_ref[...], rsa_ref[...], rsb_ref[...]
    row_t = t0 + lax.broadcasted_iota(jnp.int32, (TS_IN, LANE), 0)
    lane = lax.broadcasted_iota(jnp.int32, (TS_IN, LANE), 1)
    lo = lane < HEAD_DIM

    def head_tiles(x):
        return (x, _swap_halves(x))

    q = seg(SEG_Q)
    pool_advance(4)
    for m, chunk in enumerate(_lane_chunks(q)):
        for half, xh in enumerate(head_tiles(chunk)):
            rot = jnp.where(lo, _rope_lane_tile(xh, rc, rsa, rsb) * Q_SCALE, 0.0)
            q_ref[0, 2 * m + half] = rot.astype(bf16).T

    onehot = jnp.where(lane - HEAD_DIM == (row_t >> SEL_SHIFT), BIG, 0.0).astype(f32)
    sel_both = _lane_chunks(seg((SEG_KS[0], SEG_VS[1])))
    pool_advance(2)
    win_both = _lane_chunks(seg((SEG_KW[0], SEG_VW[1])))
    pool_advance(4)
    for gg, (ksh, kwh, vsh, vwh) in enumerate(zip(head_tiles(sel_both[0]), head_tiles(win_both[0]),
                                                   head_tiles(sel_both[1]), head_tiles(win_both[1]))):
        sl = slice(gg * LANE, (gg + 1) * LANE)
        ks_ref[0, :, sl] = jnp.where(lo, _rope_lane_tile(ksh, rc, rsa, rsb), onehot).astype(bf16)
        kw_ref[0, :, sl] = jnp.where(lo, _rope_lane_tile(kwh, rc, rsa, rsb), 0.0).astype(bf16)
        vs_t = jnp.where(lo, vsh, 1.0).astype(bf16).T
        for j in range(TS_IN // TK):
            vs_ref[0, gg, j] = vs_t[:, j * TK:(j + 1) * TK]
        vw_t = jnp.where(lo, vwh, 1.0).astype(bf16).T
        for j in range(TS_IN // TQ):
            vw_ref[0, gg, j] = vw_t[:, j * TQ:(j + 1) * TQ]

    ga_ref[0] = jax.nn.sigmoid(seg(SEG_GA)).T
    merge_logits = seg(SEG_GM)
    for _ in pool:
        pass
    gm_ref[0] = jax.nn.sigmoid(merge_logits).astype(bf16)
    c_both = _lane_chunks(seg((SEG_KC[0], SEG_VC[1])))
    kc_ref[0] = c_both[0]
    vc_ref[0] = c_both[1]


def _in_proj(x, g, w, rc, rsa, rsb, pw, ps):
    B, S, D = x.shape
    grid = (B, S // TS_IN)
    tok = lambda width: pl.BlockSpec((1, TS_IN, width), lambda b, s: (b, s, 0))
    const = lambda shape: pl.BlockSpec(shape, lambda b, s: (0,) * len(shape),
                                       pipeline_mode=pl.Buffered(1))
    tab = pl.BlockSpec((TS_IN, LANE), lambda b, s: (s, 0))
    out_shapes = [
        jax.ShapeDtypeStruct((B, N_HEADS, LANE, S), bf16),
        jax.ShapeDtypeStruct((B, S, N_KV * LANE), bf16),
        jax.ShapeDtypeStruct((B, S, N_KV * LANE), bf16),
        jax.ShapeDtypeStruct((B, N_KV, S // TK, LANE, TK), bf16),
        jax.ShapeDtypeStruct((B, N_KV, S // TQ, LANE, TQ), bf16),
        jax.ShapeDtypeStruct((B, S, KV_WIDTH), f32),
        jax.ShapeDtypeStruct((B, S, KV_WIDTH), f32),
        jax.ShapeDtypeStruct((B, LANE, S), f32),
        jax.ShapeDtypeStruct((B, S, POOL_WIDTH), bf16),
        jax.ShapeDtypeStruct((B, S, 2 * D_MODEL), bf16),
    ]
    out_specs = [
        pl.BlockSpec((1, N_HEADS, LANE, TS_IN), lambda b, s: (b, 0, 0, s)),
        tok(N_KV * LANE), tok(N_KV * LANE),
        pl.BlockSpec((1, N_KV, TS_IN // TK, LANE, TK), lambda b, s: (b, 0, s, 0, 0)),
        pl.BlockSpec((1, N_KV, TS_IN // TQ, LANE, TQ), lambda b, s: (b, 0, s, 0, 0)),
        tok(KV_WIDTH), tok(KV_WIDTH),
        pl.BlockSpec((1, LANE, TS_IN), lambda b, s: (b, 0, s)),
        tok(POOL_WIDTH), tok(2 * D_MODEL),
    ]
    return pl.pallas_call(
        _in_proj_kernel,
        grid=grid,
        in_specs=[tok(D), const((1, D)), const((D, W_COLS)), tab, tab, tab,
                  const((len(POOL_WINDOWS), POOL_GROUP, POOL_GROUP)), const((1, POOL_WIDTH))],
        out_specs=out_specs,
        out_shape=out_shapes,
        scratch_shapes=[pltpu.VMEM((POOL_HALO + TS_IN, POOL_WIDTH), f32)],
        compiler_params=pltpu.CompilerParams(
            dimension_semantics=("arbitrary", "arbitrary"), vmem_limit_bytes=VMEM_LIMIT),
        name="in_proj",
    )(x, g, w, rc, rsa, rsb, pw, ps)


def _compress_kernel(kc_ref, vc_ref, pek_ref, pev_ref, kw1_ref, kw2_ref, vw1_ref, vw2_ref,
                     rc_ref, rsa_ref, rsb_ref, kout_ref, vout_ref):
    n_chunk = kc_ref.shape[1] // CMP_STRIDE

    def hidden(src_ref, pe_ref, w1_ref):
        parts = []
        for part in range(CMP_RATIO):
            acc = None
            for tok in range(0, CMP_STRIDE, 2):
                t = part * CMP_STRIDE + tok
                a = jnp.concatenate(
                    [(src_ref[0, pl.ds(tok + u, n_chunk, stride=CMP_STRIDE), :]
                      + pe_ref[t + u:t + u + 1, :]).astype(bf16) for u in range(2)], axis=1)
                d = _dot(a, w1_ref[t // 2])
                acc = d if acc is None else acc + d
            parts.append(acc)
        pre = parts[0] + pltpu.roll(parts[1], n_chunk - 1, axis=0)
        return jax.nn.gelu(pre, approximate=True).astype(bf16)

    lane = lax.broadcasted_iota(jnp.int32, (n_chunk, LANE), 1)
    lo = lane < HEAD_DIM
    kcmp = _dot(hidden(kc_ref, pek_ref, kw1_ref), kw2_ref[...])
    vcmp = _dot(hidden(vc_ref, pev_ref, vw1_ref), vw2_ref[...])
    for gg in range(N_KV):
        sl = slice(gg * LANE, (gg + 1) * LANE)
        kout_ref[0, gg] = _rope_lane_tile(kcmp[:, sl], rc_ref[...], rsa_ref[...], rsb_ref[...]).astype(bf16)
        vout_ref[0, gg] = jnp.where(lo, vcmp[:, sl], 1.0).T.astype(bf16)


def _compress(kc, vc, pek, pev, kw1, kw2, vw1, vw2, rc, rsa, rsb):
    B, S, W = kc.shape
    NC = S // CMP_STRIDE
    const = lambda shape: pl.BlockSpec(shape, lambda b: (0,) * len(shape))
    src = pl.BlockSpec((1, S, W), lambda b: (b, 0, 0))
    return pl.pallas_call(
        _compress_kernel,
        grid=(B,),
        in_specs=[src, src, const(pek.shape), const(pev.shape), const(kw1.shape), const(kw2.shape),
                  const(vw1.shape), const(vw2.shape), const(rc.shape), const(rsa.shape), const(rsb.shape)],
        out_specs=[pl.BlockSpec((1, N_KV, NC, LANE), lambda b: (b, 0, 0, 0)),
                   pl.BlockSpec((1, N_KV, LANE, NC), lambda b: (b, 0, 0, 0))],
        out_shape=[jax.ShapeDtypeStruct((B, N_KV, NC, LANE), bf16),
                   jax.ShapeDtypeStruct((B, N_KV, LANE, NC), bf16)],
        compiler_params=pltpu.CompilerParams(
            dimension_semantics=("arbitrary",), vmem_limit_bytes=VMEM_LIMIT),
        name="compress",
    )(kc, vc, pek, pev, kw1, kw2, vw1, vw2, rc, rsa, rsb)


def _attn_kernel(q_ref, kc_ref, vc_ref, ks_ref, vs_ref, kw_ref, vw_ref, ga_ref, mt_ref, wb_ref, o_ref,
                 qa_sc, s_sc, mx_sc, acc_sc, imp_sc, sel_sc, *, n_cmp, n_sel):
    g = pl.program_id(1)
    i = pl.program_id(2)
    t0 = i * TQ
    cols = GROUP * TQ
    ncp = kc_ref.shape[2]

    tok = t0 + lax.broadcasted_iota(jnp.int32, (1, TQ), 1)
    tok4 = t0 + (lax.broadcasted_iota(jnp.int32, (1, cols), 1) & (TQ - 1))
    head_cols = lambda a, r: a[:, r * TQ:(r + 1) * TQ]

    q_all = jnp.concatenate([q_ref[0, r] for r in range(GROUP)], axis=1)

    kc = kc_ref[0, 0]
    vc = vc_ref[0, 0]
    cpos = lax.broadcasted_iota(jnp.int32, (ncp, 1), 0)
    cvalid = ((cpos * CMP_STRIDE + (CMP_BLOCK - 1)) <= tok) & (cpos < n_cmp)
    cbias = jnp.where(cvalid, 0.0, NEG_INF).astype(f32)
    any_valid = (tok >= CMP_BLOCK - 1).astype(f32)
    s_cmp = _dot(kc, q_all)
    band = WINDOW + TQ
    w0 = pl.multiple_of(jnp.maximum(t0 - WINDOW, 0), TQ)
    wbias = wb_ref[jnp.minimum(i, WINDOW // TQ)]
    kwin = kw_ref[0, pl.ds(w0, band), :]
    vwin = jnp.concatenate([vw_ref[0, 0, w0 // TQ + j] for j in range(band // TQ)], axis=1)
    s_win = _dot(kwin, q_all).astype(bf16)
    p_win = []
    psum = jnp.zeros((ncp, TQ), f32)
    e_cmp = []
    inv_cmp = []
    for r in range(GROUP):
        s = head_cols(s_win, r) + wbias
        p_win.append(jnp.exp2(s - jnp.max(s, axis=0, keepdims=True)))
        s = head_cols(s_cmp, r) + cbias
        e = jnp.exp2(s - jnp.max(s, axis=0, keepdims=True))
        inv = 1.0 / jnp.sum(e, axis=0, keepdims=True)
        psum = psum + e * inv
        e_cmp.append(e.astype(bf16))
        inv_cmp.append(inv * any_valid)
    pv_win = _dot(vwin, jnp.concatenate(p_win, axis=1))
    pv_cmp = _dot(vc, jnp.concatenate(e_cmp, axis=1))

    mt = mt_ref[...]
    p_hi = psum.astype(bf16)
    rem = psum - p_hi.astype(f32)
    p_mid = rem.astype(bf16)
    p_lo = (rem - p_mid.astype(f32)).astype(bf16)
    imp = _dot(mt, p_hi) + _dot(mt, p_mid) + _dot(mt, p_lo)
    blk = lax.broadcasted_iota(jnp.int32, (n_sel, TQ), 0)
    cur = tok >> SEL_SHIFT
    imp = jnp.where(blk > cur, -FORCE_SCORE, imp)
    imp = jnp.where((blk == 0) | (blk == cur) | (blk == cur - 1), FORCE_SCORE, imp)

    imp_sc[...] = imp
    sel_sc[...] = jnp.full((n_sel, TQ), -1.0, f32)
    sub = lax.broadcasted_iota(jnp.int32, (SUBLANE, TQ), 0)
    row_keys = SUBLANE * SEL_BLOCK
    rows_in_play = (t0 + TQ + row_keys - 1) // row_keys
    for n_rows in range(1, n_sel // SUBLANE + 1):
        @pl.when(rows_in_play == n_rows)
        def _(n_rows=n_rows):
            vrows = [imp_sc[v * SUBLANE:(v + 1) * SUBLANE, :] for v in range(n_rows)]
            ahead = [jnp.zeros((SUBLANE, TQ), f32) for _ in range(n_rows)]
            for vj in range(n_rows):
                for sj in range(SUBLANE):
                    rival = jnp.broadcast_to(vrows[vj][sj:sj + 1, :], (SUBLANE, TQ))
                    for v in range(n_rows):
                        if v > vj:
                            ahead[v] = ahead[v] + jnp.where(rival >= vrows[v], 1.0, 0.0)
                        elif v < vj:
                            ahead[v] = ahead[v] + jnp.where(rival > vrows[v], 1.0, 0.0)
                        else:
                            tie = jnp.where(rival == vrows[v], (sub > sj).astype(f32), 0.0)
                            ahead[v] = ahead[v] + jnp.where(rival > vrows[v], 1.0, tie)
            for v in range(n_rows):
                sel_sc[v * SUBLANE:(v + 1) * SUBLANE, :] = jnp.where(
                    ahead[v] < float(min(SEL_TOPK, n_sel)), 0.0, -1.0)

    helper = jnp.concatenate([jnp.zeros((LANE - n_sel, TQ), f32), sel_sc[...]], axis=0).astype(bf16)
    qa_sc[...] = q_all + jnp.concatenate([helper] * GROUP, axis=1)


    n_tiles = (t0 + TQ + TK - 1) // TK
    mx_sc[...] = jnp.full((SUBLANE, cols), M_INIT, f32)

    def score_tiles(kts, ends_sweep):
        slabs = []
        for idx, kt in enumerate(kts):
            k0 = pl.multiple_of(kt * TK, TK)
            s = _dot(ks_ref[0, pl.ds(k0, TK), :], qa_sc[...])
            if ends_sweep and idx == len(kts) - 1:
                kpos = k0 + lax.broadcasted_iota(jnp.int32, (TK, 1), 0)
                s = jnp.where(kpos <= tok4, s, -BIG)
            s_sc[kt] = s.astype(bf16)
            slabs += [s[j * SUBLANE:(j + 1) * SUBLANE] for j in range(TK // SUBLANE)]
        mx_sc[...] = jnp.maximum(mx_sc[...], _tree(jnp.maximum, slabs))

    def weight_tiles(kts, ends_sweep):
        m = mx_sc[0:1, :].astype(bf16)
        total = None
        for kt in kts:
            p = jnp.exp2(s_sc[kt] - m)
            d = _dot(vs_ref[0, 0, kt], p)
            total = d if total is None else total + d
        acc_sc[...] += total

    def sweep(n, tiles_fn):
        def trip(j, carry):
            tiles_fn([SWEEP_UNROLL * j + u for u in range(SWEEP_UNROLL)], False)
            return carry

        trips = (n - 1) // SWEEP_UNROLL
        lax.fori_loop(0, trips, trip, 0)
        done = trips * SWEEP_UNROLL
        for left in range(1, SWEEP_UNROLL + 1):
            @pl.when(n - done == left)
            def _(left=left):
                tiles_fn([done + u for u in range(left)], True)

    sweep(n_tiles, score_tiles)
    mx_sc[...] = jnp.broadcast_to(jnp.max(mx_sc[...], axis=0, keepdims=True), (SUBLANE, cols))
    acc_sc[...] = jnp.zeros((LANE, cols), f32)
    sweep(n_tiles, weight_tiles)
    pv_sel = acc_sc[...]

    outs = []
    for r in range(GROUP):
        def branch(pv):
            x = head_cols(pv, r)
            return x[:HEAD_DIM], x[HEAD_DIM:HEAD_DIM + 1]

        num_c, _ = branch(pv_cmp)
        num_s, den_s = branch(pv_sel)
        num_w, den_w = branch(pv_win)
        gate = lambda br: ga_ref[0, pl.ds(br * N_HEADS + g * GROUP + r, 1), :]
        outs.append((gate(0) * inv_cmp[r]) * num_c + (gate(1) / den_s) * num_s + (gate(2) / den_w) * num_w)
    o_ref[0] = jnp.concatenate(outs, axis=0).astype(bf16).T


def _attention(q, kcmp, vcmp, ks, vs, kw, vw, ga, mt, wb, n_cmp):
    B, _, _, S = q.shape
    NC = kcmp.shape[2]
    n_sel = S // SEL_BLOCK
    cols = GROUP * TQ
    grid = (B, N_KV, S // TQ)
    keys_g = pl.BlockSpec((1, S, LANE), lambda b, g, i: (b, 0, g))
    const = lambda a: pl.BlockSpec(a.shape, lambda b, g, i: (0,) * a.ndim)
    return pl.pallas_call(
        functools.partial(_attn_kernel, n_cmp=n_cmp, n_sel=n_sel),
        grid=grid,
        in_specs=[
            pl.BlockSpec((1, GROUP, LANE, TQ), lambda b, g, i: (b, g, 0, i)),
            pl.BlockSpec((1, 1, NC, LANE), lambda b, g, i: (b, g, 0, 0)),
            pl.BlockSpec((1, 1, LANE, NC), lambda b, g, i: (b, g, 0, 0)),
            keys_g,
            pl.BlockSpec((1, 1, S // TK, LANE, TK), lambda b, g, i: (b, g, 0, 0, 0)),
            keys_g,
            pl.BlockSpec((1, 1, S // TQ, LANE, TQ), lambda b, g, i: (b, g, 0, 0, 0)),
            pl.BlockSpec((1, LANE, TQ), lambda b, g, i: (b, 0, i)),
            const(mt), const(wb),
        ],
        out_specs=pl.BlockSpec((1, TQ, GROUP * HEAD_DIM), lambda b, g, i: (b, i, g)),
        out_shape=jax.ShapeDtypeStruct((B, S, ATTN_WIDTH), bf16),
        scratch_shapes=[pltpu.VMEM((LANE, cols), bf16),
                        pltpu.VMEM((S // TK, TK, cols), bf16),
                        pltpu.VMEM((SUBLANE, cols), f32),
                        pltpu.VMEM((LANE, cols), f32),
                        pltpu.VMEM((n_sel, TQ), f32),
                        pltpu.VMEM((n_sel, TQ), f32)],
        compiler_params=pltpu.CompilerParams(
            dimension_semantics=("arbitrary", "arbitrary", "arbitrary"),
            vmem_limit_bytes=VMEM_LIMIT),
        name="attention",
    )(q, kcmp, vcmp, ks, vs, kw, vw, ga, mt, wb)


def _out_mlp_kernel(x_ref, ao_ref, pm_ref, gm_ref, wba_ref, wbp_ref, wo_ref, gn_ref,
                    w1_ref, w2_ref, gf_ref, o_ref):
    a = _dot(ao_ref[...], wba_ref[...])
    b = _dot(pm_ref[...], wbp_ref[...])
    gm = gm_ref[...].astype(f32)
    merged = (gm[:, :D_MODEL] * a + gm[:, D_MODEL:] * b).astype(bf16)
    x1 = x_ref[...] + _dot(merged, wo_ref[...])
    h = _rms(x1, gn_ref[...]).astype(bf16)
    f = jnp.square(jnp.maximum(_dot(h, w1_ref[...]), 0.0)).astype(bf16)
    x2 = x1 + _dot(f, w2_ref[...])
    o_ref[...] = _rms(x2, gf_ref[...])


def _out_mlp(x2d, ao, pm, gm, wba, wbp, wo, gn, w1, w2, gf):
    T, D = x2d.shape
    tok = lambda width: pl.BlockSpec((TS_OUT, width), lambda t: (t, 0))
    const = lambda arr: pl.BlockSpec(arr.shape, lambda t: (0, 0), pipeline_mode=pl.Buffered(1))
    return pl.pallas_call(
        _out_mlp_kernel,
        grid=(T // TS_OUT,),
        in_specs=[tok(D), tok(ATTN_WIDTH), tok(POOL_WIDTH), tok(2 * D_MODEL),
                  const(wba), const(wbp), const(wo), const(gn), const(w1), const(w2), const(gf)],
        out_specs=tok(D),
        out_shape=jax.ShapeDtypeStruct((T, D), f32),
        compiler_params=pltpu.CompilerParams(
            dimension_semantics=("arbitrary",), vmem_limit_bytes=VMEM_LIMIT),
        name="out_mlp",
    )(x2d, ao, pm, gm, wba, wbp, wo, gn, w1, w2, gf)


def _rope_tables(pos):
    inv = ROPE_THETA ** (-np.arange(0, ROT_DIM, 2, dtype=np.float64) / ROT_DIM)
    ang = pos.astype(np.float64)[:, None] * inv
    cos, sin = np.cos(ang), np.sin(ang)
    P = pos.shape[0]
    c = np.concatenate([cos, cos, np.ones((P, LANE - ROT_DIM))], axis=1)
    sa = np.concatenate([-sin, np.zeros((P, LANE - ROT_HALF))], axis=1)
    sb = np.concatenate([np.zeros((P, ROT_HALF)), sin, np.zeros((P, LANE - ROT_DIM))], axis=1)
    return tuple(jnp.asarray(t, dtype=f32) for t in (c, sa, sb))


def _cmp_to_sel_t(S):
    n_cmp = (S - CMP_BLOCK) // CMP_STRIDE + 1
    n_sel = S // SEL_BLOCK
    cs = np.arange(n_cmp)[:, None] * CMP_STRIDE
    js = np.arange(n_sel)[None, :] * SEL_BLOCK
    ov = np.clip(np.minimum(cs + CMP_BLOCK, js + SEL_BLOCK) - np.maximum(cs, js), 0, None)
    m = np.zeros((S // CMP_STRIDE, n_sel), np.float32)
    m[:n_cmp] = ov / CMP_STRIDE
    return jnp.asarray(m.T, dtype=bf16), n_cmp


def _window_bias():
    band = WINDOW + TQ
    k = np.arange(band)[:, None]
    r = np.arange(TQ)[None, :]
    cases = [k <= r + case * TQ for case in range(WINDOW // TQ)]
    cases.append((k > r) & (k <= r + WINDOW))
    return jnp.asarray(np.where(np.stack(cases), 0.0, -BIG), dtype=bf16)


def _layout_w_in(w):
    D = w.shape[0]
    w = w.astype(bf16)
    n_gate = N_BRANCH * N_HEADS
    head = w[:, :SEG_U[0]]
    gates = w[:, SEG_U[0]:SEG_U[0] + n_gate]
    rest = w[:, SEG_U[0] + n_gate:]
    out = jnp.concatenate([head, rest, gates, jnp.zeros((D, LANE - n_gate), w.dtype)], axis=1)
    assert out.shape[1] == W_COLS and rest.shape[1] == POOL_WIDTH + 2 * D_MODEL
    return out


def _block_diag_kv(m):
    z = jnp.zeros_like(m)
    return jnp.concatenate([jnp.concatenate([m, z], axis=-1), jnp.concatenate([z, m], axis=-1)], axis=-2)


def kernel(x, norm_mix, w_in, cmp_pe_k, cmp_pe_v, cmp_k_w1, cmp_k_w2, cmp_v_w1, cmp_v_w2,
           w_branch_attn, pool_w, pool_scale, w_branch_pool, w_out, norm_mlp, w_ff1, w_ff2,
           norm_final):
    B, S, D = x.shape
    assert norm_mix.shape[0] == 1, "single-layer block: the final norm is fused into out_mlp"
    assert S % TS_IN == 0 and S % TQ == 0 and (B * S) % TS_OUT == 0 and TS_IN % TK == 0
    assert TK % TQ == 0 and S % TK == 0 and WINDOW % TQ == 0 and N_KV == 2
    assert SWEEP_UNROLL & (SWEEP_UNROLL - 1) == 0
    n_chunk = S // CMP_STRIDE
    rc, rsa, rsb = _rope_tables(np.arange(S))
    crc, crsa, crsb = _rope_tables(np.arange(n_chunk) * CMP_STRIDE + CMP_BLOCK - 1)
    mt, n_cmp = _cmp_to_sel_t(S)
    layer = lambda a: a.reshape(a.shape[1:])
    row = lambda a: a.reshape(1, -1)

    (q, ks, kw, vs, vw, kc, vc, ga, pm, gm) = _in_proj(
        x, row(norm_mix), _layout_w_in(layer(w_in)), rc, rsa, rsb,
        layer(pool_w).astype(bf16), row(pool_scale))

    def first_layer(w1):
        per_token = _block_diag_kv(w1.reshape(CMP_BLOCK, HEAD_DIM, CMP_HIDDEN)).astype(bf16)
        return per_token.reshape(CMP_BLOCK // 2, 2 * N_KV * HEAD_DIM, N_KV * CMP_HIDDEN)

    def second_layer(w2):
        return _block_diag_kv(jnp.concatenate([w2, jnp.zeros_like(w2)], axis=1)).astype(bf16)

    kcmp, vcmp = _compress(
        kc, vc, jnp.tile(layer(cmp_pe_k), (1, N_KV)), jnp.tile(layer(cmp_pe_v), (1, N_KV)),
        first_layer(layer(cmp_k_w1)), second_layer(layer(cmp_k_w2)),
        first_layer(layer(cmp_v_w1)), second_layer(layer(cmp_v_w2)), crc, crsa, crsb)

    ao = _attention(q, kcmp, vcmp, ks, vs, kw, vw, ga, mt, _window_bias(), n_cmp)

    y = _out_mlp(x.reshape(B * S, D), ao.reshape(B * S, ATTN_WIDTH), pm.reshape(B * S, POOL_WIDTH),
                 gm.reshape(B * S, 2 * D_MODEL), layer(w_branch_attn).astype(bf16),
                 layer(w_branch_pool).astype(bf16), layer(w_out).astype(bf16), row(norm_mlp),
                 layer(w_ff1).astype(bf16), layer(w_ff2).astype(bf16), row(norm_final))
    return y.reshape(B, S, D)
```
